```python
import math
import jax, jax.numpy as jnp
from jax import lax
import numpy as np

D_MODEL = 1024
BATCH = 32
SEQ = 2048
DEPTH = 1

PLE_DIM = 256
RMS_EPS = 1e-6
ROPE_THETA = 10000.0
RET_HEADS = 8
RET_DK = D_MODEL // RET_HEADS
RET_DV = 2 * RET_DK
RET_CHUNK = 128
RET_QK_W = RET_HEADS * RET_DK
RET_V_W = RET_HEADS * RET_DV
NSA_HEADS = 16
NSA_GROUPS = 2
NSA_HPG = NSA_HEADS // NSA_GROUPS
NSA_DH = 64
NSA_Q_W = NSA_HEADS * NSA_DH
NSA_KV_W = NSA_GROUPS * NSA_DH
NSA_GATE_W = 3 * NSA_HEADS
CMP_LEN = 32
CMP_STRIDE = 16
CMP_HIDDEN = 256
SEL_BLOCK = 64
SEL_TOPN = 8
WINDOW = 512
NSA_QBLOCK = 64
FORCE_SCORE = 1e6
NEG_INF = -1e30
MLP_HIDDEN = 4 * D_MODEL
IN_SPLITS = [RET_QK_W, RET_QK_W, RET_V_W, RET_V_W, NSA_Q_W] + [NSA_KV_W] * 6 + [NSA_GATE_W]
D_IN = sum(IN_SPLITS)

kernel_name = "hybrid_retention_nsa_gated_block"


def rms_norm(x, g):
    x32 = x.astype(jnp.float32)
    y = x32 * lax.rsqrt(jnp.mean(x32 * x32, axis=-1, keepdims=True) + RMS_EPS)
    return (y * g.astype(jnp.float32)).astype(x.dtype)


def rope_tables(positions, dim):
    inv = ROPE_THETA ** (-jnp.arange(0, dim, 2, dtype=jnp.float32) / dim)
    ang = positions.astype(jnp.float32)[..., None] * inv
    return jnp.cos(ang)[:, :, None, :], jnp.sin(ang)[:, :, None, :]


def apply_rope(x, cos, sin):
    x32 = x.astype(jnp.float32)
    x1, x2 = jnp.split(x32, 2, axis=-1)
    return jnp.concatenate([x1 * cos - x2 * sin, x2 * cos + x1 * sin], axis=-1).astype(x.dtype)


def masked_softmax(s, mask, axis):
    s32 = jnp.where(mask, s.astype(jnp.float32), NEG_INF)
    return jax.nn.softmax(s32, axis=axis) * mask


def retention(q, k, v, g, gn_g):
    B, S = q.shape[:2]
    C = RET_CHUNK
    N = S // C
    dt = q.dtype
    log_g = jnp.log(1.0 - 2.0 ** (-5.0 - jnp.arange(RET_HEADS, dtype=jnp.float32)))
    idx = jnp.arange(C, dtype=jnp.float32)
    diff = idx[:, None] - idx[None, :]
    decay = jnp.where(diff >= 0, jnp.exp(jnp.maximum(diff, 0.0)[None] * log_g[:, None, None]), 0.0).astype(dt)
    xi = jnp.exp((idx + 1.0)[None] * log_g[:, None]).astype(dt)
    zeta = jnp.exp((C - 1.0 - idx)[None] * log_g[:, None]).astype(dt)
    g_chunk = jnp.exp(C * log_g).astype(dt)
    k = k * (RET_DK ** -0.5)

    def chunks(t, d):
        return t.reshape(B, N, C, RET_HEADS, d).transpose(1, 0, 3, 2, 4)

    qc = chunks(q, RET_DK)
    kc = chunks(k, RET_DK)
    vc = chunks(v.reshape(B, S, RET_HEADS, RET_DV), RET_DV)

    def step(R, inp):
        qi, ki, vi = inp
        inner = jnp.einsum('bhnd,bhmd->bhnm', qi, ki) * decay[None]
        o = (jnp.einsum('bhnm,bhme->bhne', inner, vi)
             + jnp.einsum('bhnd,bhde->bhne', qi, R) * xi[None, :, :, None])
        R = g_chunk[None, :, None, None] * R + jnp.einsum('bhmd,bhme->bhde', ki * zeta[None, :, :, None], vi)
        return R, o

    R0 = jnp.zeros((B, RET_HEADS, RET_DK, RET_DV), dt)
    _, o = lax.scan(step, R0, (qc, kc, vc))
    o32 = o.transpose(1, 0, 3, 2, 4).reshape(B, S, RET_HEADS, RET_DV).astype(jnp.float32)
    mu = jnp.mean(o32, axis=-1, keepdims=True)
    var = jnp.mean(jnp.square(o32 - mu), axis=-1, keepdims=True)
    o32 = ((o32 - mu) * lax.rsqrt(var + RMS_EPS)).reshape(B, S, RET_V_W) * gn_g.astype(jnp.float32)
    return (o32 * jax.nn.silu(g.astype(jnp.float32))).astype(dt)


def compress(k, pe, w1, w2):
    S = k.shape[2]
    n_cmp = (S - CMP_LEN) // CMP_STRIDE + 1
    idx = jnp.arange(n_cmp)[:, None] * CMP_STRIDE + jnp.arange(CMP_LEN)[None, :]
    blk = k[:, :, idx] + pe
    blk = blk.reshape(blk.shape[0], blk.shape[1], n_cmp, CMP_LEN * NSA_DH)
    return jax.nn.gelu(blk @ w1) @ w2


def nsa(q, k_cmp, v_cmp, k_sel, v_sel, k_win, v_win, gates,
        cmp_pe_k, cmp_k_w1, cmp_k_w2, cmp_pe_v, cmp_v_w1, cmp_v_w2):
    B, S = q.shape[:2]
    dt = q.dtype
    G, Hg, dh, QB = NSA_GROUPS, NSA_HPG, NSA_DH, NSA_QBLOCK
    scale = dh ** -0.5
    qg = q.reshape(B, S, G, Hg, dh).transpose(0, 2, 3, 1, 4)
    tr = lambda t: t.transpose(0, 2, 1, 3)
    Kc = compress(tr(k_cmp), cmp_pe_k, cmp_k_w1, cmp_k_w2)
    Vc = compress(tr(v_cmp), cmp_pe_v, cmp_v_w1, cmp_v_w2)
    n_cmp = Kc.shape[2]
    c_start = jnp.arange(n_cmp) * CMP_STRIDE
    c_end = c_start + CMP_LEN - 1
    n_blk = S // SEL_BLOCK
    b_start = jnp.arange(n_blk) * SEL_BLOCK
    overlap = ((c_start[:, None] < b_start[None, :] + SEL_BLOCK)
               & (c_end[:, None] >= b_start[None, :])).astype(jnp.float32)
    n_sel = min(SEL_TOPN, n_blk)
    ks_blocks = tr(k_sel).reshape(B, G, n_blk, SEL_BLOCK, dh)
    vs_blocks = tr(v_sel).reshape(B, G, n_blk, SEL_BLOCK, dh)
    pad = ((0, 0), (0, 0), (WINDOW, 0), (0, 0))
    kw_pad = jnp.pad(tr(k_win), pad)
    vw_pad = jnp.pad(tr(v_win), pad)
    n_qb = S // QB
    q_blocks = qg.reshape(B, G, Hg, n_qb, QB, dh).transpose(3, 0, 1, 2, 4, 5)
    g_blocks = jax.nn.sigmoid(gates).reshape(B, n_qb, QB, 3, G, Hg).transpose(1, 3, 0, 4, 5, 2)
    bi = jnp.arange(B)[:, None, None, None]
    gi = jnp.arange(G)[None, :, None, None]
    blk_ids = jnp.arange(n_blk)

    def block_fn(inp):
        qb, gb, qi = inp
        t = qi * QB + jnp.arange(QB)
        s_c = jnp.einsum('bghqd,bgcd->bghqc', qb, Kc) * scale
        p_c = masked_softmax(s_c, c_end[None, :] <= t[:, None], -1)
        o_c = jnp.einsum('bghqc,bgcd->bghqd', p_c.astype(dt), Vc)
        imp = jnp.einsum('bgqc,cn->bgqn', p_c.sum(axis=2), overlap)
        cur = t // SEL_BLOCK
        forced = ((blk_ids[None] == 0) | (blk_ids[None] == cur[:, None])
                  | (blk_ids[None] == cur[:, None] - 1))
        valid = blk_ids[None] <= cur[:, None]
        score = jnp.where(forced, FORCE_SCORE, jnp.where(valid, imp, -1.0))
        _, sel = lax.top_k(score, n_sel)
        k_g = ks_blocks[bi, gi, sel]
        v_g = vs_blocks[bi, gi, sel]
        kpos = sel[..., None] * SEL_BLOCK + jnp.arange(SEL_BLOCK)
        s_mask = (kpos <= t[None, None, :, None, None])[:, :, None]
        s_s = jnp.einsum('bghqd,bgqnkd->bghqnk', qb, k_g) * scale
        p_s = masked_softmax(s_s, s_mask, (-2, -1))
        o_s = jnp.einsum('bghqnk,bgqnkd->bghqd', p_s.astype(dt), v_g)
        k_w = lax.dynamic_slice_in_dim(kw_pad, qi * QB, WINDOW + QB, axis=2)
        v_w = lax.dynamic_slice_in_dim(vw_pad, qi * QB, WINDOW + QB, axis=2)
        wpos = qi * QB - WINDOW + jnp.arange(WINDOW + QB)
        w_mask = ((wpos[None] >= 0) & (wpos[None] <= t[:, None])
                  & (wpos[None] > t[:, None] - WINDOW))
        s_w = jnp.einsum('bghqd,bgkd->bghqk', qb, k_w) * scale
        p_w = masked_softmax(s_w, w_mask, -1)
        o_w = jnp.einsum('bghqk,bgkd->bghqd', p_w.astype(dt), v_w)
        return gb[0][..., None] * o_c + gb[1][..., None] * o_s + gb[2][..., None] * o_w

    out = lax.map(block_fn, (q_blocks, g_blocks, jnp.arange(n_qb)))
    return out.transpose(1, 0, 4, 2, 3, 5).reshape(B, S, NSA_Q_W)


def setup_inputs(seed: int = 0) -> dict:
    key = jax.random.key(seed)
    ks = jax.random.split(key, 24)
    nrm = lambda k, shape, fan_in: jax.random.normal(k, shape, jnp.float32) * (fan_in ** -0.5)
    gain = lambda k, shape: 1.0 + 0.02 * jax.random.normal(k, shape, jnp.float32)
    L = DEPTH
    offs = jax.random.randint(ks[2], (BATCH, 1), 0, 1024, dtype=jnp.int32)
    return {
        "x": jax.random.normal(ks[0], (BATCH, SEQ, D_MODEL), jnp.float32),
        "p": jax.random.normal(ks[1], (DEPTH, BATCH, SEQ, PLE_DIM), jnp.float32),
        "positions": offs + jnp.arange(SEQ, dtype=jnp.int32)[None, :],
        "norm_mix_g": gain(ks[3], (L, D_MODEL)),
        "w_in": nrm(ks[4], (L, D_MODEL, D_IN), D_MODEL),
        "ret_gn_g": gain(ks[5], (L, RET_V_W)),
        "w_ret_o": nrm(ks[6], (L, RET_V_W, D_MODEL), RET_V_W),
        "cmp_pe_k": 0.1 * jax.random.normal(ks[7], (L, CMP_LEN, NSA_DH), jnp.float32),
        "cmp_k_w1": nrm(ks[8], (L, CMP_LEN * NSA_DH, CMP_HIDDEN), CMP_LEN * NSA_DH),
        "cmp_k_w2": nrm(ks[9], (L, CMP_HIDDEN, NSA_DH), CMP_HIDDEN),
        "cmp_pe_v": 0.1 * jax.random.normal(ks[10], (L, CMP_LEN, NSA_DH), jnp.float32),
        "cmp_v_w1": nrm(ks[11], (L, CMP_LEN * NSA_DH, CMP_HIDDEN), CMP_LEN * NSA_DH),
        "cmp_v_w2": nrm(ks[12], (L, CMP_HIDDEN, NSA_DH), CMP_HIDDEN),
        "w_nsa_o": nrm(ks[13], (L, NSA_Q_W, D_MODEL), NSA_Q_W),
        "w_merge_gate": nrm(ks[14], (L, D_MODEL, 2 * D_MODEL), D_MODEL),
        "w_out": nrm(ks[15], (L, D_MODEL, D_MODEL), D_MODEL),
        "norm_mlp_g": gain(ks[16], (L, D_MODEL)),
        "w_mlp_up": nrm(ks[17], (L, D_MODEL, MLP_HIDDEN), D_MODEL),
        "w_mlp_down": nrm(ks[18], (L, MLP_HIDDEN, D_MODEL), MLP_HIDDEN),
        "norm_ple_g": gain(ks[19], (L, D_MODEL)),
        "w_ple_gate": nrm(ks[20], (L, D_MODEL, D_MODEL), D_MODEL),
        "w_ple_proj": nrm(ks[21], (L, PLE_DIM, D_MODEL), PLE_DIM),
        "norm_final_g": gain(ks[22], (D_MODEL,)),
    }


def reference(x, p, positions, norm_mix_g, w_in, ret_gn_g, w_ret_o, cmp_pe_k, cmp_k_w1, cmp_k_w2,
              cmp_pe_v, cmp_v_w1, cmp_v_w2, w_nsa_o, w_merge_gate, w_out, norm_mlp_g, w_mlp_up,
              w_mlp_down, norm_ple_g, w_ple_gate, w_ple_proj, norm_final_g):
    B, S, _ = x.shape
    split_points = np.cumsum(IN_SPLITS)[:-1].tolist()
    cos_r, sin_r = rope_tables(positions, RET_DK)
    cos_n, sin_n = rope_tables(positions, NSA_DH)
    for i in range(DEPTH):
        h = rms_norm(x, norm_mix_g[i])
        proj = h @ w_in[i]
        (rq, rk, rv, rg, nq, kc, vc, ksl, vsl, kw, vw, ngate) = jnp.split(proj, split_points, axis=-1)
        rq = apply_rope(rq.reshape(B, S, RET_HEADS, RET_DK), cos_r, sin_r)
        rk = apply_rope(rk.reshape(B, S, RET_HEADS, RET_DK), cos_r, sin_r)
        o_ret = retention(rq, rk, rv, rg, ret_gn_g[i]) @ w_ret_o[i]
        kvr = lambda t: t.reshape(B, S, NSA_GROUPS, NSA_DH)
        nq = apply_rope(nq.reshape(B, S, NSA_HEADS, NSA_DH), cos_n, sin_n)
        kc = apply_rope(kvr(kc), cos_n, sin_n)
        ksl = apply_rope(kvr(ksl), cos_n, sin_n)
        kw = apply_rope(kvr(kw), cos_n, sin_n)
        o_nsa = nsa(nq, kc, kvr(vc), ksl, kvr(vsl), kw, kvr(vw), ngate,
                    cmp_pe_k[i], cmp_k_w1[i], cmp_k_w2[i], cmp_pe_v[i], cmp_v_w1[i], cmp_v_w2[i]) @ w_nsa_o[i]
        g_ret, g_nsa = jnp.split(jax.nn.sigmoid(h @ w_merge_gate[i]), 2, axis=-1)
        x = x + (g_ret * o_ret + g_nsa * o_nsa) @ w_out[i]
        h2 = rms_norm(x, norm_mlp_g[i])
        x = x + jnp.square(jax.nn.relu(h2 @ w_mlp_up[i])) @ w_mlp_down[i]
        x = x + (p[i] @ w_ple_proj[i]) * jax.nn.sigmoid(rms_norm(x, norm_ple_g[i]) @ w_ple_gate[i])
    return rms_norm(x, norm_final_g)
```

```python
import functools
import math

import jax
import jax.numpy as jnp
from jax import lax
from jax.experimental import pallas as pl
from jax.experimental.pallas import tpu as pltpu

F32 = jnp.float32
BF16 = jnp.bfloat16

D_MODEL = 1024
PLE_DIM = 256
RMS_EPS = 1e-6
ROPE_THETA = 10000.0
RET_HEADS = 8
RET_DK = 128
RET_DV = 256
RET_CHUNK = 128
RET_QK_W = RET_HEADS * RET_DK
RET_V_W = RET_HEADS * RET_DV
NSA_HEADS = 16
NSA_GROUPS = 2
NSA_HPG = 8
NSA_DH = 64
NSA_Q_W = NSA_HEADS * NSA_DH
NSA_GQ_W = NSA_HPG * NSA_DH
CMP_LEN = 32
CMP_STRIDE = 16
CMP_HIDDEN = 256
SEL_BLOCK = 64
SEL_TOPN = 8
WINDOW = 512
FORCE_SCORE = 1e6
MLP_HIDDEN = 4 * D_MODEL

LANES = 128
NEG_BIG = -1e30
VMEM_LIMIT = 56 * 1024 * 1024

PROJ_TM = 256
TAIL_TM = 256
NSA_TQ = 256
NSA_KT = 512
NSA_WK = WINDOW + NSA_TQ

C_RQ, C_RK, C_RV, C_RG = 0, 1024, 2048, 4096
C_NQ, C_KV, C_GATE, C_END = 6144, 7168, 7936, 8192
GATE_W = NSA_GROUPS * LANES


def _resident(shape):
    nd = len(shape)
    return pl.BlockSpec(shape, lambda *_: (0,) * nd, pipeline_mode=pl.Buffered(1))


def _rms(x, g):
    return x * lax.rsqrt(jnp.mean(x * x, axis=-1, keepdims=True) + RMS_EPS) * g


def _dot(a, b):
    return jnp.dot(a, b, preferred_element_type=F32)


def _dot_tb(a, b):
    return lax.dot_general(a, b, (((1,), (1,)), ((), ())), preferred_element_type=F32)


def _dot_ta(a, b):
    return lax.dot_general(a, b, (((0,), (0,)), ((), ())), preferred_element_type=F32)


def _proj_kernel(x_ref, pos_ref, g_ref, invr_ref, invn_ref, w_ref,
                 rq_ref, rk_ref, rv_ref, rg_ref, nq_ref, kv_ref, gate_ref):
    tm = x_ref.shape[0]
    h = _rms(x_ref[...], g_ref[...]).astype(BF16)
    pos = pos_ref[...]
    lane = lax.broadcasted_iota(jnp.int32, (tm, LANES), 1)
    ang_r = pos * invr_ref[...]
    cos_r = jnp.cos(ang_r)
    sin_r = jnp.sin(ang_r)
    sin_r = jnp.where(lane < 64, -sin_r, sin_r)
    ang_n = pos * invn_ref[...]
    cos_n = jnp.cos(ang_n)
    sin_n = jnp.sin(ang_n)
    low = (lane & 32) == 0
    sin_n = jnp.where(low, -sin_n, sin_n)

    def rope_r(y):
        return y * cos_r + pltpu.roll(y, 64, 1) * sin_r

    def rope_n(y):
        partner = jnp.where(low, pltpu.roll(y, 96, 1), pltpu.roll(y, 32, 1))
        return y * cos_n + partner * sin_n

    ident = lambda y: y
    k_scale = RET_DK ** -0.5
    q_scale = NSA_DH ** -0.5
    chunk = 512
    for c0 in range(0, C_END, chunk):
        y = _dot(h, w_ref[:, c0:c0 + chunk])
        for j in range(chunk // LANES):
            col = c0 + j * LANES
            piece = y[:, j * LANES:(j + 1) * LANES]
            if col < C_RK:
                rq_ref[:, col - C_RQ:col - C_RQ + LANES] = rope_r(piece).astype(BF16)
            elif col < C_RV:
                rk_ref[:, col - C_RK:col - C_RK + LANES] = (rope_r(piece) * k_scale).astype(BF16)
            elif col < C_RG:
                rv_ref[:, col - C_RV:col - C_RV + LANES] = piece.astype(BF16)
            elif col < C_NQ:
                rg_ref[:, col - C_RG:col - C_RG + LANES] = piece.astype(BF16)
            elif col < C_KV:
                nq_ref[:, col - C_NQ:col - C_NQ + LANES] = (rope_n(piece) * q_scale).astype(BF16)
            elif col < C_GATE:
                is_key = ((col - C_KV) // LANES) % 2 == 0
                f = rope_n if is_key else ident
                kv_ref[:, col - C_KV:col - C_KV + LANES] = f(piece).astype(BF16)
            else:
                gate_ref[:, col - C_GATE:col - C_GATE + LANES] = jax.nn.sigmoid(piece)


def _proj(x2, posf, g_mix, inv_r, inv_n, w_all):
    T = x2.shape[0]
    tm = PROJ_TM
    row = lambda w: pl.BlockSpec((tm, w), lambda i: (i, 0))
    out_shapes = [
        jax.ShapeDtypeStruct((T, RET_QK_W), BF16),
        jax.ShapeDtypeStruct((T, RET_QK_W), BF16),
        jax.ShapeDtypeStruct((T, RET_V_W), BF16),
        jax.ShapeDtypeStruct((T, RET_V_W), BF16),
        jax.ShapeDtypeStruct((T, NSA_Q_W), BF16),
        jax.ShapeDtypeStruct((T, C_GATE - C_KV), BF16),
        jax.ShapeDtypeStruct((T, GATE_W), F32),
    ]
    return pl.pallas_call(
        _proj_kernel,
        grid=(T // tm,),
        in_specs=[row(D_MODEL), row(1), _resident((1, D_MODEL)), _resident((1, LANES)),
                  _resident((1, LANES)), _resident((D_MODEL, C_END))],
        out_specs=[row(RET_QK_W), row(RET_QK_W), row(RET_V_W), row(RET_V_W), row(NSA_Q_W),
                   row(C_GATE - C_KV), row(GATE_W)],
        out_shape=out_shapes,
        compiler_params=pltpu.CompilerParams(dimension_semantics=("arbitrary",),
                                             vmem_limit_bytes=VMEM_LIMIT),
        name="proj",
    )(x2, posf, g_mix, inv_r, inv_n, w_all)


def _compress_kernel(k16_ref, v16_ref, w1k_ref, w2k_ref, pek_ref, w1v_ref, w2v_ref, pev_ref,
                     kc_ref, vc_ref):
    half = CMP_STRIDE * NSA_DH
    for x_ref, w1_ref, w2_ref, pe_ref, o_ref in ((k16_ref, w1k_ref, w2k_ref, pek_ref, kc_ref),
                                                 (v16_ref, w1v_ref, w2v_ref, pev_ref, vc_ref)):
        x = x_ref[0, 0]
        first = _dot(x, w1_ref[0:half, :])
        second = _dot(x, w1_ref[half:2 * half, :])
        pe_term = _dot(pe_ref[...], w1_ref[...])[0:1, :]
        hidden = first + pltpu.roll(second, second.shape[0] - 1, 0) + pe_term
        act = jax.nn.gelu(hidden).astype(BF16)
        o_ref[0, 0] = _dot(act, w2_ref[...]).astype(BF16)


def _compress(k16, v16, w1k, w2k, pek, w1v, w2v, pev):
    B, G, R, W = k16.shape
    blk = pl.BlockSpec((1, 1, R, W), lambda b, g: (b, g, 0, 0))
    oblk = pl.BlockSpec((1, 1, R, LANES), lambda b, g: (b, g, 0, 0))
    out = jax.ShapeDtypeStruct((B, G, R, LANES), BF16)
    wspecs = [_resident(w1k.shape), _resident(w2k.shape), _resident(pek.shape)]
    return pl.pallas_call(
        _compress_kernel,
        grid=(B, G),
        in_specs=[blk, blk] + wspecs + wspecs,
        out_specs=[oblk, oblk],
        out_shape=[out, out],
        compiler_params=pltpu.CompilerParams(dimension_semantics=("arbitrary", "arbitrary"),
                                             vmem_limit_bytes=VMEM_LIMIT),
        name="compress",
    )(k16, v16, w1k, w2k, pek, w1v, w2v, pev)


_RET_LOG_G = [math.log(1.0 - 2.0 ** (-5.0 - h)) for h in range(RET_HEADS)]


def _retention_kernel(q_ref, k_ref, v_ref, g_ref, gn_ref, o_ref, state_ref, decay_ref, xi_ref, zeta_ref):
    n = pl.program_id(1)
    C = RET_CHUNK

    @pl.when((pl.program_id(0) == 0) & (n == 0))
    def _tables():
        r = lax.broadcasted_iota(jnp.int32, (C, C), 0).astype(F32)
        c = lax.broadcasted_iota(jnp.int32, (C, C), 1).astype(F32)
        diff = r - c
        for h in range(RET_HEADS):
            lg = _RET_LOG_G[h]
            decay_ref[h] = jnp.where(diff >= 0, jnp.exp(jnp.maximum(diff, 0.0) * lg), 0.0)
            xi_ref[h] = jnp.exp((r + 1.0) * lg)
            zeta_ref[h] = jnp.exp((C - 1.0 - r) * lg)

    @pl.when(n == 0)
    def _reset():
        state_ref[...] = jnp.zeros_like(state_ref)

    for h in range(RET_HEADS):
        qs = slice(h * RET_DK, (h + 1) * RET_DK)
        vs = slice(h * RET_DV, (h + 1) * RET_DV)
        qh = q_ref[0, :, qs]
        kh = k_ref[0, :, qs]
        vh = v_ref[0, :, vs]
        inner = (_dot_tb(qh, kh) * decay_ref[h]).astype(BF16)
        state = state_ref[h]
        xi = xi_ref[h]
        cross = _dot(qh, state.astype(BF16)) * jnp.concatenate([xi, xi], axis=1)
        o = _dot(inner, vh) + cross
        kz = (kh.astype(F32) * zeta_ref[h]).astype(BF16)
        state_ref[h] = math.exp(C * _RET_LOG_G[h]) * state + _dot_ta(kz, vh)
        mu = jnp.mean(o, axis=-1, keepdims=True)
        d = o - mu
        var = jnp.mean(d * d, axis=-1, keepdims=True)
        y = d * lax.rsqrt(var + RMS_EPS) * gn_ref[:, vs]
        g = g_ref[0, :, vs].astype(F32)
        o_ref[0, :, vs] = (y * (g * jax.nn.sigmoid(g))).astype(BF16)


def _retention(rq, rk, rv, rg, gn_g):
    B, S, _ = rq.shape
    C = RET_CHUNK
    blk = lambda w: pl.BlockSpec((1, C, w), lambda b, n: (b, n, 0))
    return pl.pallas_call(
        _retention_kernel,
        grid=(B, S // C),
        in_specs=[blk(RET_QK_W), blk(RET_QK_W), blk(RET_V_W), blk(RET_V_W), _resident((1, RET_V_W))],
        out_specs=blk(RET_V_W),
        out_shape=jax.ShapeDtypeStruct((B, S, RET_V_W), BF16),
        scratch_shapes=[pltpu.VMEM((RET_HEADS, RET_DK, RET_DV), F32),
                        pltpu.VMEM((RET_HEADS, C, C), F32),
                        pltpu.VMEM((RET_HEADS, C, C), F32),
                        pltpu.VMEM((RET_HEADS, C, C), F32)],
        compiler_params=pltpu.CompilerParams(dimension_semantics=("arbitrary", "arbitrary"),
                                             vmem_limit_bytes=VMEM_LIMIT),
        name="retention",
    )(rq, rk, rv, rg, gn_g)


def _nsa_kernel(q_ref, gate_ref, ksel_ref, vsel_ref, kwin_ref, vwin_ref, kc_ref, vc_ref, e_ref, o_ref,
                acc_e, acc_o, m_e, m_o, l_e, l_o, out_acc, p_buf):
    TQ, KT, WK = NSA_TQ, NSA_KT, NSA_WK
    HP = NSA_HPG // 2
    q0 = pl.program_id(2) * TQ
    q = q_ref[0]
    q2 = jnp.concatenate([q[:, j * LANES:(j + 1) * LANES] for j in range(HP)], axis=0)
    lane_q = lax.broadcasted_iota(jnp.int32, q2.shape, 1)
    zero = jnp.zeros_like(q2)
    q_even = jnp.where(lane_q < NSA_DH, q2, zero)
    q_odd = jnp.where(lane_q >= NSA_DH, q2, zero)
    t_col = q0 + lax.broadcasted_iota(jnp.int32, (TQ, 1), 0)
    lane = lax.broadcasted_iota(jnp.int32, (TQ, LANES), 1)
    even_half = lane < NSA_DH
    rows = [slice(hp * TQ, (hp + 1) * TQ) for hp in range(HP)]
    gates = gate_ref[0]

    def gate_col(branch, hp, odd):
        c = branch * NSA_HPG + hp * 2 + odd
        return gates[:, c:c + 1]

    def emit(branch, first):
        for hp in range(HP):
            le = l_e[rows[hp]]
            lo = l_o[rows[hp]]
            we = gate_col(branch, hp, 0) * jnp.where(le > 0, 1.0 / le, 0.0)
            wo = gate_col(branch, hp, 1) * jnp.where(lo > 0, 1.0 / lo, 0.0)
            contrib = jnp.where(even_half, acc_e[rows[hp]] * we, acc_o[rows[hp]] * wo)
            if first:
                out_acc[rows[hp]] = contrib
            else:
                out_acc[rows[hp]] += contrib

    def single_tile(kk, vv, mask, width):
        for qx, acc, l_ref in ((q_even, acc_e, l_e), (q_odd, acc_o, l_o)):
            s = _dot_tb(qx, kk)
            for hp in range(HP):
                sh = jnp.where(mask, s[rows[hp]], -jnp.inf)
                m = jnp.maximum(jnp.max(sh, axis=-1, keepdims=True), NEG_BIG)
                e = jnp.exp(sh - m)
                l_ref[rows[hp]] = jnp.sum(e, axis=-1, keepdims=True)
                p_buf[rows[hp], 0:width] = e.astype(BF16)
            acc[...] = _dot(p_buf[:, 0:width], vv)

    kc = kc_ref[0, 0]
    vc = vc_ref[0, 0]
    c_end = lane * CMP_STRIDE + (CMP_LEN - 1)
    c_mask = c_end <= t_col
    p_sum = jnp.zeros((TQ, LANES), F32)
    for qx, acc, l_ref in ((q_even, acc_e, l_e), (q_odd, acc_o, l_o)):
        s = _dot_tb(qx, kc)
        for hp in range(HP):
            sh = jnp.where(c_mask, s[rows[hp]], -jnp.inf)
            m = jnp.maximum(jnp.max(sh, axis=-1, keepdims=True), NEG_BIG)
            e = jnp.exp(sh - m)
            l = jnp.sum(e, axis=-1, keepdims=True)
            p = e * jnp.where(l > 0, 1.0 / l, 0.0)
            p_sum = p_sum + p
            p_buf[rows[hp], 0:LANES] = p.astype(BF16)
        acc[...] = _dot(p_buf[:, 0:LANES], vc)
        l_ref[...] = jnp.ones_like(l_ref)
    emit(0, True)

    ci = lax.broadcasted_iota(jnp.int32, (LANES, LANES), 0)
    ni = lax.broadcasted_iota(jnp.int32, (LANES, LANES), 1)
    overlap = ((ci * CMP_STRIDE < ni * SEL_BLOCK + SEL_BLOCK)
               & (ci * CMP_STRIDE + CMP_LEN - 1 >= ni * SEL_BLOCK))
    overlap = jnp.where(overlap, 1.0, 0.0).astype(BF16)
    p_hi = p_sum.astype(BF16)
    p_lo = (p_sum - p_hi.astype(F32)).astype(BF16)
    imp = _dot(p_hi, overlap) + _dot(p_lo, overlap)
    n_blk = ksel_ref.shape[2] // SEL_BLOCK
    cur = t_col >> 6
    forced = (lane == 0) | (lane == cur) | (lane == cur - 1)
    score = jnp.where(forced, FORCE_SCORE, jnp.where(lane <= cur, imp, -1.0))
    rank = jnp.zeros((TQ, LANES), jnp.int32)
    for mblk in range(n_blk):
        col = score[:, mblk:mblk + 1]
        ahead = (col > score) | ((col == score) & (lane > mblk))
        rank = rank + jnp.where(ahead, 1, 0)
    sel = jnp.where((rank < SEL_TOPN) & (lane < n_blk), 1.0, 0.0).astype(BF16)

    m_e[...] = jnp.full_like(m_e, NEG_BIG)
    m_o[...] = jnp.full_like(m_o, NEG_BIG)
    l_e[...] = jnp.zeros_like(l_e)
    l_o[...] = jnp.zeros_like(l_o)
    acc_e[...] = jnp.zeros_like(acc_e)
    acc_o[...] = jnp.zeros_like(acc_o)
    n_tiles = (q0 + TQ + KT - 1) // KT

    def sel_step(kt, carry):
        k0 = pl.multiple_of(kt * KT, KT)
        kk = ksel_ref[0, 0, pl.ds(k0, KT), :]
        vv = vsel_ref[0, 0, pl.ds(k0, KT), :]
        sel_keys = _dot(sel, e_ref[:, pl.ds(k0, KT)])
        kidx = k0 + lax.broadcasted_iota(jnp.int32, (TQ, KT), 1)
        mask = (sel_keys > 0.5) & (kidx <= t_col)
        for qx, acc, m_ref, l_ref in ((q_even, acc_e, m_e, l_e), (q_odd, acc_o, m_o, l_o)):
            s = _dot_tb(qx, kk)
            for hp in range(HP):
                sh = jnp.where(mask, s[rows[hp]], -jnp.inf)
                m_old = m_ref[rows[hp]]
                m_new = jnp.maximum(m_old, jnp.max(sh, axis=-1, keepdims=True))
                alpha = jnp.exp(m_old - m_new)
                e = jnp.exp(sh - m_new)
                l_ref[rows[hp]] = alpha * l_ref[rows[hp]] + jnp.sum(e, axis=-1, keepdims=True)
                acc[rows[hp]] = alpha * acc[rows[hp]]
                m_ref[rows[hp]] = m_new
                p_buf[rows[hp], 0:KT] = e.astype(BF16)
            acc[...] += _dot(p_buf[:, 0:KT], vv)
        return carry

    lax.fori_loop(0, n_tiles, sel_step, 0)
    emit(1, False)

    w0 = pl.multiple_of(jnp.maximum(q0 - WINDOW, 0), TQ)
    kk = kwin_ref[0, 0, pl.ds(w0, WK), :]
    vv = vwin_ref[0, 0, pl.ds(w0, WK), :]
    kidx = w0 + lax.broadcasted_iota(jnp.int32, (TQ, WK), 1)
    w_mask = (kidx <= t_col) & (kidx > t_col - WINDOW)
    single_tile(kk, vv, w_mask, WK)
    emit(2, False)

    o_ref[0] = jnp.concatenate([out_acc[rows[hp]] for hp in range(HP)], axis=1).astype(BF16)


def _nsa(nq, gates, ksel, vsel, kwin, vwin, kc, vc, e_mat):
    B, S, _ = nq.shape
    G = NSA_GROUPS
    TQ = NSA_TQ
    rows = (NSA_HPG // 2) * TQ
    qblk = pl.BlockSpec((1, TQ, NSA_GQ_W), lambda b, g, t: (b, t, g))
    gblk = pl.BlockSpec((1, TQ, LANES), lambda b, g, t: (b, t, g))
    kvblk = pl.BlockSpec((1, 1, S, LANES), lambda b, g, t: (b, g, 0, 0))
    cblk = pl.BlockSpec((1, 1, LANES, LANES), lambda b, g, t: (b, g, 0, 0))
    col = pltpu.VMEM((rows, 1), F32)
    wide = pltpu.VMEM((rows, LANES), F32)
    return pl.pallas_call(
        _nsa_kernel,
        grid=(B, G, S // TQ),
        in_specs=[qblk, gblk, kvblk, kvblk, kvblk, kvblk, cblk, cblk, _resident(e_mat.shape)],
        out_specs=qblk,
        out_shape=jax.ShapeDtypeStruct((B, S, NSA_Q_W), BF16),
        scratch_shapes=[wide, wide, col, col, col, col, wide,
                        pltpu.VMEM((rows, max(NSA_KT, NSA_WK)), BF16)],
        compiler_params=pltpu.CompilerParams(dimension_semantics=("arbitrary",) * 3,
                                             vmem_limit_bytes=VMEM_LIMIT),
        name="nsa",
    )(nq, gates, ksel, vsel, kwin, vwin, kc, vc, e_mat)


def _tail_kernel(final, x_ref, yr_ref, yn_ref, p_ref, gmix_ref, gmlp_ref, gple_ref, gfin_ref,
                 wmg_ref, wro_ref, wno_ref, wout_ref, wup_ref, wdn_ref, wpg_ref, wpp_ref, o_ref):
    x = x_ref[...]
    h = _rms(x, gmix_ref[...]).astype(BF16)
    o_ret = _dot(yr_ref[...], wro_ref[...])
    o_nsa = _dot(yn_ref[...], wno_ref[...])
    g_ret = jax.nn.sigmoid(_dot(h, wmg_ref[:, 0:D_MODEL]))
    g_nsa = jax.nn.sigmoid(_dot(h, wmg_ref[:, D_MODEL:2 * D_MODEL]))
    mix = (g_ret * o_ret + g_nsa * o_nsa).astype(BF16)
    x = x + _dot(mix, wout_ref[...])
    h2 = _rms(x, gmlp_ref[...]).astype(BF16)
    mlp = jnp.zeros_like(x)
    step = 1024
    for c0 in range(0, MLP_HIDDEN, step):
        up = jnp.maximum(_dot(h2, wup_ref[:, c0:c0 + step]), 0.0)
        mlp = mlp + _dot((up * up).astype(BF16), wdn_ref[c0:c0 + step, :])
    x = x + mlp
    h3 = _rms(x, gple_ref[...]).astype(BF16)
    ple_gate = jax.nn.sigmoid(_dot(h3, wpg_ref[...]))
    x = x + _dot(p_ref[...].astype(BF16), wpp_ref[...]) * ple_gate
    if final:
        x = _rms(x, gfin_ref[...])
    o_ref[...] = x


def _tail(final, x2, y_ret, y_nsa, p2, g_mix, g_mlp, g_ple, g_fin, w_mg, w_ro, w_no, w_out, w_up, w_dn,
          w_pg, w_pp):
    T = x2.shape[0]
    tm = TAIL_TM
    row = lambda w: pl.BlockSpec((tm, w), lambda i: (i, 0))
    gains = [_resident((1, D_MODEL))] * 4
    weights = [_resident(w.shape) for w in (w_mg, w_ro, w_no, w_out, w_up, w_dn, w_pg, w_pp)]
    return pl.pallas_call(
        functools.partial(_tail_kernel, final),
        grid=(T // tm,),
        in_specs=[row(D_MODEL), row(RET_V_W), row(NSA_Q_W), row(PLE_DIM)] + gains + weights,
        out_specs=row(D_MODEL),
        out_shape=jax.ShapeDtypeStruct((T, D_MODEL), F32),
        compiler_params=pltpu.CompilerParams(dimension_semantics=("arbitrary",),
                                             vmem_limit_bytes=VMEM_LIMIT),
        name="tail",
    )(x2, y_ret, y_nsa, p2, g_mix, g_mlp, g_ple, g_fin, w_mg, w_ro, w_no, w_out, w_up, w_dn, w_pg, w_pp)


def _pack_w_in(w):
    gate = w[:, C_GATE:C_GATE + 3 * NSA_HEADS]
    parts = [w[:, :C_GATE]]
    for g in range(NSA_GROUPS):
        cols = [j * NSA_HEADS + g * NSA_HPG + r for j in range(3) for r in range(NSA_HPG)]
        parts.append(jnp.pad(gate[:, jnp.array(cols)], ((0, 0), (0, LANES - len(cols)))))
    return jnp.concatenate(parts, axis=1).astype(BF16)


def kernel(x, p, positions, norm_mix_g, w_in, ret_gn_g, w_ret_o, cmp_pe_k, cmp_k_w1, cmp_k_w2, cmp_pe_v, cmp_v_w1, cmp_v_w2, w_nsa_o, w_merge_gate, w_out, norm_mlp_g, w_mlp_up, w_mlp_down, norm_ple_g, w_ple_gate, w_ple_proj, norm_final_g):
    B, S, D = x.shape
    depth = p.shape[0]
    T = B * S
    G = NSA_GROUPS
    bf = lambda a: a.astype(BF16)
    row = lambda a: a.reshape(1, -1)
    posf = positions.reshape(T, 1).astype(F32)
    inv_r = ROPE_THETA ** (-jnp.arange(0, RET_DK, 2, dtype=F32) / RET_DK)
    inv_n = ROPE_THETA ** (-jnp.arange(0, NSA_DH, 2, dtype=F32) / NSA_DH)
    inv_r = jnp.tile(inv_r, LANES // inv_r.shape[0]).reshape(1, LANES)
    inv_n = jnp.tile(inv_n, LANES // inv_n.shape[0]).reshape(1, LANES)
    key_blk = jnp.arange(S, dtype=jnp.int32)[None, :] // SEL_BLOCK
    e_mat = bf(key_blk == jnp.arange(LANES, dtype=jnp.int32)[:, None])
    n_rows = S // CMP_STRIDE

    def per_group(t):
        return t.reshape(B, S, G, NSA_DH).transpose(0, 2, 1, 3)

    def dup(t):
        t = per_group(t)
        return jnp.concatenate([t, t], axis=-1)

    def strides(t):
        return per_group(t).reshape(B, G, n_rows, CMP_STRIDE * NSA_DH)

    def pe_rows(pe):
        return jnp.broadcast_to(bf(pe).reshape(1, -1), (8, CMP_LEN * NSA_DH))

    dup_cols = lambda w: bf(jnp.concatenate([w, w], axis=1))

    x2 = x.reshape(T, D)
    for i in range(depth):
        rq, rk, rv, rg, nq, kv, gates = _proj(x2, posf, row(norm_mix_g[i]), inv_r, inv_n, _pack_w_in(w_in[i]))
        piece = lambda j: kv[:, j * LANES:(j + 1) * LANES]
        kc, vc = _compress(strides(piece(0)), strides(piece(1)),
                           bf(cmp_k_w1[i]), dup_cols(cmp_k_w2[i]), pe_rows(cmp_pe_k[i]),
                           bf(cmp_v_w1[i]), dup_cols(cmp_v_w2[i]), pe_rows(cmp_pe_v[i]))
        sh3 = lambda a: a.reshape(B, S, a.shape[-1])
        y_ret = _retention(sh3(rq), sh3(rk), sh3(rv), sh3(rg), row(ret_gn_g[i]))
        y_nsa = _nsa(sh3(nq), sh3(gates), dup(piece(2)), dup(piece(3)), dup(piece(4)), dup(piece(5)),
                     kc, vc, e_mat)
        x2 = _tail(i == depth - 1, x2, y_ret.reshape(T, RET_V_W), y_nsa.reshape(T, NSA_Q_W),
                   p[i].reshape(T, PLE_DIM), row(norm_mix_g[i]), row(norm_mlp_g[i]), row(norm_ple_g[i]),
                   row(norm_final_g), bf(w_merge_gate[i]), bf(w_ret_o[i]), bf(w_nsa_o[i]), bf(w_out[i]),
                   bf(w_mlp_up[i]), bf(w_mlp_down[i]), bf(w_ple_gate[i]), bf(w_ple_proj[i]))
    return x2.reshape(B, S, D)
```

```python
import functools
import math

import jax
import jax.numpy as jnp
from jax import lax
from jax.experimental import pallas as pl
from jax.experimental.pallas import tpu as pltpu

F32 = jnp.float32
BF16 = jnp.bfloat16

D_MODEL = 1024
PLE_DIM = 256
RMS_EPS = 1e-6
ROPE_THETA = 10000.0
RET_HEADS = 8
RET_DK = 128
RET_DV = 256
RET_CHUNK = 128
RET_QK_W = RET_HEADS * RET_DK
RET_V_W = RET_HEADS * RET_DV
NSA_HEADS = 16
NSA_GROUPS = 2
NSA_HPG = 8
NSA_DH = 64
NSA_Q_W = NSA_HEADS * NSA_DH
NSA_GQ_W = NSA_HPG * NSA_DH
CMP_LEN = 32
CMP_STRIDE = 16
CMP_HIDDEN = 256
SEL_BLOCK = 64
SEL_TOPN = 8
WINDOW = 512
FORCE_SCORE = 1e6
MLP_HIDDEN = 4 * D_MODEL

LANES = 128
MASK_BIAS = -1e30
MAX_FLOOR = -1e29
LOG2E = math.log2(math.e)
VMEM_LIMIT = 56 * 1024 * 1024

PROJ_TM = 256
TAIL_TM = 256
NSA_TQ = 256
NSA_KT = 512
NSA_WK = WINDOW + NSA_TQ

C_RQ, C_RK, C_RV, C_RG = 0, 1024, 2048, 4096
C_NQ, C_KV, C_GATE, C_END = 6144, 7168, 7936, 8192
GATE_W = NSA_GROUPS * LANES


def _resident(shape):
    nd = len(shape)
    return pl.BlockSpec(shape, lambda *_: (0,) * nd, pipeline_mode=pl.Buffered(1))


def _rms(x, g):
    return x * lax.rsqrt(jnp.mean(x * x, axis=-1, keepdims=True) + RMS_EPS) * g


def _dot(a, b):
    return jnp.dot(a, b, preferred_element_type=F32)


def _dot_tb(a, b):
    return lax.dot_general(a, b, (((1,), (1,)), ((), ())), preferred_element_type=F32)


def _dot_ta(a, b):
    return lax.dot_general(a, b, (((0,), (0,)), ((), ())), preferred_element_type=F32)


def _proj_kernel(x_ref, pos_ref, g_ref, invr_ref, invn_ref, w_ref,
                 rq_ref, rk_ref, rv_ref, rg_ref, nq_ref, kv_ref, gate_ref):
    tm = x_ref.shape[0]
    h = _rms(x_ref[...], g_ref[...]).astype(BF16)
    pos = pos_ref[...]
    lane = lax.broadcasted_iota(jnp.int32, (tm, LANES), 1)
    ang_r = pos * invr_ref[...]
    cos_r = jnp.cos(ang_r)
    sin_r = jnp.sin(ang_r)
    sin_r = jnp.where(lane < 64, -sin_r, sin_r)
    ang_n = pos * invn_ref[...]
    cos_n = jnp.cos(ang_n)
    sin_n = jnp.sin(ang_n)
    low = (lane & 32) == 0
    sin_n = jnp.where(low, -sin_n, sin_n)

    def rope_r(y):
        return y * cos_r + pltpu.roll(y, 64, 1) * sin_r

    def rope_n(y):
        partner = jnp.where(low, pltpu.roll(y, 96, 1), pltpu.roll(y, 32, 1))
        return y * cos_n + partner * sin_n

    ident = lambda y: y
    k_scale = RET_DK ** -0.5
    q_scale = NSA_DH ** -0.5 * LOG2E
    chunk = 512
    for c0 in range(0, C_END, chunk):
        y = _dot(h, w_ref[:, c0:c0 + chunk])
        for j in range(chunk // LANES):
            col = c0 + j * LANES
            piece = y[:, j * LANES:(j + 1) * LANES]
            if col < C_RK:
                rq_ref[:, col - C_RQ:col - C_RQ + LANES] = rope_r(piece).astype(BF16)
            elif col < C_RV:
                rk_ref[:, col - C_RK:col - C_RK + LANES] = (rope_r(piece) * k_scale).astype(BF16)
            elif col < C_RG:
                rv_ref[:, col - C_RV:col - C_RV + LANES] = piece.astype(BF16)
            elif col < C_NQ:
                rg_ref[:, col - C_RG:col - C_RG + LANES] = piece.astype(BF16)
            elif col < C_KV:
                nq_ref[:, col - C_NQ:col - C_NQ + LANES] = (rope_n(piece) * q_scale).astype(BF16)
            elif col < C_GATE:
                is_key = ((col - C_KV) // LANES) % 2 == 0
                f = rope_n if is_key else ident
                kv_ref[:, col - C_KV:col - C_KV + LANES] = f(piece).astype(BF16)
            else:
                gate_ref[:, col - C_GATE:col - C_GATE + LANES] = jax.nn.sigmoid(piece)


def _proj(x2, posf, g_mix, inv_r, inv_n, w_all):
    T = x2.shape[0]
    tm = PROJ_TM
    row = lambda w: pl.BlockSpec((tm, w), lambda i: (i, 0))
    out_shapes = [
        jax.ShapeDtypeStruct((T, RET_QK_W), BF16),
        jax.ShapeDtypeStruct((T, RET_QK_W), BF16),
        jax.ShapeDtypeStruct((T, RET_V_W), BF16),
        jax.ShapeDtypeStruct((T, RET_V_W), BF16),
        jax.ShapeDtypeStruct((T, NSA_Q_W), BF16),
        jax.ShapeDtypeStruct((T, C_GATE - C_KV), BF16),
        jax.ShapeDtypeStruct((T, GATE_W), F32),
    ]
    return pl.pallas_call(
        _proj_kernel,
        grid=(T // tm,),
        in_specs=[row(D_MODEL), row(1), _resident((1, D_MODEL)), _resident((1, LANES)),
                  _resident((1, LANES)), _resident((D_MODEL, C_END))],
        out_specs=[row(RET_QK_W), row(RET_QK_W), row(RET_V_W), row(RET_V_W), row(NSA_Q_W),
                   row(C_GATE - C_KV), row(GATE_W)],
        out_shape=out_shapes,
        compiler_params=pltpu.CompilerParams(dimension_semantics=("arbitrary",),
                                             vmem_limit_bytes=VMEM_LIMIT),
        name="proj",
    )(x2, posf, g_mix, inv_r, inv_n, w_all)


def _compress_kernel(k16_ref, v16_ref, w1k_ref, w2k_ref, pek_ref, w1v_ref, w2v_ref, pev_ref,
                     kc_ref, vc_ref):
    half = CMP_STRIDE * NSA_DH
    for x_ref, w1_ref, w2_ref, pe_ref, o_ref in ((k16_ref, w1k_ref, w2k_ref, pek_ref, kc_ref),
                                                 (v16_ref, w1v_ref, w2v_ref, pev_ref, vc_ref)):
        x = x_ref[0, 0]
        first = _dot(x, w1_ref[0:half, :])
        second = _dot(x, w1_ref[half:2 * half, :])
        pe_term = _dot(pe_ref[...], w1_ref[...])[0:1, :]
        hidden = first + pltpu.roll(second, second.shape[0] - 1, 0) + pe_term
        act = jax.nn.gelu(hidden).astype(BF16)
        o_ref[0, 0] = _dot(act, w2_ref[...]).astype(BF16)


def _compress(k16, v16, w1k, w2k, pek, w1v, w2v, pev):
    B, G, R, W = k16.shape
    blk = pl.BlockSpec((1, 1, R, W), lambda b, g: (b, g, 0, 0))
    oblk = pl.BlockSpec((1, 1, R, LANES), lambda b, g: (b, g, 0, 0))
    out = jax.ShapeDtypeStruct((B, G, R, LANES), BF16)
    wspecs = [_resident(w1k.shape), _resident(w2k.shape), _resident(pek.shape)]
    return pl.pallas_call(
        _compress_kernel,
        grid=(B, G),
        in_specs=[blk, blk] + wspecs + wspecs,
        out_specs=[oblk, oblk],
        out_shape=[out, out],
        compiler_params=pltpu.CompilerParams(dimension_semantics=("arbitrary", "arbitrary"),
                                             vmem_limit_bytes=VMEM_LIMIT),
        name="compress",
    )(k16, v16, w1k, w2k, pek, w1v, w2v, pev)


_RET_LOG_G = [math.log(1.0 - 2.0 ** (-5.0 - h)) for h in range(RET_HEADS)]


def _retention_kernel(q_ref, k_ref, v_ref, g_ref, gn_ref, o_ref, state_ref, decay_ref, xi_ref, zeta_ref):
    n = pl.program_id(1)
    C = RET_CHUNK

    @pl.when((pl.program_id(0) == 0) & (n == 0))
    def _tables():
        r = lax.broadcasted_iota(jnp.int32, (C, C), 0).astype(F32)
        c = lax.broadcasted_iota(jnp.int32, (C, C), 1).astype(F32)
        diff = r - c
        for h in range(RET_HEADS):
            lg = _RET_LOG_G[h]
            decay_ref[h] = jnp.where(diff >= 0, jnp.exp(jnp.maximum(diff, 0.0) * lg), 0.0)
            xi_ref[h] = jnp.exp((r + 1.0) * lg)
            zeta_ref[h] = jnp.exp((C - 1.0 - r) * lg)

    @pl.when(n == 0)
    def _reset():
        state_ref[...] = jnp.zeros_like(state_ref)

    for h in range(RET_HEADS):
        qs = slice(h * RET_DK, (h + 1) * RET_DK)
        vs = slice(h * RET_DV, (h + 1) * RET_DV)
        qh = q_ref[0, :, qs]
        kh = k_ref[0, :, qs]
        vh = v_ref[0, :, vs]
        inner = (_dot_tb(qh, kh) * decay_ref[h]).astype(BF16)
        state = state_ref[h]
        xi = xi_ref[h]
        cross = _dot(qh, state.astype(BF16)) * jnp.concatenate([xi, xi], axis=1)
        o = _dot(inner, vh) + cross
        kz = (kh.astype(F32) * zeta_ref[h]).astype(BF16)
        state_ref[h] = math.exp(C * _RET_LOG_G[h]) * state + _dot_ta(kz, vh)
        mu = jnp.mean(o, axis=-1, keepdims=True)
        d = o - mu
        var = jnp.mean(d * d, axis=-1, keepdims=True)
        y = d * lax.rsqrt(var + RMS_EPS) * gn_ref[:, vs]
        g = g_ref[0, :, vs].astype(F32)
        o_ref[0, :, vs] = (y * (g * jax.nn.sigmoid(g))).astype(BF16)


def _retention(rq, rk, rv, rg, gn_g):
    B, S, _ = rq.shape
    C = RET_CHUNK
    blk = lambda w: pl.BlockSpec((1, C, w), lambda b, n: (b, n, 0))
    return pl.pallas_call(
        _retention_kernel,
        grid=(B, S // C),
        in_specs=[blk(RET_QK_W), blk(RET_QK_W), blk(RET_V_W), blk(RET_V_W), _resident((1, RET_V_W))],
        out_specs=blk(RET_V_W),
        out_shape=jax.ShapeDtypeStruct((B, S, RET_V_W), BF16),
        scratch_shapes=[pltpu.VMEM((RET_HEADS, RET_DK, RET_DV), F32),
                        pltpu.VMEM((RET_HEADS, C, C), F32),
                        pltpu.VMEM((RET_HEADS, C, C), F32),
                        pltpu.VMEM((RET_HEADS, C, C), F32)],
        compiler_params=pltpu.CompilerParams(dimension_semantics=("arbitrary", "arbitrary"),
                                             vmem_limit_bytes=VMEM_LIMIT),
        name="retention",
    )(rq, rk, rv, rg, gn_g)


NOTSEL_LANE0 = NSA_DH


def _nsa_kernel(q_ref, gate_ref, ksel_ref, vsel_ref, kwin_ref, vwin_ref, kc_ref, vc_ref, o_ref,
                acc_e, acc_o, lp_e, lp_o, out_acc):
    TQ, KT, WK = NSA_TQ, NSA_KT, NSA_WK
    HP = NSA_HPG // 2
    n_blk = ksel_ref.shape[2] // SEL_BLOCK
    q0 = pl.program_id(2) * TQ
    q = q_ref[0].astype(F32)
    q2 = jnp.concatenate([q[:, j * LANES:(j + 1) * LANES] for j in range(HP)], axis=0)
    q2_odd = pltpu.roll(q2, NSA_DH, 1)
    head_lanes = lax.broadcasted_iota(jnp.int32, q2.shape, 1) < NSA_DH
    t_col = q0 + lax.broadcasted_iota(jnp.int32, (TQ, 1), 0)
    lane = lax.broadcasted_iota(jnp.int32, (TQ, LANES), 1)
    even_half = lane < NSA_DH
    rows = [slice(hp * TQ, (hp + 1) * TQ) for hp in range(HP)]
    gates = gate_ref[0]

    def emit(branch, first, acc_pair, inv_pair):
        for hp in range(HP):
            c = branch * NSA_HPG + hp * 2
            we = gates[:, c:c + 1] * inv_pair[0][hp]
            wo = gates[:, c + 1:c + 2] * inv_pair[1][hp]
            contrib = jnp.where(even_half, acc_pair[0][rows[hp]] * we, acc_pair[1][rows[hp]] * wo)
            if first:
                out_acc[rows[hp]] = contrib
            else:
                out_acc[rows[hp]] += contrib

    def safe_inv(l):
        return jnp.where(l > 0, 1.0 / l, 0.0)

    def single_tile(q_pair, kk, bias):
        exps, invs = [], []
        for qx in q_pair:
            s = _dot_tb(qx, kk)
            es, inv_x = [], []
            for hp in range(HP):
                sh = s[rows[hp]] + bias
                m = jnp.maximum(jnp.max(sh, axis=-1, keepdims=True), MAX_FLOOR)
                e = jnp.exp2(sh - m)
                es.append(e)
                inv_x.append(safe_inv(jnp.sum(e, axis=-1, keepdims=True)))
            exps.append(es)
            invs.append(inv_x)
        return exps, invs

    zero_q = jnp.zeros_like(q2)
    q_plain = (jnp.where(head_lanes, q2, zero_q).astype(BF16), jnp.where(head_lanes, q2_odd, zero_q).astype(BF16))
    c_bias = jnp.where(lane * CMP_STRIDE + (CMP_LEN - 1) <= t_col, 0.0, MASK_BIAS)
    es, invs = single_tile(q_plain, kc_ref[0, 0], c_bias)
    vc = vc_ref[0, 0]
    p_sum = jnp.zeros((TQ, LANES), F32)
    cmp_acc = []
    for x in range(2):
        ps = []
        for hp in range(HP):
            p = es[x][hp] * invs[x][hp]
            p_sum = p_sum + p
            ps.append(p.astype(BF16))
        cmp_acc.append(_dot(jnp.concatenate(ps, axis=0), vc))
    ones = [[1.0] * HP] * 2
    emit(0, True, cmp_acc, ones)

    ni = lax.broadcasted_iota(jnp.int32, (LANES, LANES), 0)
    ci = lax.broadcasted_iota(jnp.int32, (LANES, LANES), 1)
    overlap_t = ((ci * CMP_STRIDE < ni * SEL_BLOCK + SEL_BLOCK)
                 & (ci * CMP_STRIDE + CMP_LEN - 1 >= ni * SEL_BLOCK) & (ni < n_blk))
    overlap_t = jnp.where(overlap_t, 1.0, 0.0).astype(BF16)
    p_hi = p_sum.astype(BF16)
    p_lo = (p_sum - p_hi.astype(F32)).astype(BF16)
    imp_t = (_dot_tb(overlap_t, p_hi) + _dot_tb(overlap_t, p_lo))[0:n_blk]
    blk_id = lax.broadcasted_iota(jnp.int32, (n_blk, TQ), 0)
    cur = (q0 + lax.broadcasted_iota(jnp.int32, (n_blk, TQ), 1)) >> 6
    forced = (blk_id == 0) | (blk_id == cur) | (blk_id == cur - 1)
    score = jnp.where(forced, FORCE_SCORE, jnp.where(blk_id <= cur, imp_t, -1.0))
    rank = jnp.zeros((n_blk, TQ), jnp.int32)
    for mblk in range(n_blk):
        other = score[mblk:mblk + 1, :]
        ahead = (other > score) | ((other == score) & (blk_id > mblk))
        rank = rank + jnp.where(ahead, 1, 0)
    not_sel_t = jnp.where(rank < SEL_TOPN, 0.0, 1.0)
    padded = jnp.concatenate([jnp.zeros((NOTSEL_LANE0, TQ), F32), not_sel_t,
                              jnp.zeros((LANES - NOTSEL_LANE0 - n_blk, TQ), F32)], axis=0)
    not_sel = jnp.concatenate([padded.T] * HP, axis=0)
    q_aug = (jnp.where(head_lanes, q2, not_sel).astype(BF16), jnp.where(head_lanes, q2_odd, not_sel).astype(BF16))

    for ref in (acc_e, acc_o, lp_e, lp_o):
        ref[...] = jnp.zeros_like(ref)

    def sel_tile(k0, m_pair, bias):
        kk = ksel_ref[0, 0, pl.ds(k0, KT), :]
        vv = vsel_ref[0, 0, pl.ds(k0, KT), :]
        new_m = []
        for qx, acc, lp, m_all in ((q_aug[0], acc_e, lp_e, m_pair[0]), (q_aug[1], acc_o, lp_o, m_pair[1])):
            s = _dot_tb(qx, kk)
            es, alphas, ms = [], [], []
            for hp in range(HP):
                sh = s[rows[hp]]
                if bias is not None:
                    sh = sh + bias
                m_old = m_all[rows[hp]]
                m_new = jnp.maximum(m_old, jnp.max(sh, axis=-1, keepdims=True))
                alpha = jnp.exp2(m_old - m_new)
                e = jnp.exp2(sh - m_new)
                part = e[:, 0:LANES]
                for j in range(1, KT // LANES):
                    part = part + e[:, j * LANES:(j + 1) * LANES]
                lp[rows[hp]] = alpha * lp[rows[hp]] + part
                es.append(e.astype(BF16))
                alphas.append(alpha)
                ms.append(m_new)
            pv = _dot(jnp.concatenate(es, axis=0), vv)
            for hp in range(HP):
                acc[rows[hp]] = alphas[hp] * acc[rows[hp]] + pv[rows[hp]]
            new_m.append(jnp.concatenate(ms, axis=0))
        return tuple(new_m)

    m_init = jnp.full((HP * TQ, 1), MASK_BIAS, F32)
    n_full = q0 // KT
    m_pair = lax.fori_loop(0, n_full, lambda kt, m: sel_tile(pl.multiple_of(kt * KT, KT), m, None),
                           (m_init, m_init))
    k_diag = pl.multiple_of(n_full * KT, KT)
    kidx = k_diag + lax.broadcasted_iota(jnp.int32, (TQ, KT), 1)
    sel_tile(k_diag, m_pair, jnp.where(kidx <= t_col, 0.0, MASK_BIAS))
    sel_inv = [[safe_inv(jnp.sum(lp[rows[hp]], axis=-1, keepdims=True)) for hp in range(HP)]
               for lp in (lp_e, lp_o)]
    emit(1, False, (acc_e, acc_o), sel_inv)

    w0 = pl.multiple_of(jnp.maximum(q0 - WINDOW, 0), TQ)
    kidx = w0 + lax.broadcasted_iota(jnp.int32, (TQ, WK), 1)
    w_bias = jnp.where((kidx <= t_col) & (kidx > t_col - WINDOW), 0.0, MASK_BIAS)
    es, invs = single_tile(q_aug, kwin_ref[0, 0, pl.ds(w0, WK), :], w_bias)
    vv = vwin_ref[0, 0, pl.ds(w0, WK), :]
    win_acc = [_dot(jnp.concatenate([e.astype(BF16) for e in es[x]], axis=0), vv) for x in range(2)]
    emit(2, False, win_acc, invs)

    o_ref[0] = jnp.concatenate([out_acc[rows[hp]] for hp in range(HP)], axis=1).astype(BF16)


def _nsa(nq, gates, ksel, vsel, kwin, vwin, kc, vc):
    B, S, _ = nq.shape
    G = NSA_GROUPS
    TQ = NSA_TQ
    assert S // SEL_BLOCK <= LANES - NOTSEL_LANE0 and S % NSA_KT == 0 and S >= NSA_WK
    rows = (NSA_HPG // 2) * TQ
    qblk = pl.BlockSpec((1, TQ, NSA_GQ_W), lambda b, g, t: (b, t, g))
    gblk = pl.BlockSpec((1, TQ, LANES), lambda b, g, t: (b, t, g))
    kvblk = pl.BlockSpec((1, 1, S, LANES), lambda b, g, t: (b, g, 0, 0))
    cblk = pl.BlockSpec((1, 1, LANES, LANES), lambda b, g, t: (b, g, 0, 0))
    wide = pltpu.VMEM((rows, LANES), F32)
    return pl.pallas_call(
        _nsa_kernel,
        grid=(B, G, S // TQ),
        in_specs=[qblk, gblk, kvblk, kvblk, kvblk, kvblk, cblk, cblk],
        out_specs=qblk,
        out_shape=jax.ShapeDtypeStruct((B, S, NSA_Q_W), BF16),
        scratch_shapes=[wide] * 5,
        compiler_params=pltpu.CompilerParams(dimension_semantics=("arbitrary",) * 3,
                                             vmem_limit_bytes=VMEM_LIMIT),
        name="nsa",
    )(nq, gates, ksel, vsel, kwin, vwin, kc, vc)


def _tail_kernel(final, x_ref, yr_ref, yn_ref, p_ref, gmix_ref, gmlp_ref, gple_ref, gfin_ref,
                 wmg_ref, wro_ref, wno_ref, wout_ref, wup_ref, wdn_ref, wpg_ref, wpp_ref, o_ref):
    x = x_ref[...]
    h = _rms(x, gmix_ref[...]).astype(BF16)
    o_ret = _dot(yr_ref[...], wro_ref[...])
    o_nsa = _dot(yn_ref[...], wno_ref[...])
    g_ret = jax.nn.sigmoid(_dot(h, wmg_ref[:, 0:D_MODEL]))
    g_nsa = jax.nn.sigmoid(_dot(h, wmg_ref[:, D_MODEL:2 * D_MODEL]))
    mix = (g_ret * o_ret + g_nsa * o_nsa).astype(BF16)
    x = x + _dot(mix, wout_ref[...])
    h2 = _rms(x, gmlp_ref[...]).astype(BF16)
    mlp = jnp.zeros_like(x)
    step = 1024
    for c0 in range(0, MLP_HIDDEN, step):
        up = jnp.maximum(_dot(h2, wup_ref[:, c0:c0 + step]), 0.0)
        mlp = mlp + _dot((up * up).astype(BF16), wdn_ref[c0:c0 + step, :])
    x = x + mlp
    h3 = _rms(x, gple_ref[...]).astype(BF16)
    ple_gate = jax.nn.sigmoid(_dot(h3, wpg_ref[...]))
    x = x + _dot(p_ref[...].astype(BF16), wpp_ref[...]) * ple_gate
    if final:
        x = _rms(x, gfin_ref[...])
    o_ref[...] = x


def _tail(final, x2, y_ret, y_nsa, p2, g_mix, g_mlp, g_ple, g_fin, w_mg, w_ro, w_no, w_out, w_up, w_dn,
          w_pg, w_pp):
    T = x2.shape[0]
    tm = TAIL_TM
    row = lambda w: pl.BlockSpec((tm, w), lambda i: (i, 0))
    gains = [_resident((1, D_MODEL))] * 4
    weights = [_resident(w.shape) for w in (w_mg, w_ro, w_no, w_out, w_up, w_dn, w_pg, w_pp)]
    return pl.pallas_call(
        functools.partial(_tail_kernel, final),
        grid=(T // tm,),
        in_specs=[row(D_MODEL), row(RET_V_W), row(NSA_Q_W), row(PLE_DIM)] + gains + weights,
        out_specs=row(D_MODEL),
        out_shape=jax.ShapeDtypeStruct((T, D_MODEL), F32),
        compiler_params=pltpu.CompilerParams(dimension_semantics=("arbitrary",),
                                             vmem_limit_bytes=VMEM_LIMIT),
        name="tail",
    )(x2, y_ret, y_nsa, p2, g_mix, g_mlp, g_ple, g_fin, w_mg, w_ro, w_no, w_out, w_up, w_dn, w_pg, w_pp)


def _pack_w_in(w):
    gate = w[:, C_GATE:C_GATE + 3 * NSA_HEADS]
    parts = [w[:, :C_GATE]]
    for g in range(NSA_GROUPS):
        cols = [j * NSA_HEADS + g * NSA_HPG + r for j in range(3) for r in range(NSA_HPG)]
        parts.append(jnp.pad(gate[:, jnp.array(cols)], ((0, 0), (0, LANES - len(cols)))))
    return jnp.concatenate(parts, axis=1).astype(BF16)


def kernel(x, p, positions, norm_mix_g, w_in, ret_gn_g, w_ret_o, cmp_pe_k, cmp_k_w1, cmp_k_w2, cmp_pe_v, cmp_v_w1, cmp_v_w2, w_nsa_o, w_merge_gate, w_out, norm_mlp_g, w_mlp_up, w_mlp_down, norm_ple_g, w_ple_gate, w_ple_proj, norm_final_g):
    B, S, D = x.shape
    depth = p.shape[0]
    T = B * S
    G = NSA_GROUPS
    bf = lambda a: a.astype(BF16)
    row = lambda a: a.reshape(1, -1)
    posf = positions.reshape(T, 1).astype(F32)
    inv_r = ROPE_THETA ** (-jnp.arange(0, RET_DK, 2, dtype=F32) / RET_DK)
    inv_n = ROPE_THETA ** (-jnp.arange(0, NSA_DH, 2, dtype=F32) / NSA_DH)
    inv_r = jnp.tile(inv_r, LANES // inv_r.shape[0]).reshape(1, LANES)
    inv_n = jnp.tile(inv_n, LANES // inv_n.shape[0]).reshape(1, LANES)
    n_rows = S // CMP_STRIDE
    key_blk = jnp.arange(S, dtype=jnp.int32)[:, None] // SEL_BLOCK
    blk_lane = jnp.arange(LANES - NSA_DH, dtype=jnp.int32)[None, :]
    sel_tag = bf(jnp.where(key_blk == blk_lane, MASK_BIAS, 0.0))
    sel_tag = jnp.broadcast_to(sel_tag, (B, G, S, LANES - NSA_DH))
    zero_tag = jnp.zeros((B, G, S, LANES - NSA_DH), BF16)

    def per_group(t):
        return t.reshape(B, S, G, NSA_DH).transpose(0, 2, 1, 3)

    def dup(t):
        t = per_group(t)
        return jnp.concatenate([t, t], axis=-1)

    def tagged(t, tag):
        return jnp.concatenate([per_group(t), tag], axis=-1)

    def strides(t):
        return per_group(t).reshape(B, G, n_rows, CMP_STRIDE * NSA_DH)

    def pe_rows(pe):
        return jnp.broadcast_to(bf(pe).reshape(1, -1), (8, CMP_LEN * NSA_DH))

    dup_cols = lambda w: bf(jnp.concatenate([w, w], axis=1))
    pad_cols = lambda w: bf(jnp.concatenate([w, jnp.zeros_like(w)], axis=1))

    x2 = x.reshape(T, D)
    for i in range(depth):
        rq, rk, rv, rg, nq, kv, gates = _proj(x2, posf, row(norm_mix_g[i]), inv_r, inv_n, _pack_w_in(w_in[i]))
        piece = lambda j: kv[:, j * LANES:(j + 1) * LANES]
        kc, vc = _compress(strides(piece(0)), strides(piece(1)),
                           bf(cmp_k_w1[i]), pad_cols(cmp_k_w2[i]), pe_rows(cmp_pe_k[i]),
                           bf(cmp_v_w1[i]), dup_cols(cmp_v_w2[i]), pe_rows(cmp_pe_v[i]))
        sh3 = lambda a: a.reshape(B, S, a.shape[-1])
        y_ret = _retention(sh3(rq), sh3(rk), sh3(rv), sh3(rg), row(ret_gn_g[i]))
        y_nsa = _nsa(sh3(nq), sh3(gates), tagged(piece(2), sel_tag), dup(piece(3)),
                     tagged(piece(4), zero_tag), dup(piece(5)), kc, vc)
        x2 = _tail(i == depth - 1, x2, y_ret.reshape(T, RET_V_W), y_nsa.reshape(T, NSA_Q_W),
                   p[i].reshape(T, PLE_DIM), row(norm_mix_g[i]), row(norm_mlp_g[i]), row(norm_ple_g[i]),
                   row(norm_final_g), bf(w_merge_gate[i]), bf(w_ret_o[i]), bf(w_nsa_o[i]), bf(w_out[i]),
                   bf(w_mlp_up[i]), bf(w_mlp_down[i]), bf(w_ple_gate[i]), bf(w_ple_proj[i]))
    return x2.reshape(B, S, D)
```

```python
import functools
import math

import jax
import jax.numpy as jnp
from jax import lax
from jax.experimental import pallas as pl
from jax.experimental.pallas import tpu as pltpu

F32 = jnp.float32
BF16 = jnp.bfloat16

D_MODEL = 1024
PLE_DIM = 256
RMS_EPS = 1e-6
ROPE_THETA = 10000.0
RET_HEADS = 8
RET_DK = 128
RET_DV = 256
RET_CHUNK = 128
RET_QK_W = RET_HEADS * RET_DK
RET_V_W = RET_HEADS * RET_DV
NSA_HEADS = 16
NSA_GROUPS = 2
NSA_HPG = 8
NSA_DH = 64
NSA_Q_W = NSA_HEADS * NSA_DH
NSA_GQ_W = NSA_HPG * NSA_DH
CMP_LEN = 32
CMP_STRIDE = 16
CMP_HIDDEN = 256
SEL_BLOCK = 64
SEL_TOPN = 8
WINDOW = 512
FORCE_SCORE = 1e6
MLP_HIDDEN = 4 * D_MODEL

LANES = 128
MASK_BIAS = -1e30
MAX_FLOOR = -1e29
LOG2E = math.log2(math.e)
VMEM_LIMIT = 56 * 1024 * 1024

PROJ_TM = 256
TAIL_TM = 256
NSA_TQ = 256
NSA_KT = 512
NSA_WK = WINDOW + NSA_TQ

C_RQ, C_RK, C_RV, C_RG = 0, 1024, 2048, 4096
C_NQ, C_KV, C_GATE, C_END = 6144, 7168, 7936, 8192
GATE_W = NSA_GROUPS * LANES


def _resident(shape):
    nd = len(shape)
    return pl.BlockSpec(shape, lambda *_: (0,) * nd, pipeline_mode=pl.Buffered(1))


def _rms(x, g):
    return x * lax.rsqrt(jnp.mean(x * x, axis=-1, keepdims=True) + RMS_EPS) * g


def _dot(a, b):
    return jnp.dot(a, b, preferred_element_type=F32)


def _dot_tb(a, b):
    return lax.dot_general(a, b, (((1,), (1,)), ((), ())), preferred_element_type=F32)


def _dot_ta(a, b):
    return lax.dot_general(a, b, (((0,), (0,)), ((), ())), preferred_element_type=F32)


def _proj_kernel(x_ref, pos_ref, g_ref, invr_ref, invn_ref, w_ref,
                 rq_ref, rk_ref, rv_ref, rg_ref, nq_ref, kv_ref, gate_ref):
    tm = x_ref.shape[0]
    h = _rms(x_ref[...], g_ref[...]).astype(BF16)
    pos = pos_ref[...]
    lane = lax.broadcasted_iota(jnp.int32, (tm, LANES), 1)
    ang_r = pos * invr_ref[...]
    cos_r = jnp.cos(ang_r)
    sin_r = jnp.sin(ang_r)
    sin_r = jnp.where(lane < 64, -sin_r, sin_r)
    ang_n = pos * invn_ref[...]
    cos_n = jnp.cos(ang_n)
    sin_n = jnp.sin(ang_n)
    low = (lane & 32) == 0
    sin_n = jnp.where(low, -sin_n, sin_n)

    def rope_r(y):
        return y * cos_r + pltpu.roll(y, 64, 1) * sin_r

    def rope_n(y):
        partner = jnp.where(low, pltpu.roll(y, 96, 1), pltpu.roll(y, 32, 1))
        return y * cos_n + partner * sin_n

    ident = lambda y: y
    k_scale = RET_DK ** -0.5
    q_scale = NSA_DH ** -0.5 * LOG2E
    chunk = 512
    for c0 in range(0, C_END, chunk):
        y = _dot(h, w_ref[:, c0:c0 + chunk])
        for j in range(chunk // LANES):
            col = c0 + j * LANES
            piece = y[:, j * LANES:(j + 1) * LANES]
            if col < C_RK:
                rq_ref[:, col - C_RQ:col - C_RQ + LANES] = rope_r(piece).astype(BF16)
            elif col < C_RV:
                rk_ref[:, col - C_RK:col - C_RK + LANES] = (rope_r(piece) * k_scale).astype(BF16)
            elif col < C_RG:
                rv_ref[:, col - C_RV:col - C_RV + LANES] = piece.astype(BF16)
            elif col < C_NQ:
                rg_ref[:, col - C_RG:col - C_RG + LANES] = piece.astype(BF16)
            elif col < C_KV:
                nq_ref[:, col - C_NQ:col - C_NQ + LANES] = (rope_n(piece) * q_scale).astype(BF16)
            elif col < C_GATE:
                is_key = ((col - C_KV) // LANES) % 2 == 0
                f = rope_n if is_key else ident
                kv_ref[:, col - C_KV:col - C_KV + LANES] = f(piece).astype(BF16)
            else:
                gate_ref[:, col - C_GATE:col - C_GATE + LANES] = jax.nn.sigmoid(piece)


def _proj(x2, posf, g_mix, inv_r, inv_n, w_all):
    T = x2.shape[0]
    tm = PROJ_TM
    row = lambda w: pl.BlockSpec((tm, w), lambda i: (i, 0))
    out_shapes = [
        jax.ShapeDtypeStruct((T, RET_QK_W), BF16),
        jax.ShapeDtypeStruct((T, RET_QK_W), BF16),
        jax.ShapeDtypeStruct((T, RET_V_W), BF16),
        jax.ShapeDtypeStruct((T, RET_V_W), BF16),
        jax.ShapeDtypeStruct((T, NSA_Q_W), BF16),
        jax.ShapeDtypeStruct((T, C_GATE - C_KV), BF16),
        jax.ShapeDtypeStruct((T, GATE_W), F32),
    ]
    return pl.pallas_call(
        _proj_kernel,
        grid=(T // tm,),
        in_specs=[row(D_MODEL), row(1), _resident((1, D_MODEL)), _resident((1, LANES)),
                  _resident((1, LANES)), _resident((D_MODEL, C_END))],
        out_specs=[row(RET_QK_W), row(RET_QK_W), row(RET_V_W), row(RET_V_W), row(NSA_Q_W),
                   row(C_GATE - C_KV), row(GATE_W)],
        out_shape=out_shapes,
        compiler_params=pltpu.CompilerParams(dimension_semantics=("arbitrary",),
                                             vmem_limit_bytes=VMEM_LIMIT),
        name="proj",
    )(x2, posf, g_mix, inv_r, inv_n, w_all)


def _compress_kernel(k16_ref, v16_ref, w1k_ref, w2k_ref, pek_ref, w1v_ref, w2v_ref, pev_ref,
                     kc_ref, vc_ref):
    half = CMP_STRIDE * NSA_DH
    for x_ref, w1_ref, w2_ref, pe_ref, o_ref in ((k16_ref, w1k_ref, w2k_ref, pek_ref, kc_ref),
                                                 (v16_ref, w1v_ref, w2v_ref, pev_ref, vc_ref)):
        x = x_ref[0, 0]
        first = _dot(x, w1_ref[0:half, :])
        second = _dot(x, w1_ref[half:2 * half, :])
        pe_term = _dot(pe_ref[...], w1_ref[...])[0:1, :]
        hidden = first + pltpu.roll(second, second.shape[0] - 1, 0) + pe_term
        act = jax.nn.gelu(hidden).astype(BF16)
        o_ref[0, 0] = _dot(act, w2_ref[...]).astype(BF16)


def _compress(k16, v16, w1k, w2k, pek, w1v, w2v, pev):
    B, G, R, W = k16.shape
    blk = pl.BlockSpec((1, 1, R, W), lambda b, g: (b, g, 0, 0))
    oblk = pl.BlockSpec((1, 1, R, LANES), lambda b, g: (b, g, 0, 0))
    out = jax.ShapeDtypeStruct((B, G, R, LANES), BF16)
    wspecs = [_resident(w1k.shape), _resident(w2k.shape), _resident(pek.shape)]
    return pl.pallas_call(
        _compress_kernel,
        grid=(B, G),
        in_specs=[blk, blk] + wspecs + wspecs,
        out_specs=[oblk, oblk],
        out_shape=[out, out],
        compiler_params=pltpu.CompilerParams(dimension_semantics=("arbitrary", "arbitrary"),
                                             vmem_limit_bytes=VMEM_LIMIT),
        name="compress",
    )(k16, v16, w1k, w2k, pek, w1v, w2v, pev)


_RET_LOG_G = [math.log(1.0 - 2.0 ** (-5.0 - h)) for h in range(RET_HEADS)]


def _retention_kernel(q_ref, k_ref, v_ref, g_ref, gn_ref, o_ref, state_ref, decay_ref, xi_ref, zeta_ref):
    n = pl.program_id(1)
    C = RET_CHUNK

    @pl.when((pl.program_id(0) == 0) & (n == 0))
    def _tables():
        r = lax.broadcasted_iota(jnp.int32, (C, C), 0).astype(F32)
        c = lax.broadcasted_iota(jnp.int32, (C, C), 1).astype(F32)
        diff = r - c
        for h in range(RET_HEADS):
            lg = _RET_LOG_G[h]
            decay_ref[h] = jnp.where(diff >= 0, jnp.exp(jnp.maximum(diff, 0.0) * lg), 0.0)
            xi_ref[h] = jnp.exp((r + 1.0) * lg)
            zeta_ref[h] = jnp.exp((C - 1.0 - r) * lg)

    @pl.when(n == 0)
    def _reset():
        state_ref[...] = jnp.zeros_like(state_ref)

    for h in range(RET_HEADS):
        qs = slice(h * RET_DK, (h + 1) * RET_DK)
        vs = slice(h * RET_DV, (h + 1) * RET_DV)
        qh = q_ref[0, :, qs]
        kh = k_ref[0, :, qs]
        vh = v_ref[0, :, vs]
        inner = (_dot_tb(qh, kh) * decay_ref[h]).astype(BF16)
        state = state_ref[h]
        xi = xi_ref[h]
        cross = _dot(qh, state.astype(BF16)) * jnp.concatenate([xi, xi], axis=1)
        o = _dot(inner, vh) + cross
        kz = (kh.astype(F32) * zeta_ref[h]).astype(BF16)
        state_ref[h] = math.exp(C * _RET_LOG_G[h]) * state + _dot_ta(kz, vh)
        mu = jnp.mean(o, axis=-1, keepdims=True)
        d = o - mu
        var = jnp.mean(d * d, axis=-1, keepdims=True)
        y = d * lax.rsqrt(var + RMS_EPS) * gn_ref[:, vs]
        g = g_ref[0, :, vs].astype(F32)
        o_ref[0, :, vs] = (y * (g * jax.nn.sigmoid(g))).astype(BF16)


def _retention(rq, rk, rv, rg, gn_g):
    B, S, _ = rq.shape
    C = RET_CHUNK
    blk = lambda w: pl.BlockSpec((1, C, w), lambda b, n: (b, n, 0))
    return pl.pallas_call(
        _retention_kernel,
        grid=(B, S // C),
        in_specs=[blk(RET_QK_W), blk(RET_QK_W), blk(RET_V_W), blk(RET_V_W), _resident((1, RET_V_W))],
        out_specs=blk(RET_V_W),
        out_shape=jax.ShapeDtypeStruct((B, S, RET_V_W), BF16),
        scratch_shapes=[pltpu.VMEM((RET_HEADS, RET_DK, RET_DV), F32),
                        pltpu.VMEM((RET_HEADS, C, C), F32),
                        pltpu.VMEM((RET_HEADS, C, C), F32),
                        pltpu.VMEM((RET_HEADS, C, C), F32)],
        compiler_params=pltpu.CompilerParams(dimension_semantics=("arbitrary", "arbitrary"),
                                             vmem_limit_bytes=VMEM_LIMIT),
        name="retention",
    )(rq, rk, rv, rg, gn_g)


NOTSEL_LANE0 = NSA_DH


def _nsa_kernel(q_ref, gate_ref, ksel_ref, vsel_ref, kwin_ref, vwin_ref, kc_ref, vc_ref, o_ref,
                acc_e, acc_o, lp_e, lp_o, m_e, m_o, out_acc):
    TQ, KT, WK = NSA_TQ, NSA_KT, NSA_WK
    HP = NSA_HPG // 2
    n_blk = ksel_ref.shape[2] // SEL_BLOCK
    q0 = pl.program_id(2) * TQ
    q = q_ref[0].astype(F32)
    q2 = jnp.concatenate([q[:, j * LANES:(j + 1) * LANES] for j in range(HP)], axis=0)
    q2_odd = pltpu.roll(q2, NSA_DH, 1)
    head_lanes = lax.broadcasted_iota(jnp.int32, q2.shape, 1) < NSA_DH
    t_col = q0 + lax.broadcasted_iota(jnp.int32, (TQ, 1), 0)
    lane = lax.broadcasted_iota(jnp.int32, (TQ, LANES), 1)
    even_half = lane < NSA_DH
    rows = [slice(hp * TQ, (hp + 1) * TQ) for hp in range(HP)]
    gates = gate_ref[0]

    def emit(branch, first, acc_pair, inv_pair):
        for hp in range(HP):
            c = branch * NSA_HPG + hp * 2
            we = gates[:, c:c + 1] * inv_pair[0][hp]
            wo = gates[:, c + 1:c + 2] * inv_pair[1][hp]
            contrib = jnp.where(even_half, acc_pair[0][rows[hp]] * we, acc_pair[1][rows[hp]] * wo)
            if first:
                out_acc[rows[hp]] = contrib
            else:
                out_acc[rows[hp]] += contrib

    def safe_inv(l):
        return jnp.where(l > 0, 1.0 / l, 0.0)

    def single_tile(q_pair, kk, bias):
        exps, invs = [], []
        for qx in q_pair:
            s = _dot_tb(qx, kk)
            es, inv_x = [], []
            for hp in range(HP):
                sh = s[rows[hp]] + bias
                m = jnp.maximum(jnp.max(sh, axis=-1, keepdims=True), MAX_FLOOR)
                e = jnp.exp2(sh - m)
                es.append(e)
                inv_x.append(safe_inv(jnp.sum(e, axis=-1, keepdims=True)))
            exps.append(es)
            invs.append(inv_x)
        return exps, invs

    zero_q = jnp.zeros_like(q2)
    q_plain = (jnp.where(head_lanes, q2, zero_q).astype(BF16), jnp.where(head_lanes, q2_odd, zero_q).astype(BF16))
    c_bias = jnp.where(lane * CMP_STRIDE + (CMP_LEN - 1) <= t_col, 0.0, MASK_BIAS)
    es, invs = single_tile(q_plain, kc_ref[0, 0], c_bias)
    vc = vc_ref[0, 0]
    p_sum = jnp.zeros((TQ, LANES), F32)
    cmp_acc = []
    for x in range(2):
        ps = []
        for hp in range(HP):
            p = es[x][hp] * invs[x][hp]
            p_sum = p_sum + p
            ps.append(p.astype(BF16))
        cmp_acc.append(_dot(jnp.concatenate(ps, axis=0), vc))
    ones = [[1.0] * HP] * 2
    emit(0, True, cmp_acc, ones)

    ni = lax.broadcasted_iota(jnp.int32, (LANES, LANES), 0)
    ci = lax.broadcasted_iota(jnp.int32, (LANES, LANES), 1)
    overlap_t = ((ci * CMP_STRIDE < ni * SEL_BLOCK + SEL_BLOCK)
                 & (ci * CMP_STRIDE + CMP_LEN - 1 >= ni * SEL_BLOCK) & (ni < n_blk))
    overlap_t = jnp.where(overlap_t, 1.0, 0.0).astype(BF16)
    p_hi = p_sum.astype(BF16)
    p_lo = (p_sum - p_hi.astype(F32)).astype(BF16)
    imp_t = (_dot_tb(overlap_t, p_hi) + _dot_tb(overlap_t, p_lo))[0:n_blk]
    blk_id = lax.broadcasted_iota(jnp.int32, (n_blk, TQ), 0)
    cur = (q0 + lax.broadcasted_iota(jnp.int32, (n_blk, TQ), 1)) >> 6
    forced = (blk_id == 0) | (blk_id == cur) | (blk_id == cur - 1)
    score = jnp.where(forced, FORCE_SCORE, jnp.where(blk_id <= cur, imp_t, -1.0))
    rank = jnp.zeros((n_blk, TQ), jnp.int32)
    for mblk in range(n_blk):
        other = score[mblk:mblk + 1, :]
        ahead = (other > score) | ((other == score) & (blk_id > mblk))
        rank = rank + jnp.where(ahead, 1, 0)
    not_sel_t = jnp.where(rank < SEL_TOPN, 0.0, 1.0)
    padded = jnp.concatenate([jnp.zeros((NOTSEL_LANE0, TQ), F32), not_sel_t,
                              jnp.zeros((LANES - NOTSEL_LANE0 - n_blk, TQ), F32)], axis=0)
    not_sel = jnp.concatenate([padded.T] * HP, axis=0)
    q_aug = (jnp.where(head_lanes, q2, not_sel).astype(BF16), jnp.where(head_lanes, q2_odd, not_sel).astype(BF16))

    for ref in (acc_e, acc_o, lp_e, lp_o):
        ref[...] = jnp.zeros_like(ref)
    for ref in (m_e, m_o):
        ref[...] = jnp.full_like(ref, MASK_BIAS)

    def sel_tile(k0, bias):
        kk = ksel_ref[0, 0, pl.ds(k0, KT), :]
        vv = vsel_ref[0, 0, pl.ds(k0, KT), :]
        for qx, acc, lp, m_ref in ((q_aug[0], acc_e, lp_e, m_e), (q_aug[1], acc_o, lp_o, m_o)):
            s = _dot_tb(qx, kk)
            es, alphas = [], []
            for hp in range(HP):
                sh = s[rows[hp]]
                if bias is not None:
                    sh = sh + bias
                tiles = [sh[:, j * LANES:(j + 1) * LANES] for j in range(KT // LANES)]
                tile_max = functools.reduce(jnp.maximum, tiles)
                m_old = m_ref[rows[hp]]
                m_new = jnp.maximum(m_old, jnp.max(tile_max, axis=-1, keepdims=True))
                alpha = jnp.exp2(m_old - m_new)
                exps = [jnp.exp2(t - m_new) for t in tiles]
                lp[rows[hp]] = alpha * lp[rows[hp]] + functools.reduce(jnp.add, exps)
                m_ref[rows[hp]] = m_new
                es.append(jnp.concatenate([e.astype(BF16) for e in exps], axis=1))
                alphas.append(alpha)
            pv = _dot(jnp.concatenate(es, axis=0), vv)
            for hp in range(HP):
                acc[rows[hp]] = alphas[hp] * acc[rows[hp]] + pv[rows[hp]]

    n_full = q0 // KT

    def full_tile(kt, carry):
        sel_tile(pl.multiple_of(kt * KT, KT), None)
        return carry

    lax.fori_loop(0, n_full, full_tile, 0)
    k_diag = pl.multiple_of(n_full * KT, KT)
    kidx = k_diag + lax.broadcasted_iota(jnp.int32, (TQ, KT), 1)
    sel_tile(k_diag, jnp.where(kidx <= t_col, 0.0, MASK_BIAS))
    sel_inv = [[safe_inv(jnp.sum(lp[rows[hp]], axis=-1, keepdims=True)) for hp in range(HP)]
               for lp in (lp_e, lp_o)]
    emit(1, False, (acc_e, acc_o), sel_inv)

    w0 = pl.multiple_of(jnp.maximum(q0 - WINDOW, 0), TQ)
    kidx = w0 + lax.broadcasted_iota(jnp.int32, (TQ, WK), 1)
    w_bias = jnp.where((kidx <= t_col) & (kidx > t_col - WINDOW), 0.0, MASK_BIAS)
    es, invs = single_tile(q_aug, kwin_ref[0, 0, pl.ds(w0, WK), :], w_bias)
    vv = vwin_ref[0, 0, pl.ds(w0, WK), :]
    win_acc = [_dot(jnp.concatenate([e.astype(BF16) for e in es[x]], axis=0), vv) for x in range(2)]
    emit(2, False, win_acc, invs)

    o_ref[0] = jnp.concatenate([out_acc[rows[hp]] for hp in range(HP)], axis=1).astype(BF16)


def _nsa(nq, gates, ksel, vsel, kwin, vwin, kc, vc):
    B, S, _ = nq.shape
    G = NSA_GROUPS
    TQ = NSA_TQ
    assert S // SEL_BLOCK <= LANES - NOTSEL_LANE0 and S % NSA_KT == 0 and S >= NSA_WK
    rows = (NSA_HPG // 2) * TQ
    qblk = pl.BlockSpec((1, TQ, NSA_GQ_W), lambda b, g, t: (b, t, g))
    gblk = pl.BlockSpec((1, TQ, LANES), lambda b, g, t: (b, t, g))
    kvblk = pl.BlockSpec((1, 1, S, LANES), lambda b, g, t: (b, g, 0, 0))
    cblk = pl.BlockSpec((1, 1, LANES, LANES), lambda b, g, t: (b, g, 0, 0))
    wide = pltpu.VMEM((rows, LANES), F32)
    return pl.pallas_call(
        _nsa_kernel,
        grid=(B, G, S // TQ),
        in_specs=[qblk, gblk, kvblk, kvblk, kvblk, kvblk, cblk, cblk],
        out_specs=qblk,
        out_shape=jax.ShapeDtypeStruct((B, S, NSA_Q_W), BF16),
        scratch_shapes=[wide] * 7,
        compiler_params=pltpu.CompilerParams(dimension_semantics=("arbitrary",) * 3,
                                             vmem_limit_bytes=VMEM_LIMIT),
        name="nsa",
    )(nq, gates, ksel, vsel, kwin, vwin, kc, vc)


def _tail_kernel(final, x_ref, yr_ref, yn_ref, p_ref, gmix_ref, gmlp_ref, gple_ref, gfin_ref,
                 wmg_ref, wro_ref, wno_ref, wout_ref, wup_ref, wdn_ref, wpg_ref, wpp_ref, o_ref):
    x = x_ref[...]
    h = _rms(x, gmix_ref[...]).astype(BF16)
    o_ret = _dot(yr_ref[...], wro_ref[...])
    o_nsa = _dot(yn_ref[...], wno_ref[...])
    g_ret = jax.nn.sigmoid(_dot(h, wmg_ref[:, 0:D_MODEL]))
    g_nsa = jax.nn.sigmoid(_dot(h, wmg_ref[:, D_MODEL:2 * D_MODEL]))
    mix = (g_ret * o_ret + g_nsa * o_nsa).astype(BF16)
    x = x + _dot(mix, wout_ref[...])
    h2 = _rms(x, gmlp_ref[...]).astype(BF16)
    mlp = jnp.zeros_like(x)
    step = 1024
    for c0 in range(0, MLP_HIDDEN, step):
        up = jnp.maximum(_dot(h2, wup_ref[:, c0:c0 + step]), 0.0)
        mlp = mlp + _dot((up * up).astype(BF16), wdn_ref[c0:c0 + step, :])
    x = x + mlp
    h3 = _rms(x, gple_ref[...]).astype(BF16)
    ple_gate = jax.nn.sigmoid(_dot(h3, wpg_ref[...]))
    x = x + _dot(p_ref[...].astype(BF16), wpp_ref[...]) * ple_gate
    if final:
        x = _rms(x, gfin_ref[...])
    o_ref[...] = x


def _tail(final, x2, y_ret, y_nsa, p2, g_mix, g_mlp, g_ple, g_fin, w_mg, w_ro, w_no, w_out, w_up, w_dn,
          w_pg, w_pp):
    T = x2.shape[0]
    tm = TAIL_TM
    row = lambda w: pl.BlockSpec((tm, w), lambda i: (i, 0))
    gains = [_resident((1, D_MODEL))] * 4
    weights = [_resident(w.shape) for w in (w_mg, w_ro, w_no, w_out, w_up, w_dn, w_pg, w_pp)]
    return pl.pallas_call(
        functools.partial(_tail_kernel, final),
        grid=(T // tm,),
        in_specs=[row(D_MODEL), row(RET_V_W), row(NSA_Q_W), row(PLE_DIM)] + gains + weights,
        out_specs=row(D_MODEL),
        out_shape=jax.ShapeDtypeStruct((T, D_MODEL), F32),
        compiler_params=pltpu.CompilerParams(dimension_semantics=("arbitrary",),
                                             vmem_limit_bytes=VMEM_LIMIT),
        name="tail",
    )(x2, y_ret, y_nsa, p2, g_mix, g_mlp, g_ple, g_fin, w_mg, w_ro, w_no, w_out, w_up, w_dn, w_pg, w_pp)


def _pack_w_in(w):
    gate = w[:, C_GATE:C_GATE + 3 * NSA_HEADS]
    parts = [w[:, :C_GATE]]
    for g in range(NSA_GROUPS):
        cols = [j * NSA_HEADS + g * NSA_HPG + r for j in range(3) for r in range(NSA_HPG)]
        parts.append(jnp.pad(gate[:, jnp.array(cols)], ((0, 0), (0, LANES - len(cols)))))
    return jnp.concatenate(parts, axis=1).astype(BF16)


def kernel(x, p, positions, norm_mix_g, w_in, ret_gn_g, w_ret_o, cmp_pe_k, cmp_k_w1, cmp_k_w2, cmp_pe_v, cmp_v_w1, cmp_v_w2, w_nsa_o, w_merge_gate, w_out, norm_mlp_g, w_mlp_up, w_mlp_down, norm_ple_g, w_ple_gate, w_ple_proj, norm_final_g):
    B, S, D = x.shape
    depth = p.shape[0]
    T = B * S
    G = NSA_GROUPS
    bf = lambda a: a.astype(BF16)
    row = lambda a: a.reshape(1, -1)
    posf = positions.reshape(T, 1).astype(F32)
    inv_r = ROPE_THETA ** (-jnp.arange(0, RET_DK, 2, dtype=F32) / RET_DK)
    inv_n = ROPE_THETA ** (-jnp.arange(0, NSA_DH, 2, dtype=F32) / NSA_DH)
    inv_r = jnp.tile(inv_r, LANES // inv_r.shape[0]).reshape(1, LANES)
    inv_n = jnp.tile(inv_n, LANES // inv_n.shape[0]).reshape(1, LANES)
    n_rows = S // CMP_STRIDE
    key_blk = jnp.arange(S, dtype=jnp.int32)[:, None] // SEL_BLOCK
    blk_lane = jnp.arange(LANES - NSA_DH, dtype=jnp.int32)[None, :]
    sel_tag = bf(jnp.where(key_blk == blk_lane, MASK_BIAS, 0.0))
    sel_tag = jnp.broadcast_to(sel_tag, (B, G, S, LANES - NSA_DH))
    zero_tag = jnp.zeros((B, G, S, LANES - NSA_DH), BF16)

    def per_group(t):
        return t.reshape(B, S, G, NSA_DH).transpose(0, 2, 1, 3)

    def dup(t):
        t = per_group(t)
        return jnp.concatenate([t, t], axis=-1)

    def tagged(t, tag):
        return jnp.concatenate([per_group(t), tag], axis=-1)

    def strides(t):
        return per_group(t).reshape(B, G, n_rows, CMP_STRIDE * NSA_DH)

    def pe_rows(pe):
        return jnp.broadcast_to(bf(pe).reshape(1, -1), (8, CMP_LEN * NSA_DH))

    dup_cols = lambda w: bf(jnp.concatenate([w, w], axis=1))
    pad_cols = lambda w: bf(jnp.concatenate([w, jnp.zeros_like(w)], axis=1))

    x2 = x.reshape(T, D)
    for i in range(depth):
        rq, rk, rv, rg, nq, kv, gates = _proj(x2, posf, row(norm_mix_g[i]), inv_r, inv_n, _pack_w_in(w_in[i]))
        piece = lambda j: kv[:, j * LANES:(j + 1) * LANES]
        kc, vc = _compress(strides(piece(0)), strides(piece(1)),
                           bf(cmp_k_w1[i]), pad_cols(cmp_k_w2[i]), pe_rows(cmp_pe_k[i]),
                           bf(cmp_v_w1[i]), dup_cols(cmp_v_w2[i]), pe_rows(cmp_pe_v[i]))
        sh3 = lambda a: a.reshape(B, S, a.shape[-1])
        y_ret = _retention(sh3(rq), sh3(rk), sh3(rv), sh3(rg), row(ret_gn_g[i]))
        y_nsa = _nsa(sh3(nq), sh3(gates), tagged(piece(2), sel_tag), dup(piece(3)),
                     tagged(piece(4), zero_tag), dup(piece(5)), kc, vc)
        x2 = _tail(i == depth - 1, x2, y_ret.reshape(T, RET_V_W), y_nsa.reshape(T, NSA_Q_W),
                   p[i].reshape(T, PLE_DIM), row(norm_mix_g[i]), row(norm_mlp_g[i]), row(norm_ple_g[i]),
                   row(norm_final_g), bf(w_merge_gate[i]), bf(w_ret_o[i]), bf(w_nsa_o[i]), bf(w_out[i]),
                   bf(w_mlp_up[i]), bf(w_mlp_down[i]), bf(w_ple_gate[i]), bf(w_ple_proj[i]))
    return x2.reshape(B, S, D)
```

```python
import functools
import math

import jax
import jax.numpy as jnp
from jax import lax
from jax.experimental import pallas as pl
from jax.experimental.pallas import tpu as pltpu

F32 = jnp.float32
BF16 = jnp.bfloat16

D_MODEL = 1024
PLE_DIM = 256
RMS_EPS = 1e-6
ROPE_THETA = 10000.0
RET_HEADS = 8
RET_DK = 128
RET_DV = 256
RET_CHUNK = 128
RET_QK_W = RET_HEADS * RET_DK
RET_V_W = RET_HEADS * RET_DV
NSA_HEADS = 16
NSA_GROUPS = 2
NSA_HPG = 8
NSA_DH = 64
NSA_Q_W = NSA_HEADS * NSA_DH
NSA_GQ_W = NSA_HPG * NSA_DH
CMP_LEN = 32
CMP_STRIDE = 16
CMP_HIDDEN = 256
SEL_BLOCK = 64
SEL_TOPN = 8
WINDOW = 512
FORCE_SCORE = 1e6
MLP_HIDDEN = 4 * D_MODEL

LANES = 128
MASK_BIAS = -1e30
MAX_FLOOR = -1e29
LOG2E = math.log2(math.e)
VMEM_LIMIT = 56 * 1024 * 1024

PROJ_TM = 256
TAIL_TM = 256
NSA_TQ = 256
NSA_KT = 512
NSA_WK = WINDOW + NSA_TQ

C_RQ, C_RK, C_RV, C_RG = 0, 1024, 2048, 4096
C_NQ, C_KV, C_GATE, C_END = 6144, 7168, 7936, 8192
GATE_W = NSA_GROUPS * LANES


def _resident(shape):
    nd = len(shape)
    return pl.BlockSpec(shape, lambda *_: (0,) * nd, pipeline_mode=pl.Buffered(1))


def _rms(x, g):
    return x * lax.rsqrt(jnp.mean(x * x, axis=-1, keepdims=True) + RMS_EPS) * g


def _dot(a, b):
    return jnp.dot(a, b, preferred_element_type=F32)


def _dot_tb(a, b):
    return lax.dot_general(a, b, (((1,), (1,)), ((), ())), preferred_element_type=F32)


def _dot_ta(a, b):
    return lax.dot_general(a, b, (((0,), (0,)), ((), ())), preferred_element_type=F32)


def _proj_kernel(seq_len, x_ref, pos_ref, g_ref, invr_ref, invn_ref, w_ref,
                 rq_ref, rk_ref, rv_ref, rg_ref, nq_ref, cmp_ref, kv_ref, gate_ref):
    tm = x_ref.shape[0]
    h = _rms(x_ref[...], g_ref[...]).astype(BF16)
    pos = pos_ref[...]
    lane = lax.broadcasted_iota(jnp.int32, (tm, LANES), 1)
    ang_r = pos * invr_ref[...]
    cos_r = jnp.cos(ang_r)
    sin_r = jnp.sin(ang_r)
    sin_r = jnp.where(lane < 64, -sin_r, sin_r)
    ang_n = pos * invn_ref[...]
    cos_n = jnp.cos(ang_n)
    sin_n = jnp.sin(ang_n)
    low = (lane & 32) == 0
    sin_n = jnp.where(low, -sin_n, sin_n)

    def rope_r(y):
        return y * cos_r + pltpu.roll(y, 64, 1) * sin_r

    def rope_n(y):
        partner = jnp.where(low, pltpu.roll(y, 96, 1), pltpu.roll(y, 32, 1))
        return y * cos_n + partner * sin_n

    seq_pos = (pl.program_id(0) * tm) % seq_len + lax.broadcasted_iota(jnp.int32, (tm, LANES), 0)
    sel_tag = jnp.where(lane - NSA_DH == seq_pos // SEL_BLOCK, MASK_BIAS, 0.0)

    k_scale = RET_DK ** -0.5
    q_scale = NSA_DH ** -0.5 * LOG2E
    chunk = 512
    for c0 in range(0, C_END, chunk):
        y = _dot(h, w_ref[:, c0:c0 + chunk])
        for j in range(chunk // LANES):
            col = c0 + j * LANES
            piece = y[:, j * LANES:(j + 1) * LANES]
            if col < C_RK:
                rq_ref[:, col - C_RQ:col - C_RQ + LANES] = rope_r(piece).astype(BF16)
            elif col < C_RV:
                rk_ref[:, col - C_RK:col - C_RK + LANES] = (rope_r(piece) * k_scale).astype(BF16)
            elif col < C_RG:
                rv_ref[:, col - C_RV:col - C_RV + LANES] = piece.astype(BF16)
            elif col < C_NQ:
                rg_ref[:, col - C_RG:col - C_RG + LANES] = piece.astype(BF16)
            elif col < C_KV:
                nq_ref[:, col - C_NQ:col - C_NQ + LANES] = (rope_n(piece) * q_scale).astype(BF16)
            elif col < C_GATE:
                j_kv = (col - C_KV) // LANES
                is_key = j_kv % 2 == 0
                val = rope_n(piece) if is_key else piece
                if j_kv < 2:
                    cmp_ref[:, j_kv * LANES:(j_kv + 1) * LANES] = val.astype(BF16)
                    continue
                swapped = pltpu.roll(val, NSA_DH, 1)
                if is_key:
                    fill = sel_tag if j_kv == 2 else jnp.zeros_like(val)
                    per_group = (jnp.where(lane < NSA_DH, val, fill), jnp.where(lane < NSA_DH, swapped, fill))
                else:
                    per_group = (jnp.where(lane < NSA_DH, val, swapped), jnp.where(lane < NSA_DH, swapped, val))
                for g in range(NSA_GROUPS):
                    kv_ref[g, :, (j_kv - 2) * LANES:(j_kv - 1) * LANES] = per_group[g].astype(BF16)
            else:
                gate_ref[:, col - C_GATE:col - C_GATE + LANES] = jax.nn.sigmoid(piece)


def _proj(seq_len, x2, posf, g_mix, inv_r, inv_n, w_all):
    T = x2.shape[0]
    tm = PROJ_TM
    assert seq_len % tm == 0
    row = lambda w: pl.BlockSpec((tm, w), lambda i: (i, 0))
    kv_w = 4 * LANES
    out_shapes = [
        jax.ShapeDtypeStruct((T, RET_QK_W), BF16),
        jax.ShapeDtypeStruct((T, RET_QK_W), BF16),
        jax.ShapeDtypeStruct((T, RET_V_W), BF16),
        jax.ShapeDtypeStruct((T, RET_V_W), BF16),
        jax.ShapeDtypeStruct((T, NSA_Q_W), BF16),
        jax.ShapeDtypeStruct((T, 2 * LANES), BF16),
        jax.ShapeDtypeStruct((NSA_GROUPS, T, kv_w), BF16),
        jax.ShapeDtypeStruct((T, GATE_W), F32),
    ]
    return pl.pallas_call(
        functools.partial(_proj_kernel, seq_len),
        grid=(T // tm,),
        in_specs=[row(D_MODEL), row(1), _resident((1, D_MODEL)), _resident((1, LANES)),
                  _resident((1, LANES)), _resident((D_MODEL, C_END))],
        out_specs=[row(RET_QK_W), row(RET_QK_W), row(RET_V_W), row(RET_V_W), row(NSA_Q_W),
                   row(2 * LANES), pl.BlockSpec((NSA_GROUPS, tm, kv_w), lambda i: (0, i, 0)), row(GATE_W)],
        out_shape=out_shapes,
        compiler_params=pltpu.CompilerParams(dimension_semantics=("arbitrary",),
                                             vmem_limit_bytes=VMEM_LIMIT),
        name="proj",
    )(x2, posf, g_mix, inv_r, inv_n, w_all)


def _compress_kernel(k16_ref, v16_ref, w1k_ref, w2k_ref, pek_ref, w1v_ref, w2v_ref, pev_ref,
                     kc_ref, vc_ref):
    half = CMP_STRIDE * NSA_DH
    for x_ref, w1_ref, w2_ref, pe_ref, o_ref in ((k16_ref, w1k_ref, w2k_ref, pek_ref, kc_ref),
                                                 (v16_ref, w1v_ref, w2v_ref, pev_ref, vc_ref)):
        x = x_ref[0, 0]
        first = _dot(x, w1_ref[0:half, :])
        second = _dot(x, w1_ref[half:2 * half, :])
        pe_term = _dot(pe_ref[...], w1_ref[...])[0:1, :]
        hidden = first + pltpu.roll(second, second.shape[0] - 1, 0) + pe_term
        act = jax.nn.gelu(hidden).astype(BF16)
        o_ref[0, 0] = _dot(act, w2_ref[...]).astype(BF16)


def _compress(k16, v16, w1k, w2k, pek, w1v, w2v, pev):
    B, G, R, W = k16.shape
    blk = pl.BlockSpec((1, 1, R, W), lambda b, g: (b, g, 0, 0))
    oblk = pl.BlockSpec((1, 1, R, LANES), lambda b, g: (b, g, 0, 0))
    out = jax.ShapeDtypeStruct((B, G, R, LANES), BF16)
    wspecs = [_resident(w1k.shape), _resident(w2k.shape), _resident(pek.shape)]
    return pl.pallas_call(
        _compress_kernel,
        grid=(B, G),
        in_specs=[blk, blk] + wspecs + wspecs,
        out_specs=[oblk, oblk],
        out_shape=[out, out],
        compiler_params=pltpu.CompilerParams(dimension_semantics=("arbitrary", "arbitrary"),
                                             vmem_limit_bytes=VMEM_LIMIT),
        name="compress",
    )(k16, v16, w1k, w2k, pek, w1v, w2v, pev)


_RET_LOG_G = [math.log(1.0 - 2.0 ** (-5.0 - h)) for h in range(RET_HEADS)]


def _retention_kernel(q_ref, k_ref, v_ref, g_ref, gn_ref, o_ref, state_ref, decay_ref, xi_ref, zeta_ref):
    n = pl.program_id(1)
    C = RET_CHUNK

    @pl.when((pl.program_id(0) == 0) & (n == 0))
    def _tables():
        r = lax.broadcasted_iota(jnp.int32, (C, C), 0).astype(F32)
        c = lax.broadcasted_iota(jnp.int32, (C, C), 1).astype(F32)
        diff = r - c
        for h in range(RET_HEADS):
            lg = _RET_LOG_G[h]
            decay_ref[h] = jnp.where(diff >= 0, jnp.exp(jnp.maximum(diff, 0.0) * lg), 0.0)
            xi_ref[h] = jnp.exp((r + 1.0) * lg)
            zeta_ref[h] = jnp.exp((C - 1.0 - r) * lg)

    @pl.when(n == 0)
    def _reset():
        state_ref[...] = jnp.zeros_like(state_ref)

    for h in range(RET_HEADS):
        qs = slice(h * RET_DK, (h + 1) * RET_DK)
        vs = slice(h * RET_DV, (h + 1) * RET_DV)
        qh = q_ref[0, :, qs]
        kh = k_ref[0, :, qs]
        vh = v_ref[0, :, vs]
        inner = (_dot_tb(qh, kh) * decay_ref[h]).astype(BF16)
        state = state_ref[h]
        xi = xi_ref[h]
        cross = _dot(qh, state.astype(BF16)) * jnp.concatenate([xi, xi], axis=1)
        o = _dot(inner, vh) + cross
        kz = (kh.astype(F32) * zeta_ref[h]).astype(BF16)
        state_ref[h] = math.exp(C * _RET_LOG_G[h]) * state + _dot_ta(kz, vh)
        mu = jnp.mean(o, axis=-1, keepdims=True)
        d = o - mu
        var = jnp.mean(d * d, axis=-1, keepdims=True)
        y = d * lax.rsqrt(var + RMS_EPS) * gn_ref[:, vs]
        g = g_ref[0, :, vs].astype(F32)
        o_ref[0, :, vs] = (y * (g * jax.nn.sigmoid(g))).astype(BF16)


def _retention(rq, rk, rv, rg, gn_g):
    B, S, _ = rq.shape
    C = RET_CHUNK
    blk = lambda w: pl.BlockSpec((1, C, w), lambda b, n: (b, n, 0))
    return pl.pallas_call(
        _retention_kernel,
        grid=(B, S // C),
        in_specs=[blk(RET_QK_W), blk(RET_QK_W), blk(RET_V_W), blk(RET_V_W), _resident((1, RET_V_W))],
        out_specs=blk(RET_V_W),
        out_shape=jax.ShapeDtypeStruct((B, S, RET_V_W), BF16),
        scratch_shapes=[pltpu.VMEM((RET_HEADS, RET_DK, RET_DV), F32),
                        pltpu.VMEM((RET_HEADS, C, C), F32),
                        pltpu.VMEM((RET_HEADS, C, C), F32),
                        pltpu.VMEM((RET_HEADS, C, C), F32)],
        compiler_params=pltpu.CompilerParams(dimension_semantics=("arbitrary", "arbitrary"),
                                             vmem_limit_bytes=VMEM_LIMIT),
        name="retention",
    )(rq, rk, rv, rg, gn_g)


NOTSEL_LANE0 = NSA_DH


def _nsa_kernel(q_ref, gate_ref, ksel_ref, vsel_ref, kwin_ref, vwin_ref, kc_ref, vc_ref, o_ref,
                acc_e, acc_o, lp_e, lp_o, m_e, m_o, out_acc):
    TQ, KT, WK = NSA_TQ, NSA_KT, NSA_WK
    HP = NSA_HPG // 2
    n_blk = ksel_ref.shape[2] // SEL_BLOCK
    q0 = pl.program_id(2) * TQ
    q = q_ref[0].astype(F32)
    q2 = jnp.concatenate([q[:, j * LANES:(j + 1) * LANES] for j in range(HP)], axis=0)
    q2_odd = pltpu.roll(q2, NSA_DH, 1)
    head_lanes = lax.broadcasted_iota(jnp.int32, q2.shape, 1) < NSA_DH
    t_col = q0 + lax.broadcasted_iota(jnp.int32, (TQ, 1), 0)
    lane = lax.broadcasted_iota(jnp.int32, (TQ, LANES), 1)
    even_half = lane < NSA_DH
    rows = [slice(hp * TQ, (hp + 1) * TQ) for hp in range(HP)]
    gates = gate_ref[0]

    def emit(branch, first, acc_pair, inv_pair):
        for hp in range(HP):
            c = branch * NSA_HPG + hp * 2
            we = gates[:, c:c + 1] * inv_pair[0][hp]
            wo = gates[:, c + 1:c + 2] * inv_pair[1][hp]
            contrib = jnp.where(even_half, acc_pair[0][rows[hp]] * we, acc_pair[1][rows[hp]] * wo)
            if first:
                out_acc[rows[hp]] = contrib
            else:
                out_acc[rows[hp]] += contrib

    def safe_inv(l):
        return jnp.where(l > 0, 1.0 / l, 0.0)

    def single_tile(q_pair, kk, bias):
        exps, invs = [], []
        for qx in q_pair:
            s = _dot_tb(qx, kk)
            es, inv_x = [], []
            for hp in range(HP):
                sh = s[rows[hp]] + bias
                m = jnp.maximum(jnp.max(sh, axis=-1, keepdims=True), MAX_FLOOR)
                e = jnp.exp2(sh - m)
                es.append(e)
                inv_x.append(safe_inv(jnp.sum(e, axis=-1, keepdims=True)))
            exps.append(es)
            invs.append(inv_x)
        return exps, invs

    zero_q = jnp.zeros_like(q2)
    q_plain = (jnp.where(head_lanes, q2, zero_q).astype(BF16), jnp.where(head_lanes, q2_odd, zero_q).astype(BF16))
    c_bias = jnp.where(lane * CMP_STRIDE + (CMP_LEN - 1) <= t_col, 0.0, MASK_BIAS)
    es, invs = single_tile(q_plain, kc_ref[0, 0], c_bias)
    vc = vc_ref[0, 0]
    p_sum = jnp.zeros((TQ, LANES), F32)
    cmp_acc = []
    for x in range(2):
        ps = []
        for hp in range(HP):
            p = es[x][hp] * invs[x][hp]
            p_sum = p_sum + p
            ps.append(p.astype(BF16))
        cmp_acc.append(_dot(jnp.concatenate(ps, axis=0), vc))
    ones = [[1.0] * HP] * 2
    emit(0, True, cmp_acc, ones)

    w0 = pl.multiple_of(jnp.maximum(q0 - WINDOW, 0), TQ)
    kidx = w0 + lax.broadcasted_iota(jnp.int32, (TQ, WK), 1)
    w_bias = jnp.where((kidx <= t_col) & (kidx > t_col - WINDOW), 0.0, MASK_BIAS)
    es, invs = single_tile(q_plain, kwin_ref[0, 0, pl.ds(w0, WK), :], w_bias)
    vv = vwin_ref[0, 0, pl.ds(w0, WK), :]
    win_acc = [_dot(jnp.concatenate([e.astype(BF16) for e in es[x]], axis=0), vv) for x in range(2)]
    emit(2, False, win_acc, invs)

    ni = lax.broadcasted_iota(jnp.int32, (LANES, LANES), 0)
    ci = lax.broadcasted_iota(jnp.int32, (LANES, LANES), 1)
    overlap_t = ((ci * CMP_STRIDE < ni * SEL_BLOCK + SEL_BLOCK)
                 & (ci * CMP_STRIDE + CMP_LEN - 1 >= ni * SEL_BLOCK) & (ni < n_blk))
    overlap_t = jnp.where(overlap_t, 1.0, 0.0).astype(BF16)
    p_hi = p_sum.astype(BF16)
    p_lo = (p_sum - p_hi.astype(F32)).astype(BF16)
    imp_t = (_dot_tb(overlap_t, p_hi) + _dot_tb(overlap_t, p_lo))[0:n_blk]
    blk_id = lax.broadcasted_iota(jnp.int32, (n_blk, TQ), 0)
    cur = (q0 + lax.broadcasted_iota(jnp.int32, (n_blk, TQ), 1)) >> 6
    forced = (blk_id == 0) | (blk_id == cur) | (blk_id == cur - 1)
    score = jnp.where(forced, FORCE_SCORE, jnp.where(blk_id <= cur, imp_t, -1.0))
    rank = jnp.zeros((n_blk, TQ), jnp.int32)
    for mblk in range(n_blk):
        other = score[mblk:mblk + 1, :]
        ahead = (other > score) | ((other == score) & (blk_id > mblk))
        rank = rank + jnp.where(ahead, 1, 0)
    not_sel_t = jnp.where(rank < SEL_TOPN, 0.0, 1.0)
    padded = jnp.concatenate([jnp.zeros((NOTSEL_LANE0, TQ), F32), not_sel_t,
                              jnp.zeros((LANES - NOTSEL_LANE0 - n_blk, TQ), F32)], axis=0)
    not_sel = jnp.concatenate([padded.T] * HP, axis=0)
    q_aug = (jnp.where(head_lanes, q2, not_sel).astype(BF16), jnp.where(head_lanes, q2_odd, not_sel).astype(BF16))

    for ref in (acc_e, acc_o, lp_e, lp_o):
        ref[...] = jnp.zeros_like(ref)
    for ref in (m_e, m_o):
        ref[...] = jnp.full_like(ref, MASK_BIAS)

    key_lane = lax.broadcasted_iota(jnp.int32, (TQ, KT), 1)

    def sel_tile(kt, carry):
        k0 = pl.multiple_of(kt * KT, KT)
        kk = ksel_ref[0, 0, pl.ds(k0, KT), :]
        vv = vsel_ref[0, 0, pl.ds(k0, KT), :]
        bias = jnp.where(key_lane <= t_col - k0, 0.0, MASK_BIAS)
        for qx, acc, lp, m_ref in ((q_aug[0], acc_e, lp_e, m_e), (q_aug[1], acc_o, lp_o, m_o)):
            s = _dot_tb(qx, kk)
            es, alphas = [], []
            for hp in range(HP):
                sh = s[rows[hp]] + bias
                tiles = [sh[:, j * LANES:(j + 1) * LANES] for j in range(KT // LANES)]
                tile_max = functools.reduce(jnp.maximum, tiles)
                m_old = m_ref[rows[hp]]
                m_new = jnp.maximum(m_old, jnp.max(tile_max, axis=-1, keepdims=True))
                alpha = jnp.exp2(m_old - m_new)
                exps = [jnp.exp2(t - m_new) for t in tiles]
                lp[rows[hp]] = alpha * lp[rows[hp]] + functools.reduce(jnp.add, exps)
                m_ref[rows[hp]] = m_new
                es.append(jnp.concatenate([e.astype(BF16) for e in exps], axis=1))
                alphas.append(alpha)
            pv = _dot(jnp.concatenate(es, axis=0), vv)
            for hp in range(HP):
                acc[rows[hp]] = alphas[hp] * acc[rows[hp]] + pv[rows[hp]]
        return carry

    lax.fori_loop(0, (q0 + TQ + KT - 1) // KT, sel_tile, 0)
    sel_inv = [[safe_inv(jnp.sum(lp[rows[hp]], axis=-1, keepdims=True)) for hp in range(HP)]
               for lp in (lp_e, lp_o)]
    emit(1, False, (acc_e, acc_o), sel_inv)

    o_ref[0] = jnp.concatenate([out_acc[rows[hp]] for hp in range(HP)], axis=1).astype(BF16)


def _nsa(nq, gates, kv, kc, vc):
    B, S, _ = nq.shape
    G = NSA_GROUPS
    TQ = NSA_TQ
    assert S // SEL_BLOCK <= LANES - NOTSEL_LANE0 and S % NSA_KT == 0 and S >= NSA_WK
    rows = (NSA_HPG // 2) * TQ
    qblk = pl.BlockSpec((1, TQ, NSA_GQ_W), lambda b, g, t: (b, t, g))
    gblk = pl.BlockSpec((1, TQ, LANES), lambda b, g, t: (b, t, g))
    kvblk = lambda j: pl.BlockSpec((1, 1, S, LANES), lambda b, g, t: (g, b, 0, j))
    cblk = pl.BlockSpec((1, 1, LANES, LANES), lambda b, g, t: (b, g, 0, 0))
    wide = pltpu.VMEM((rows, LANES), F32)
    return pl.pallas_call(
        _nsa_kernel,
        grid=(B, G, S // TQ),
        in_specs=[qblk, gblk, kvblk(0), kvblk(1), kvblk(2), kvblk(3), cblk, cblk],
        out_specs=qblk,
        out_shape=jax.ShapeDtypeStruct((B, S, NSA_Q_W), BF16),
        scratch_shapes=[wide] * 7,
        compiler_params=pltpu.CompilerParams(dimension_semantics=("arbitrary",) * 3,
                                             vmem_limit_bytes=VMEM_LIMIT),
        name="nsa",
    )(nq, gates, kv, kv, kv, kv, kc, vc)


def _tail_kernel(final, x_ref, yr_ref, yn_ref, p_ref, gmix_ref, gmlp_ref, gple_ref, gfin_ref,
                 wmg_ref, wro_ref, wno_ref, wout_ref, wup_ref, wdn_ref, wpg_ref, wpp_ref, o_ref):
    x = x_ref[...]
    h = _rms(x, gmix_ref[...]).astype(BF16)
    o_ret = _dot(yr_ref[...], wro_ref[...])
    o_nsa = _dot(yn_ref[...], wno_ref[...])
    g_ret = jax.nn.sigmoid(_dot(h, wmg_ref[:, 0:D_MODEL]))
    g_nsa = jax.nn.sigmoid(_dot(h, wmg_ref[:, D_MODEL:2 * D_MODEL]))
    mix = (g_ret * o_ret + g_nsa * o_nsa).astype(BF16)
    x = x + _dot(mix, wout_ref[...])
    h2 = _rms(x, gmlp_ref[...]).astype(BF16)
    mlp = jnp.zeros_like(x)
    step = 1024
    for c0 in range(0, MLP_HIDDEN, step):
        up = jnp.maximum(_dot(h2, wup_ref[:, c0:c0 + step]), 0.0)
        mlp = mlp + _dot((up * up).astype(BF16), wdn_ref[c0:c0 + step, :])
    x = x + mlp
    h3 = _rms(x, gple_ref[...]).astype(BF16)
    ple_gate = jax.nn.sigmoid(_dot(h3, wpg_ref[...]))
    x = x + _dot(p_ref[...].astype(BF16), wpp_ref[...]) * ple_gate
    if final:
        x = _rms(x, gfin_ref[...])
    o_ref[...] = x


def _tail(final, x2, y_ret, y_nsa, p2, g_mix, g_mlp, g_ple, g_fin, w_mg, w_ro, w_no, w_out, w_up, w_dn,
          w_pg, w_pp):
    T = x2.shape[0]
    tm = TAIL_TM
    row = lambda w: pl.BlockSpec((tm, w), lambda i: (i, 0))
    gains = [_resident((1, D_MODEL))] * 4
    weights = [_resident(w.shape) for w in (w_mg, w_ro, w_no, w_out, w_up, w_dn, w_pg, w_pp)]
    return pl.pallas_call(
        functools.partial(_tail_kernel, final),
        grid=(T // tm,),
        in_specs=[row(D_MODEL), row(RET_V_W), row(NSA_Q_W), row(PLE_DIM)] + gains + weights,
        out_specs=row(D_MODEL),
        out_shape=jax.ShapeDtypeStruct((T, D_MODEL), F32),
        compiler_params=pltpu.CompilerParams(dimension_semantics=("arbitrary",),
                                             vmem_limit_bytes=VMEM_LIMIT),
        name="tail",
    )(x2, y_ret, y_nsa, p2, g_mix, g_mlp, g_ple, g_fin, w_mg, w_ro, w_no, w_out, w_up, w_dn, w_pg, w_pp)


def _pack_w_in(w):
    gate = w[:, C_GATE:C_GATE + 3 * NSA_HEADS]
    parts = [w[:, :C_GATE]]
    for g in range(NSA_GROUPS):
        cols = [j * NSA_HEADS + g * NSA_HPG + r for j in range(3) for r in range(NSA_HPG)]
        parts.append(jnp.pad(gate[:, jnp.array(cols)], ((0, 0), (0, LANES - len(cols)))))
    return jnp.concatenate(parts, axis=1).astype(BF16)


def kernel(x, p, positions, norm_mix_g, w_in, ret_gn_g, w_ret_o, cmp_pe_k, cmp_k_w1, cmp_k_w2, cmp_pe_v, cmp_v_w1, cmp_v_w2, w_nsa_o, w_merge_gate, w_out, norm_mlp_g, w_mlp_up, w_mlp_down, norm_ple_g, w_ple_gate, w_ple_proj, norm_final_g):
    B, S, D = x.shape
    depth = p.shape[0]
    T = B * S
    G = NSA_GROUPS
    bf = lambda a: a.astype(BF16)
    row = lambda a: a.reshape(1, -1)
    posf = positions.reshape(T, 1).astype(F32)
    inv_r = ROPE_THETA ** (-jnp.arange(0, RET_DK, 2, dtype=F32) / RET_DK)
    inv_n = ROPE_THETA ** (-jnp.arange(0, NSA_DH, 2, dtype=F32) / NSA_DH)
    inv_r = jnp.tile(inv_r, LANES // inv_r.shape[0]).reshape(1, LANES)
    inv_n = jnp.tile(inv_n, LANES // inv_n.shape[0]).reshape(1, LANES)
    n_rows = S // CMP_STRIDE

    def strides(t):
        t = t.reshape(B, S, G, NSA_DH).transpose(0, 2, 1, 3)
        return t.reshape(B, G, n_rows, CMP_STRIDE * NSA_DH)

    def pe_rows(pe):
        return jnp.broadcast_to(bf(pe).reshape(1, -1), (8, CMP_LEN * NSA_DH))

    dup_cols = lambda w: bf(jnp.concatenate([w, w], axis=1))
    pad_cols = lambda w: bf(jnp.concatenate([w, jnp.zeros_like(w)], axis=1))

    x2 = x.reshape(T, D)
    for i in range(depth):
        rq, rk, rv, rg, nq, cmp_kv, kv, gates = _proj(S, x2, posf, row(norm_mix_g[i]), inv_r, inv_n,
                                                       _pack_w_in(w_in[i]))
        kc, vc = _compress(strides(cmp_kv[:, :LANES]), strides(cmp_kv[:, LANES:]),
                           bf(cmp_k_w1[i]), pad_cols(cmp_k_w2[i]), pe_rows(cmp_pe_k[i]),
                           bf(cmp_v_w1[i]), dup_cols(cmp_v_w2[i]), pe_rows(cmp_pe_v[i]))
        sh3 = lambda a: a.reshape(B, S, a.shape[-1])
        y_ret = _retention(sh3(rq), sh3(rk), sh3(rv), sh3(rg), row(ret_gn_g[i]))
        y_nsa = _nsa(sh3(nq), sh3(gates), kv.reshape(G, B, S, kv.shape[-1]), kc, vc)
        x2 = _tail(i == depth - 1, x2, y_ret.reshape(T, RET_V_W), y_nsa.reshape(T, NSA_Q_W),
                   p[i].reshape(T, PLE_DIM), row(norm_mix_g[i]), row(norm_mlp_g[i]), row(norm_ple_g[i]),
                   row(norm_final_g), bf(w_merge_gate[i]), bf(w_ret_o[i]), bf(w_nsa_o[i]), bf(w_out[i]),
                   bf(w_mlp_up[i]), bf(w_mlp_down[i]), bf(w_ple_gate[i]), bf(w_ple_proj[i]))
    return x2.reshape(B, S, D)
```

```python
import functools
import math

import jax
import jax.numpy as jnp
from jax import lax
from jax.experimental import pallas as pl
from jax.experimental.pallas import tpu as pltpu

F32 = jnp.float32
BF16 = jnp.bfloat16

D_MODEL = 1024
PLE_DIM = 256
RMS_EPS = 1e-6
ROPE_THETA = 10000.0
RET_HEADS = 8
RET_DK = 128
RET_DV = 256
RET_CHUNK = 128
RET_STEP_CHUNKS = 4
RET_QK_W = RET_HEADS * RET_DK
RET_V_W = RET_HEADS * RET_DV
NSA_HEADS = 16
NSA_GROUPS = 2
NSA_HPG = 8
NSA_DH = 64
NSA_Q_W = NSA_HEADS * NSA_DH
NSA_GQ_W = NSA_HPG * NSA_DH
CMP_LEN = 32
CMP_STRIDE = 16
CMP_HIDDEN = 256
SEL_BLOCK = 64
SEL_TOPN = 8
WINDOW = 512
FORCE_SCORE = 1e6
MLP_HIDDEN = 4 * D_MODEL

LANES = 128
MASK_BIAS = -1e30
MAX_FLOOR = -1e29
LOG2E = math.log2(math.e)
VMEM_LIMIT = 56 * 1024 * 1024

PROJ_TM = 256
TAIL_TM = 256
NSA_TQ = 256
NSA_KT = 512
NSA_WK = WINDOW + NSA_TQ

C_RQ, C_RK, C_RV, C_RG = 0, 1024, 2048, 4096
C_NQ, C_KV, C_GATE, C_END = 6144, 7168, 7936, 8192
GATE_W = NSA_GROUPS * LANES


def _resident(shape):
    nd = len(shape)
    return pl.BlockSpec(shape, lambda *_: (0,) * nd, pipeline_mode=pl.Buffered(1))


def _rms(x, g):
    return x * lax.rsqrt(jnp.mean(x * x, axis=-1, keepdims=True) + RMS_EPS) * g


def _dot(a, b):
    return jnp.dot(a, b, preferred_element_type=F32)


def _dot_tb(a, b):
    return lax.dot_general(a, b, (((1,), (1,)), ((), ())), preferred_element_type=F32)


def _dot_ta(a, b):
    return lax.dot_general(a, b, (((0,), (0,)), ((), ())), preferred_element_type=F32)


def _proj_kernel(seq_len, x_ref, pos_ref, g_ref, invr_ref, invn_ref, w_ref,
                 rq_ref, rk_ref, rv_ref, rg_ref, nq_ref, cmp_ref, kv_ref, gate_ref):
    tm = x_ref.shape[0]
    h = _rms(x_ref[...], g_ref[...]).astype(BF16)
    pos = pos_ref[...]
    lane = lax.broadcasted_iota(jnp.int32, (tm, LANES), 1)
    ang_r = pos * invr_ref[...]
    cos_r = jnp.cos(ang_r)
    sin_r = jnp.sin(ang_r)
    sin_r = jnp.where(lane < 64, -sin_r, sin_r)
    ang_n = pos * invn_ref[...]
    cos_n = jnp.cos(ang_n)
    sin_n = jnp.sin(ang_n)
    low = (lane & 32) == 0
    sin_n = jnp.where(low, -sin_n, sin_n)

    def rope_r(y):
        return y * cos_r + pltpu.roll(y, 64, 1) * sin_r

    def rope_n(y):
        partner = jnp.where(low, pltpu.roll(y, 96, 1), pltpu.roll(y, 32, 1))
        return y * cos_n + partner * sin_n

    seq_pos = (pl.program_id(0) * tm) % seq_len + lax.broadcasted_iota(jnp.int32, (tm, LANES), 0)
    sel_tag = jnp.where(lane - NSA_DH == seq_pos // SEL_BLOCK, MASK_BIAS, 0.0)

    k_scale = RET_DK ** -0.5
    q_scale = NSA_DH ** -0.5 * LOG2E
    chunk = 512
    for c0 in range(0, C_END, chunk):
        y = _dot(h, w_ref[:, c0:c0 + chunk])
        for j in range(chunk // LANES):
            col = c0 + j * LANES
            piece = y[:, j * LANES:(j + 1) * LANES]
            if col < C_RK:
                rq_ref[:, col - C_RQ:col - C_RQ + LANES] = rope_r(piece).astype(BF16)
            elif col < C_RV:
                rk_ref[:, col - C_RK:col - C_RK + LANES] = (rope_r(piece) * k_scale).astype(BF16)
            elif col < C_RG:
                rv_ref[:, col - C_RV:col - C_RV + LANES] = piece.astype(BF16)
            elif col < C_NQ:
                rg_ref[:, col - C_RG:col - C_RG + LANES] = piece.astype(BF16)
            elif col < C_KV:
                nq_ref[:, col - C_NQ:col - C_NQ + LANES] = (rope_n(piece) * q_scale).astype(BF16)
            elif col < C_GATE:
                j_kv = (col - C_KV) // LANES
                is_key = j_kv % 2 == 0
                val = rope_n(piece) if is_key else piece
                if j_kv < 2:
                    cmp_ref[:, j_kv * LANES:(j_kv + 1) * LANES] = val.astype(BF16)
                    continue
                swapped = pltpu.roll(val, NSA_DH, 1)
                if is_key:
                    fill = sel_tag if j_kv == 2 else jnp.zeros_like(val)
                    per_group = (jnp.where(lane < NSA_DH, val, fill), jnp.where(lane < NSA_DH, swapped, fill))
                else:
                    per_group = (jnp.where(lane < NSA_DH, val, swapped), jnp.where(lane < NSA_DH, swapped, val))
                for g in range(NSA_GROUPS):
                    kv_ref[g, :, (j_kv - 2) * LANES:(j_kv - 1) * LANES] = per_group[g].astype(BF16)
            else:
                gate_ref[:, col - C_GATE:col - C_GATE + LANES] = jax.nn.sigmoid(piece)


def _proj(seq_len, x2, posf, g_mix, inv_r, inv_n, w_all):
    T = x2.shape[0]
    tm = PROJ_TM
    assert seq_len % tm == 0
    row = lambda w: pl.BlockSpec((tm, w), lambda i: (i, 0))
    kv_w = 4 * LANES
    out_shapes = [
        jax.ShapeDtypeStruct((T, RET_QK_W), BF16),
        jax.ShapeDtypeStruct((T, RET_QK_W), BF16),
        jax.ShapeDtypeStruct((T, RET_V_W), BF16),
        jax.ShapeDtypeStruct((T, RET_V_W), BF16),
        jax.ShapeDtypeStruct((T, NSA_Q_W), BF16),
        jax.ShapeDtypeStruct((T, 2 * LANES), BF16),
        jax.ShapeDtypeStruct((NSA_GROUPS, T, kv_w), BF16),
        jax.ShapeDtypeStruct((T, GATE_W), F32),
    ]
    return pl.pallas_call(
        functools.partial(_proj_kernel, seq_len),
        grid=(T // tm,),
        in_specs=[row(D_MODEL), row(1), _resident((1, D_MODEL)), _resident((1, LANES)),
                  _resident((1, LANES)), _resident((D_MODEL, C_END))],
        out_specs=[row(RET_QK_W), row(RET_QK_W), row(RET_V_W), row(RET_V_W), row(NSA_Q_W),
                   row(2 * LANES), pl.BlockSpec((NSA_GROUPS, tm, kv_w), lambda i: (0, i, 0)), row(GATE_W)],
        out_shape=out_shapes,
        compiler_params=pltpu.CompilerParams(dimension_semantics=("arbitrary",),
                                             vmem_limit_bytes=VMEM_LIMIT),
        name="proj",
    )(x2, posf, g_mix, inv_r, inv_n, w_all)


def _compress_kernel(k16_ref, v16_ref, w1k_ref, w2k_ref, pek_ref, w1v_ref, w2v_ref, pev_ref,
                     kc_ref, vc_ref):
    half = CMP_STRIDE * NSA_DH
    for x_ref, w1_ref, w2_ref, pe_ref, o_ref in ((k16_ref, w1k_ref, w2k_ref, pek_ref, kc_ref),
                                                 (v16_ref, w1v_ref, w2v_ref, pev_ref, vc_ref)):
        x = x_ref[0, 0]
        first = _dot(x, w1_ref[0:half, :])
        second = _dot(x, w1_ref[half:2 * half, :])
        pe_term = _dot(pe_ref[...], w1_ref[...])[0:1, :]
        hidden = first + pltpu.roll(second, second.shape[0] - 1, 0) + pe_term
        act = jax.nn.gelu(hidden).astype(BF16)
        o_ref[0, 0] = _dot(act, w2_ref[...]).astype(BF16)


def _compress(k16, v16, w1k, w2k, pek, w1v, w2v, pev):
    B, G, R, W = k16.shape
    blk = pl.BlockSpec((1, 1, R, W), lambda b, g: (b, g, 0, 0))
    oblk = pl.BlockSpec((1, 1, R, LANES), lambda b, g: (b, g, 0, 0))
    out = jax.ShapeDtypeStruct((B, G, R, LANES), BF16)
    wspecs = [_resident(w1k.shape), _resident(w2k.shape), _resident(pek.shape)]
    return pl.pallas_call(
        _compress_kernel,
        grid=(B, G),
        in_specs=[blk, blk] + wspecs + wspecs,
        out_specs=[oblk, oblk],
        out_shape=[out, out],
        compiler_params=pltpu.CompilerParams(dimension_semantics=("arbitrary", "arbitrary"),
                                             vmem_limit_bytes=VMEM_LIMIT),
        name="compress",
    )(k16, v16, w1k, w2k, pek, w1v, w2v, pev)


_RET_LOG_G = [math.log(1.0 - 2.0 ** (-5.0 - h)) for h in range(RET_HEADS)]


def _retention_kernel(q_ref, k_ref, v_ref, g_ref, gn_ref, o_ref, state_ref, decay_ref, xi_ref, zeta_ref):
    n = pl.program_id(1)
    C = RET_CHUNK

    @pl.when((pl.program_id(0) == 0) & (n == 0))
    def _tables():
        r = lax.broadcasted_iota(jnp.int32, (C, C), 0).astype(F32)
        c = lax.broadcasted_iota(jnp.int32, (C, C), 1).astype(F32)
        diff = r - c
        for h in range(RET_HEADS):
            lg = _RET_LOG_G[h]
            decay_ref[h] = jnp.where(diff >= 0, jnp.exp(jnp.maximum(diff, 0.0) * lg), 0.0)
            xi_ref[h] = jnp.exp((r + 1.0) * lg)
            zeta_ref[h] = jnp.exp((C - 1.0 - r) * lg)

    @pl.when(n == 0)
    def _reset():
        state_ref[...] = jnp.zeros_like(state_ref)

    for h in range(RET_HEADS):
        qs = slice(h * RET_DK, (h + 1) * RET_DK)
        vs = slice(h * RET_DV, (h + 1) * RET_DV)
        xi = xi_ref[h]
        xi = jnp.concatenate([xi, xi], axis=1)
        state = state_ref[h]
        for c in range(RET_STEP_CHUNKS):
            tok = slice(c * C, (c + 1) * C)
            qh = q_ref[0, tok, qs]
            kh = k_ref[0, tok, qs]
            vh = v_ref[0, tok, vs]
            inner = (_dot_tb(qh, kh) * decay_ref[h]).astype(BF16)
            o = _dot(inner, vh) + _dot(qh, state.astype(BF16)) * xi
            kz = (kh.astype(F32) * zeta_ref[h]).astype(BF16)
            state = math.exp(C * _RET_LOG_G[h]) * state + _dot_ta(kz, vh)
            mu = jnp.mean(o, axis=-1, keepdims=True)
            d = o - mu
            var = jnp.mean(d * d, axis=-1, keepdims=True)
            y = d * lax.rsqrt(var + RMS_EPS) * gn_ref[:, vs]
            g = g_ref[0, tok, vs].astype(F32)
            o_ref[0, tok, vs] = (y * (g * jax.nn.sigmoid(g))).astype(BF16)
        state_ref[h] = state


def _retention(rq, rk, rv, rg, gn_g):
    B, S, _ = rq.shape
    C = RET_CHUNK
    rows = RET_STEP_CHUNKS * C
    assert S % rows == 0
    blk = lambda w: pl.BlockSpec((1, rows, w), lambda b, n: (b, n, 0))
    return pl.pallas_call(
        _retention_kernel,
        grid=(B, S // rows),
        in_specs=[blk(RET_QK_W), blk(RET_QK_W), blk(RET_V_W), blk(RET_V_W), _resident((1, RET_V_W))],
        out_specs=blk(RET_V_W),
        out_shape=jax.ShapeDtypeStruct((B, S, RET_V_W), BF16),
        scratch_shapes=[pltpu.VMEM((RET_HEADS, RET_DK, RET_DV), F32),
                        pltpu.VMEM((RET_HEADS, C, C), F32),
                        pltpu.VMEM((RET_HEADS, C, C), F32),
                        pltpu.VMEM((RET_HEADS, C, C), F32)],
        compiler_params=pltpu.CompilerParams(dimension_semantics=("arbitrary", "arbitrary"),
                                             vmem_limit_bytes=VMEM_LIMIT),
        name="retention",
    )(rq, rk, rv, rg, gn_g)


NOTSEL_LANE0 = NSA_DH


def _nsa_kernel(q_ref, gate_ref, ksel_ref, vsel_ref, kwin_ref, vwin_ref, kc_ref, vc_ref, o_ref,
                acc_e, acc_o, lp_e, lp_o, m_e, m_o, out_acc):
    TQ, KT, WK = NSA_TQ, NSA_KT, NSA_WK
    HP = NSA_HPG // 2
    n_blk = ksel_ref.shape[2] // SEL_BLOCK
    q0 = pl.program_id(2) * TQ
    q = q_ref[0].astype(F32)
    q2 = jnp.concatenate([q[:, j * LANES:(j + 1) * LANES] for j in range(HP)], axis=0)
    q2_odd = pltpu.roll(q2, NSA_DH, 1)
    head_lanes = lax.broadcasted_iota(jnp.int32, q2.shape, 1) < NSA_DH
    t_col = q0 + lax.broadcasted_iota(jnp.int32, (TQ, 1), 0)
    lane = lax.broadcasted_iota(jnp.int32, (TQ, LANES), 1)
    even_half = lane < NSA_DH
    rows = [slice(hp * TQ, (hp + 1) * TQ) for hp in range(HP)]
    gates = gate_ref[0]

    def emit(branch, first, acc_pair, inv_pair):
        for hp in range(HP):
            c = branch * NSA_HPG + hp * 2
            we = gates[:, c:c + 1] * inv_pair[0][hp]
            wo = gates[:, c + 1:c + 2] * inv_pair[1][hp]
            contrib = jnp.where(even_half, acc_pair[0][rows[hp]] * we, acc_pair[1][rows[hp]] * wo)
            if first:
                out_acc[rows[hp]] = contrib
            else:
                out_acc[rows[hp]] += contrib

    def safe_inv(l):
        return jnp.where(l > 0, 1.0 / l, 0.0)

    def single_tile(q_pair, kk, bias):
        exps, invs = [], []
        for qx in q_pair:
            s = _dot_tb(qx, kk)
            es, inv_x = [], []
            for hp in range(HP):
                sh = s[rows[hp]] + bias
                m = jnp.maximum(jnp.max(sh, axis=-1, keepdims=True), MAX_FLOOR)
                e = jnp.exp2(sh - m)
                es.append(e)
                inv_x.append(safe_inv(jnp.sum(e, axis=-1, keepdims=True)))
            exps.append(es)
            invs.append(inv_x)
        return exps, invs

    zero_q = jnp.zeros_like(q2)
    q_plain = (jnp.where(head_lanes, q2, zero_q).astype(BF16), jnp.where(head_lanes, q2_odd, zero_q).astype(BF16))
    c_bias = jnp.where(lane * CMP_STRIDE + (CMP_LEN - 1) <= t_col, 0.0, MASK_BIAS)
    es, invs = single_tile(q_plain, kc_ref[0, 0], c_bias)
    vc = vc_ref[0, 0]
    p_sum = jnp.zeros((TQ, LANES), F32)
    cmp_acc = []
    for x in range(2):
        ps = []
        for hp in range(HP):
            p = es[x][hp] * invs[x][hp]
            p_sum = p_sum + p
            ps.append(p.astype(BF16))
        cmp_acc.append(_dot(jnp.concatenate(ps, axis=0), vc))
    ones = [[1.0] * HP] * 2
    emit(0, True, cmp_acc, ones)

    w0 = pl.multiple_of(jnp.maximum(q0 - WINDOW, 0), TQ)
    kidx = w0 + lax.broadcasted_iota(jnp.int32, (TQ, WK), 1)
    w_bias = jnp.where((kidx <= t_col) & (kidx > t_col - WINDOW), 0.0, MASK_BIAS)
    es, invs = single_tile(q_plain, kwin_ref[0, 0, pl.ds(w0, WK), :], w_bias)
    vv = vwin_ref[0, 0, pl.ds(w0, WK), :]
    win_acc = [_dot(jnp.concatenate([e.astype(BF16) for e in es[x]], axis=0), vv) for x in range(2)]
    emit(2, False, win_acc, invs)

    ni = lax.broadcasted_iota(jnp.int32, (LANES, LANES), 0)
    ci = lax.broadcasted_iota(jnp.int32, (LANES, LANES), 1)
    overlap_t = ((ci * CMP_STRIDE < ni * SEL_BLOCK + SEL_BLOCK)
                 & (ci * CMP_STRIDE + CMP_LEN - 1 >= ni * SEL_BLOCK) & (ni < n_blk))
    overlap_t = jnp.where(overlap_t, 1.0, 0.0).astype(BF16)
    p_hi = p_sum.astype(BF16)
    p_lo = (p_sum - p_hi.astype(F32)).astype(BF16)
    imp_t = (_dot_tb(overlap_t, p_hi) + _dot_tb(overlap_t, p_lo))[0:n_blk]
    blk_id = lax.broadcasted_iota(jnp.int32, (n_blk, TQ), 0)
    cur = (q0 + lax.broadcasted_iota(jnp.int32, (n_blk, TQ), 1)) >> 6
    forced = (blk_id == 0) | (blk_id == cur) | (blk_id == cur - 1)
    score = jnp.where(forced, FORCE_SCORE, jnp.where(blk_id <= cur, imp_t, -1.0))
    rank = jnp.zeros((n_blk, TQ), jnp.int32)
    for mblk in range(n_blk):
        other = score[mblk:mblk + 1, :]
        ahead = (other > score) | ((other == score) & (blk_id > mblk))
        rank = rank + jnp.where(ahead, 1, 0)
    not_sel_t = jnp.where(rank < SEL_TOPN, 0.0, 1.0)
    padded = jnp.concatenate([jnp.zeros((NOTSEL_LANE0, TQ), F32), not_sel_t,
                              jnp.zeros((LANES - NOTSEL_LANE0 - n_blk, TQ), F32)], axis=0)
    not_sel = jnp.concatenate([padded.T] * HP, axis=0)
    q_aug = (jnp.where(head_lanes, q2, not_sel).astype(BF16), jnp.where(head_lanes, q2_odd, not_sel).astype(BF16))

    for ref in (acc_e, acc_o, lp_e, lp_o):
        ref[...] = jnp.zeros_like(ref)
    for ref in (m_e, m_o):
        ref[...] = jnp.full_like(ref, MASK_BIAS)

    key_lane = lax.broadcasted_iota(jnp.int32, (TQ, KT), 1)

    def sel_tile(kt, carry):
        k0 = pl.multiple_of(kt * KT, KT)
        kk = ksel_ref[0, 0, pl.ds(k0, KT), :]
        vv = vsel_ref[0, 0, pl.ds(k0, KT), :]
        bias = jnp.where(key_lane <= t_col - k0, 0.0, MASK_BIAS)
        for qx, acc, lp, m_ref in ((q_aug[0], acc_e, lp_e, m_e), (q_aug[1], acc_o, lp_o, m_o)):
            s = _dot_tb(qx, kk)
            es, alphas = [], []
            for hp in range(HP):
                sh = s[rows[hp]] + bias
                tiles = [sh[:, j * LANES:(j + 1) * LANES] for j in range(KT // LANES)]
                tile_max = functools.reduce(jnp.maximum, tiles)
                m_old = m_ref[rows[hp]]
                m_new = jnp.maximum(m_old, jnp.max(tile_max, axis=-1, keepdims=True))
                alpha = jnp.exp2(m_old - m_new)
                exps = [jnp.exp2(t - m_new) for t in tiles]
                lp[rows[hp]] = alpha * lp[rows[hp]] + functools.reduce(jnp.add, exps)
                m_ref[rows[hp]] = m_new
                es.append(jnp.concatenate([e.astype(BF16) for e in exps], axis=1))
                alphas.append(alpha)
            pv = _dot(jnp.concatenate(es, axis=0), vv)
            for hp in range(HP):
                acc[rows[hp]] = alphas[hp] * acc[rows[hp]] + pv[rows[hp]]
        return carry

    lax.fori_loop(0, (q0 + TQ + KT - 1) // KT, sel_tile, 0)
    sel_inv = [[safe_inv(jnp.sum(lp[rows[hp]], axis=-1, keepdims=True)) for hp in range(HP)]
               for lp in (lp_e, lp_o)]
    emit(1, False, (acc_e, acc_o), sel_inv)

    o_ref[0] = jnp.concatenate([out_acc[rows[hp]] for hp in range(HP)], axis=1).astype(BF16)


def _nsa(nq, gates, kv, kc, vc):
    B, S, _ = nq.shape
    G = NSA_GROUPS
    TQ = NSA_TQ
    assert S // SEL_BLOCK <= LANES - NOTSEL_LANE0 and S % NSA_KT == 0 and S >= NSA_WK
    rows = (NSA_HPG // 2) * TQ
    qblk = pl.BlockSpec((1, TQ, NSA_GQ_W), lambda b, g, t: (b, t, g))
    gblk = pl.BlockSpec((1, TQ, LANES), lambda b, g, t: (b, t, g))
    kvblk = lambda j: pl.BlockSpec((1, 1, S, LANES), lambda b, g, t: (g, b, 0, j))
    cblk = pl.BlockSpec((1, 1, LANES, LANES), lambda b, g, t: (b, g, 0, 0))
    wide = pltpu.VMEM((rows, LANES), F32)
    return pl.pallas_call(
        _nsa_kernel,
        grid=(B, G, S // TQ),
        in_specs=[qblk, gblk, kvblk(0), kvblk(1), kvblk(2), kvblk(3), cblk, cblk],
        out_specs=qblk,
        out_shape=jax.ShapeDtypeStruct((B, S, NSA_Q_W), BF16),
        scratch_shapes=[wide] * 7,
        compiler_params=pltpu.CompilerParams(dimension_semantics=("arbitrary",) * 3,
                                             vmem_limit_bytes=VMEM_LIMIT),
        name="nsa",
    )(nq, gates, kv, kv, kv, kv, kc, vc)


def _tail_kernel(final, x_ref, yr_ref, yn_ref, p_ref, gmix_ref, gmlp_ref, gple_ref, gfin_ref,
                 wmg_ref, wro_ref, wno_ref, wout_ref, wup_ref, wdn_ref, wpg_ref, wpp_ref, o_ref):
    x = x_ref[...]
    h = _rms(x, gmix_ref[...]).astype(BF16)
    o_ret = _dot(yr_ref[...], wro_ref[...])
    o_nsa = _dot(yn_ref[...], wno_ref[...])
    g_ret = jax.nn.sigmoid(_dot(h, wmg_ref[:, 0:D_MODEL]))
    g_nsa = jax.nn.sigmoid(_dot(h, wmg_ref[:, D_MODEL:2 * D_MODEL]))
    mix = (g_ret * o_ret + g_nsa * o_nsa).astype(BF16)
    x = x + _dot(mix, wout_ref[...])
    h2 = _rms(x, gmlp_ref[...]).astype(BF16)
    mlp = jnp.zeros_like(x)
    step = 1024
    for c0 in range(0, MLP_HIDDEN, step):
        up = jnp.maximum(_dot(h2, wup_ref[:, c0:c0 + step]), 0.0)
        mlp = mlp + _dot((up * up).astype(BF16), wdn_ref[c0:c0 + step, :])
    x = x + mlp
    h3 = _rms(x, gple_ref[...]).astype(BF16)
    ple_gate = jax.nn.sigmoid(_dot(h3, wpg_ref[...]))
    x = x + _dot(p_ref[...].astype(BF16), wpp_ref[...]) * ple_gate
    if final:
        x = _rms(x, gfin_ref[...])
    o_ref[...] = x


def _tail(final, x2, y_ret, y_nsa, p2, g_mix, g_mlp, g_ple, g_fin, w_mg, w_ro, w_no, w_out, w_up, w_dn,
          w_pg, w_pp):
    T = x2.shape[0]
    tm = TAIL_TM
    row = lambda w: pl.BlockSpec((tm, w), lambda i: (i, 0))
    gains = [_resident((1, D_MODEL))] * 4
    weights = [_resident(w.shape) for w in (w_mg, w_ro, w_no, w_out, w_up, w_dn, w_pg, w_pp)]
    return pl.pallas_call(
        functools.partial(_tail_kernel, final),
        grid=(T // tm,),
        in_specs=[row(D_MODEL), row(RET_V_W), row(NSA_Q_W), row(PLE_DIM)] + gains + weights,
        out_specs=row(D_MODEL),
        out_shape=jax.ShapeDtypeStruct((T, D_MODEL), F32),
        compiler_params=pltpu.CompilerParams(dimension_semantics=("arbitrary",),
                                             vmem_limit_bytes=VMEM_LIMIT),
        name="tail",
    )(x2, y_ret, y_nsa, p2, g_mix, g_mlp, g_ple, g_fin, w_mg, w_ro, w_no, w_out, w_up, w_dn, w_pg, w_pp)


def _pack_w_in(w):
    gate = w[:, C_GATE:C_GATE + 3 * NSA_HEADS]
    parts = [w[:, :C_GATE]]
    for g in range(NSA_GROUPS):
        cols = [j * NSA_HEADS + g * NSA_HPG + r for j in range(3) for r in range(NSA_HPG)]
        parts.append(jnp.pad(gate[:, jnp.array(cols)], ((0, 0), (0, LANES - len(cols)))))
    return jnp.concatenate(parts, axis=1).astype(BF16)


def kernel(x, p, positions, norm_mix_g, w_in, ret_gn_g, w_ret_o, cmp_pe_k, cmp_k_w1, cmp_k_w2, cmp_pe_v, cmp_v_w1, cmp_v_w2, w_nsa_o, w_merge_gate, w_out, norm_mlp_g, w_mlp_up, w_mlp_down, norm_ple_g, w_ple_gate, w_ple_proj, norm_final_g):
    B, S, D = x.shape
    depth = p.shape[0]
    T = B * S
    G = NSA_GROUPS
    bf = lambda a: a.astype(BF16)
    row = lambda a: a.reshape(1, -1)
    posf = positions.reshape(T, 1).astype(F32)
    inv_r = ROPE_THETA ** (-jnp.arange(0, RET_DK, 2, dtype=F32) / RET_DK)
    inv_n = ROPE_THETA ** (-jnp.arange(0, NSA_DH, 2, dtype=F32) / NSA_DH)
    inv_r = jnp.tile(inv_r, LANES // inv_r.shape[0]).reshape(1, LANES)
    inv_n = jnp.tile(inv_n, LANES // inv_n.shape[0]).reshape(1, LANES)
    n_rows = S // CMP_STRIDE

    def strides(t):
        t = t.reshape(B, S, G, NSA_DH).transpose(0, 2, 1, 3)
        return t.reshape(B, G, n_rows, CMP_STRIDE * NSA_DH)

    def pe_rows(pe):
        return jnp.broadcast_to(bf(pe).reshape(1, -1), (8, CMP_LEN * NSA_DH))

    dup_cols = lambda w: bf(jnp.concatenate([w, w], axis=1))
    pad_cols = lambda w: bf(jnp.concatenate([w, jnp.zeros_like(w)], axis=1))

    x2 = x.reshape(T, D)
    for i in range(depth):
        rq, rk, rv, rg, nq, cmp_kv, kv, gates = _proj(S, x2, posf, row(norm_mix_g[i]), inv_r, inv_n,
                                                       _pack_w_in(w_in[i]))
        kc, vc = _compress(strides(cmp_kv[:, :LANES]), strides(cmp_kv[:, LANES:]),
                           bf(cmp_k_w1[i]), pad_cols(cmp_k_w2[i]), pe_rows(cmp_pe_k[i]),
                           bf(cmp_v_w1[i]), dup_cols(cmp_v_w2[i]), pe_rows(cmp_pe_v[i]))
        sh3 = lambda a: a.reshape(B, S, a.shape[-1])
        y_ret = _retention(sh3(rq), sh3(rk), sh3(rv), sh3(rg), row(ret_gn_g[i]))
        y_nsa = _nsa(sh3(nq), sh3(gates), kv.reshape(G, B, S, kv.shape[-1]), kc, vc)
        x2 = _tail(i == depth - 1, x2, y_ret.reshape(T, RET_V_W), y_nsa.reshape(T, NSA_Q_W),
                   p[i].reshape(T, PLE_DIM), row(norm_mix_g[i]), row(norm_mlp_g[i]), row(norm_ple_g[i]),
                   row(norm_final_g), bf(w_merge_gate[i]), bf(w_ret_o[i]), bf(w_nsa_o[i]), bf(w_out[i]),
                   bf(w_mlp_up[i]), bf(w_mlp_down[i]), bf(w_ple_gate[i]), bf(w_ple_proj[i]))
    return x2.reshape(B, S, D)
```

```python
import functools
import math

import jax
import jax.numpy as jnp
from jax import lax
from jax.experimental import pallas as pl
from jax.experimental.pallas import tpu as pltpu

F32 = jnp.float32
BF16 = jnp.bfloat16

D_MODEL = 1024
PLE_DIM = 256
RMS_EPS = 1e-6
ROPE_THETA = 10000.0
RET_HEADS = 8
RET_DK = 128
RET_DV = 256
RET_CHUNK = 128
RET_STEP_CHUNKS = 4
RET_QK_W = RET_HEADS * RET_DK
RET_V_W = RET_HEADS * RET_DV
NSA_HEADS = 16
NSA_GROUPS = 2
NSA_HPG = 8
NSA_DH = 64
NSA_Q_W = NSA_HEADS * NSA_DH
NSA_GQ_W = NSA_HPG * NSA_DH
CMP_LEN = 32
CMP_STRIDE = 16
CMP_HIDDEN = 256
SEL_BLOCK = 64
SEL_TOPN = 8
WINDOW = 512
FORCE_SCORE = 1e6
MLP_HIDDEN = 4 * D_MODEL

LANES = 128
MASK_BIAS = -1e30
MAX_FLOOR = -1e29
LOG2E = math.log2(math.e)
VMEM_LIMIT = 56 * 1024 * 1024

PROJ_TM = 256
TAIL_TM = 256
NSA_TQ = 256
NSA_KT = 512
NSA_WK = WINDOW + NSA_TQ

C_RQ, C_RK, C_RV, C_RG = 0, 1024, 2048, 4096
C_NQ, C_KV, C_GATE, C_END = 6144, 7168, 7936, 8192
GATE_W = NSA_GROUPS * LANES
KV_TILE = {2: 0, 3: 1, 4: 3, 5: 4}
KV_TILES = 6


def _resident(shape):
    nd = len(shape)
    return pl.BlockSpec(shape, lambda *_: (0,) * nd, pipeline_mode=pl.Buffered(1))


def _rms(x, g):
    return x * lax.rsqrt(jnp.mean(x * x, axis=-1, keepdims=True) + RMS_EPS) * g


def _dot(a, b):
    return jnp.dot(a, b, preferred_element_type=F32)


def _dot_tb(a, b):
    return lax.dot_general(a, b, (((1,), (1,)), ((), ())), preferred_element_type=F32)


def _dot_ta(a, b):
    return lax.dot_general(a, b, (((0,), (0,)), ((), ())), preferred_element_type=F32)


def _proj_kernel(seq_len, x_ref, pos_ref, g_ref, invr_ref, invn_ref, w_ref,
                 rq_ref, rk_ref, rv_ref, rg_ref, nq_ref, cmp_ref, kv_ref, gate_ref):
    tm = x_ref.shape[0]
    h = _rms(x_ref[...], g_ref[...]).astype(BF16)
    pos = pos_ref[...]
    lane = lax.broadcasted_iota(jnp.int32, (tm, LANES), 1)
    ang_r = pos * invr_ref[...]
    cos_r = jnp.cos(ang_r)
    sin_r = jnp.sin(ang_r)
    sin_r = jnp.where(lane < 64, -sin_r, sin_r)
    ang_n = pos * invn_ref[...]
    cos_n = jnp.cos(ang_n)
    sin_n = jnp.sin(ang_n)
    low = (lane & 32) == 0
    sin_n = jnp.where(low, -sin_n, sin_n)

    def rope_r(y):
        return y * cos_r + pltpu.roll(y, 64, 1) * sin_r

    def rope_n(y):
        partner = jnp.where(low, pltpu.roll(y, 96, 1), pltpu.roll(y, 32, 1))
        return y * cos_n + partner * sin_n

    seq_pos = (pl.program_id(0) * tm) % seq_len + lax.broadcasted_iota(jnp.int32, (tm, LANES), 0)
    sel_tag = jnp.where(lane - NSA_DH == seq_pos // SEL_BLOCK, MASK_BIAS, 0.0)

    k_scale = RET_DK ** -0.5
    q_scale = NSA_DH ** -0.5 * LOG2E
    chunk = 512
    for c0 in range(0, C_END, chunk):
        y = _dot(h, w_ref[:, c0:c0 + chunk])
        for j in range(chunk // LANES):
            col = c0 + j * LANES
            piece = y[:, j * LANES:(j + 1) * LANES]
            if col < C_RK:
                rq_ref[:, col - C_RQ:col - C_RQ + LANES] = rope_r(piece).astype(BF16)
            elif col < C_RV:
                rk_ref[:, col - C_RK:col - C_RK + LANES] = (rope_r(piece) * k_scale).astype(BF16)
            elif col < C_RG:
                rv_ref[:, col - C_RV:col - C_RV + LANES] = piece.astype(BF16)
            elif col < C_NQ:
                rg_ref[:, col - C_RG:col - C_RG + LANES] = piece.astype(BF16)
            elif col < C_KV:
                nq_ref[:, col - C_NQ:col - C_NQ + LANES] = (rope_n(piece) * q_scale).astype(BF16)
            elif col < C_GATE:
                j_kv = (col - C_KV) // LANES
                is_key = j_kv % 2 == 0
                val = rope_n(piece) if is_key else piece
                if j_kv < 2:
                    cmp_ref[:, j_kv * LANES:(j_kv + 1) * LANES] = val.astype(BF16)
                    continue
                swapped = pltpu.roll(val, NSA_DH, 1)
                lower = lane < NSA_DH
                grouped = (val, swapped), (swapped, val)
                t0 = KV_TILE[j_kv]
                for g in range(NSA_GROUPS):
                    lo, hi = grouped[g]
                    if is_key:
                        fill = sel_tag if j_kv == 2 else jnp.zeros_like(val)
                        tiles = (jnp.where(lower, lo, fill),)
                    else:
                        tiles = (jnp.where(lower, lo, 1.0), jnp.where(lower, 1.0, hi))
                    for k, tile in enumerate(tiles):
                        kv_ref[g, :, (t0 + k) * LANES:(t0 + k + 1) * LANES] = tile.astype(BF16)
            else:
                gate_ref[:, col - C_GATE:col - C_GATE + LANES] = jax.nn.sigmoid(piece)


def _proj(seq_len, x2, posf, g_mix, inv_r, inv_n, w_all):
    T = x2.shape[0]
    tm = PROJ_TM
    assert seq_len % tm == 0
    row = lambda w: pl.BlockSpec((tm, w), lambda i: (i, 0))
    kv_w = KV_TILES * LANES
    out_shapes = [
        jax.ShapeDtypeStruct((T, RET_QK_W), BF16),
        jax.ShapeDtypeStruct((T, RET_QK_W), BF16),
        jax.ShapeDtypeStruct((T, RET_V_W), BF16),
        jax.ShapeDtypeStruct((T, RET_V_W), BF16),
        jax.ShapeDtypeStruct((T, NSA_Q_W), BF16),
        jax.ShapeDtypeStruct((T, 2 * LANES), BF16),
        jax.ShapeDtypeStruct((NSA_GROUPS, T, kv_w), BF16),
        jax.ShapeDtypeStruct((T, GATE_W), F32),
    ]
    return pl.pallas_call(
        functools.partial(_proj_kernel, seq_len),
        grid=(T // tm,),
        in_specs=[row(D_MODEL), row(1), _resident((1, D_MODEL)), _resident((1, LANES)),
                  _resident((1, LANES)), _resident((D_MODEL, C_END))],
        out_specs=[row(RET_QK_W), row(RET_QK_W), row(RET_V_W), row(RET_V_W), row(NSA_Q_W),
                   row(2 * LANES), pl.BlockSpec((NSA_GROUPS, tm, kv_w), lambda i: (0, i, 0)), row(GATE_W)],
        out_shape=out_shapes,
        compiler_params=pltpu.CompilerParams(dimension_semantics=("arbitrary",),
                                             vmem_limit_bytes=VMEM_LIMIT),
        name="proj",
    )(x2, posf, g_mix, inv_r, inv_n, w_all)


def _compress_kernel(k16_ref, v16_ref, w1k_ref, w2k_ref, pek_ref, w1v_ref, w2v_ref, pev_ref,
                     kc_ref, vc_ref):
    half = CMP_STRIDE * NSA_DH
    for x_ref, w1_ref, w2_ref, pe_ref, o_ref in ((k16_ref, w1k_ref, w2k_ref, pek_ref, kc_ref),
                                                 (v16_ref, w1v_ref, w2v_ref, pev_ref, vc_ref)):
        x = x_ref[0, 0]
        first = _dot(x, w1_ref[0:half, :])
        second = _dot(x, w1_ref[half:2 * half, :])
        pe_term = _dot(pe_ref[...], w1_ref[...])[0:1, :]
        hidden = first + pltpu.roll(second, second.shape[0] - 1, 0) + pe_term
        act = jax.nn.gelu(hidden).astype(BF16)
        o_ref[0, 0] = _dot(act, w2_ref[...]).astype(BF16)


def _compress(k16, v16, w1k, w2k, pek, w1v, w2v, pev):
    B, G, R, W = k16.shape
    blk = pl.BlockSpec((1, 1, R, W), lambda b, g: (b, g, 0, 0))
    oblk = pl.BlockSpec((1, 1, R, LANES), lambda b, g: (b, g, 0, 0))
    out = jax.ShapeDtypeStruct((B, G, R, LANES), BF16)
    wspecs = [_resident(w1k.shape), _resident(w2k.shape), _resident(pek.shape)]
    return pl.pallas_call(
        _compress_kernel,
        grid=(B, G),
        in_specs=[blk, blk] + wspecs + wspecs,
        out_specs=[oblk, oblk],
        out_shape=[out, out],
        compiler_params=pltpu.CompilerParams(dimension_semantics=("arbitrary", "arbitrary"),
                                             vmem_limit_bytes=VMEM_LIMIT),
        name="compress",
    )(k16, v16, w1k, w2k, pek, w1v, w2v, pev)


_RET_LOG_G = [math.log(1.0 - 2.0 ** (-5.0 - h)) for h in range(RET_HEADS)]


def _retention_kernel(q_ref, k_ref, v_ref, g_ref, gn_ref, o_ref, state_ref, decay_ref, xi_ref, zeta_ref):
    n = pl.program_id(1)
    C = RET_CHUNK

    @pl.when((pl.program_id(0) == 0) & (n == 0))
    def _tables():
        r = lax.broadcasted_iota(jnp.int32, (C, C), 0).astype(F32)
        c = lax.broadcasted_iota(jnp.int32, (C, C), 1).astype(F32)
        diff = r - c
        for h in range(RET_HEADS):
            lg = _RET_LOG_G[h]
            decay_ref[h] = jnp.where(diff >= 0, jnp.exp(jnp.maximum(diff, 0.0) * lg), 0.0)
            xi_ref[h] = jnp.exp((r + 1.0) * lg)
            zeta_ref[h] = jnp.exp((C - 1.0 - r) * lg)

    @pl.when(n == 0)
    def _reset():
        state_ref[...] = jnp.zeros_like(state_ref)

    for h in range(RET_HEADS):
        qs = slice(h * RET_DK, (h + 1) * RET_DK)
        vs = slice(h * RET_DV, (h + 1) * RET_DV)
        xi = xi_ref[h]
        xi = jnp.concatenate([xi, xi], axis=1)
        state = state_ref[h]
        for c in range(RET_STEP_CHUNKS):
            tok = slice(c * C, (c + 1) * C)
            qh = q_ref[0, tok, qs]
            kh = k_ref[0, tok, qs]
            vh = v_ref[0, tok, vs]
            inner = (_dot_tb(qh, kh) * decay_ref[h]).astype(BF16)
            o = _dot(inner, vh) + _dot(qh, state.astype(BF16)) * xi
            kz = (kh.astype(F32) * zeta_ref[h]).astype(BF16)
            state = math.exp(C * _RET_LOG_G[h]) * state + _dot_ta(kz, vh)
            mu = jnp.mean(o, axis=-1, keepdims=True)
            d = o - mu
            var = jnp.mean(d * d, axis=-1, keepdims=True)
            y = d * lax.rsqrt(var + RMS_EPS) * gn_ref[:, vs]
            g = g_ref[0, tok, vs].astype(F32)
            o_ref[0, tok, vs] = (y * (g * jax.nn.sigmoid(g))).astype(BF16)
        state_ref[h] = state


def _retention(rq, rk, rv, rg, gn_g):
    B, S, _ = rq.shape
    C = RET_CHUNK
    rows = RET_STEP_CHUNKS * C
    assert S % rows == 0
    blk = lambda w: pl.BlockSpec((1, rows, w), lambda b, n: (b, n, 0))
    return pl.pallas_call(
        _retention_kernel,
        grid=(B, S // rows),
        in_specs=[blk(RET_QK_W), blk(RET_QK_W), blk(RET_V_W), blk(RET_V_W), _resident((1, RET_V_W))],
        out_specs=blk(RET_V_W),
        out_shape=jax.ShapeDtypeStruct((B, S, RET_V_W), BF16),
        scratch_shapes=[pltpu.VMEM((RET_HEADS, RET_DK, RET_DV), F32),
                        pltpu.VMEM((RET_HEADS, C, C), F32),
                        pltpu.VMEM((RET_HEADS, C, C), F32),
                        pltpu.VMEM((RET_HEADS, C, C), F32)],
        compiler_params=pltpu.CompilerParams(dimension_semantics=("arbitrary", "arbitrary"),
                                             vmem_limit_bytes=VMEM_LIMIT),
        name="retention",
    )(rq, rk, rv, rg, gn_g)


NOTSEL_LANE0 = NSA_DH


def _nsa_kernel(q_ref, gate_ref, ksel_ref, vsel_e_ref, vsel_o_ref, kwin_ref, vwin_e_ref, vwin_o_ref,
                kc_ref, vc_ref, o_ref, acc_e, acc_o, m_e, m_o, out_acc, s_next):
    TQ, KT, WK = NSA_TQ, NSA_KT, NSA_WK
    HP = NSA_HPG // 2
    n_blk = ksel_ref.shape[2] // SEL_BLOCK
    q0 = pl.program_id(2) * TQ
    q = q_ref[0].astype(F32)
    q2 = jnp.concatenate([q[:, j * LANES:(j + 1) * LANES] for j in range(HP)], axis=0)
    q2_odd = pltpu.roll(q2, NSA_DH, 1)
    head_lanes = lax.broadcasted_iota(jnp.int32, q2.shape, 1) < NSA_DH
    t_col = q0 + lax.broadcasted_iota(jnp.int32, (TQ, 1), 0)
    lane = lax.broadcasted_iota(jnp.int32, (TQ, LANES), 1)
    even_half = lane < NSA_DH
    rows = [slice(hp * TQ, (hp + 1) * TQ) for hp in range(HP)]
    gates = gate_ref[0]

    def emit(branch, first, acc_pair, normalised):
        for hp in range(HP):
            c = branch * NSA_HPG + hp * 2
            a_e = acc_pair[0][rows[hp]]
            a_o = acc_pair[1][rows[hp]]
            we = gates[:, c:c + 1]
            wo = gates[:, c + 1:c + 2]
            if not normalised:
                we = we / a_e[:, NSA_DH:NSA_DH + 1]
                wo = wo / a_o[:, 0:1]
            contrib = jnp.where(even_half, a_e * we, a_o * wo)
            if first:
                out_acc[rows[hp]] = contrib
            else:
                out_acc[rows[hp]] += contrib

    zero_q = jnp.zeros_like(q2)
    q_plain = (jnp.where(head_lanes, q2, zero_q).astype(BF16), jnp.where(head_lanes, q2_odd, zero_q).astype(BF16))
    c_bias = jnp.where(lane * CMP_STRIDE + (CMP_LEN - 1) <= t_col, 0.0, MASK_BIAS)
    kc = kc_ref[0, 0]
    vc = vc_ref[0, 0]
    p_sum = jnp.zeros((TQ, LANES), F32)
    cmp_acc = []
    for qx in q_plain:
        s = _dot_tb(qx, kc)
        ps = []
        for hp in range(HP):
            sh = s[rows[hp]] + c_bias
            m = jnp.maximum(jnp.max(sh, axis=-1, keepdims=True), MAX_FLOOR)
            e = jnp.exp2(sh - m)
            l = jnp.sum(e, axis=-1, keepdims=True)
            p = e * jnp.where(l > 0, 1.0 / l, 0.0)
            p_sum = p_sum + p
            ps.append(p.astype(BF16))
        cmp_acc.append(_dot(jnp.concatenate(ps, axis=0), vc))
    emit(0, True, cmp_acc, True)

    w0 = pl.multiple_of(jnp.maximum(q0 - WINDOW, 0), TQ)
    kidx = w0 + lax.broadcasted_iota(jnp.int32, (TQ, WK), 1)
    w_bias = jnp.where((kidx <= t_col) & (kidx > t_col - WINDOW), 0.0, MASK_BIAS)
    kk = kwin_ref[0, 0, pl.ds(w0, WK), :]
    win_acc = []
    for qx, v_ref in zip(q_plain, (vwin_e_ref, vwin_o_ref)):
        s = _dot_tb(qx, kk)
        es = []
        for hp in range(HP):
            sh = s[rows[hp]] + w_bias
            es.append(jnp.exp2(sh - jnp.max(sh, axis=-1, keepdims=True)).astype(BF16))
        win_acc.append(_dot(jnp.concatenate(es, axis=0), v_ref[0, 0, pl.ds(w0, WK), :]))
    emit(2, False, win_acc, False)

    ni = lax.broadcasted_iota(jnp.int32, (LANES, LANES), 0)
    ci = lax.broadcasted_iota(jnp.int32, (LANES, LANES), 1)
    overlap_t = ((ci * CMP_STRIDE < ni * SEL_BLOCK + SEL_BLOCK)
                 & (ci * CMP_STRIDE + CMP_LEN - 1 >= ni * SEL_BLOCK) & (ni < n_blk))
    overlap_t = jnp.where(overlap_t, 1.0, 0.0).astype(BF16)
    p_hi = p_sum.astype(BF16)
    p_lo = (p_sum - p_hi.astype(F32)).astype(BF16)
    imp_t = (_dot_tb(overlap_t, p_hi) + _dot_tb(overlap_t, p_lo))[0:n_blk]
    blk_id = lax.broadcasted_iota(jnp.int32, (n_blk, TQ), 0)
    cur = (q0 + lax.broadcasted_iota(jnp.int32, (n_blk, TQ), 1)) >> 6
    forced = (blk_id == 0) | (blk_id == cur) | (blk_id == cur - 1)
    score = jnp.where(forced, FORCE_SCORE, jnp.where(blk_id <= cur, imp_t, -1.0))
    rank = jnp.zeros((n_blk, TQ), jnp.int32)
    for mblk in range(n_blk):
        other = score[mblk:mblk + 1, :]
        ahead = (other > score) | ((other == score) & (blk_id > mblk))
        rank = rank + jnp.where(ahead, 1, 0)
    not_sel_t = jnp.where(rank < SEL_TOPN, 0.0, 1.0)
    padded = jnp.concatenate([jnp.zeros((NOTSEL_LANE0, TQ), F32), not_sel_t,
                              jnp.zeros((LANES - NOTSEL_LANE0 - n_blk, TQ), F32)], axis=0)
    not_sel = jnp.concatenate([padded.T] * HP, axis=0)
    q_aug = (jnp.where(head_lanes, q2, not_sel).astype(BF16), jnp.where(head_lanes, q2_odd, not_sel).astype(BF16))

    for ref in (acc_e, acc_o):
        ref[...] = jnp.zeros_like(ref)
    for ref in (m_e, m_o):
        ref[...] = jnp.full_like(ref, MASK_BIAS)

    key_lane = lax.broadcasted_iota(jnp.int32, (TQ, KT), 1)
    last_k0 = ksel_ref.shape[2] - KT

    def even_scores(kt):
        k0 = pl.multiple_of(jnp.minimum(kt * KT, last_k0), KT)
        return _dot_tb(q_aug[0], ksel_ref[0, 0, pl.ds(k0, KT), :])

    def softmax_pv(get_scores, bias, vv, acc, m_ref):
        es, alphas = [], []
        for hp in range(HP):
            sh = get_scores(hp) + bias
            tiles = [sh[:, j * LANES:(j + 1) * LANES] for j in range(KT // LANES)]
            m_old = m_ref[rows[hp]]
            m_new = jnp.maximum(m_old, jnp.max(functools.reduce(jnp.maximum, tiles), axis=-1, keepdims=True))
            m_ref[rows[hp]] = m_new
            alphas.append(jnp.exp2(m_old - m_new))
            es.append(jnp.concatenate([jnp.exp2(t - m_new).astype(BF16) for t in tiles], axis=1))
        pv = _dot(jnp.concatenate(es, axis=0), vv)
        for hp in range(HP):
            acc[rows[hp]] = alphas[hp] * acc[rows[hp]] + pv[rows[hp]]

    s_next[...] = even_scores(0)

    def sel_tile(kt, carry):
        k0 = pl.multiple_of(kt * KT, KT)
        bias = jnp.where(key_lane <= t_col - k0, 0.0, MASK_BIAS)
        s_odd = _dot_tb(q_aug[1], ksel_ref[0, 0, pl.ds(k0, KT), :])
        softmax_pv(lambda hp: s_next[rows[hp], :], bias, vsel_e_ref[0, 0, pl.ds(k0, KT), :], acc_e, m_e)
        s_next[...] = even_scores(kt + 1)
        softmax_pv(lambda hp: s_odd[rows[hp]], bias, vsel_o_ref[0, 0, pl.ds(k0, KT), :], acc_o, m_o)
        return carry

    lax.fori_loop(0, (q0 + TQ + KT - 1) // KT, sel_tile, 0)
    emit(1, False, (acc_e, acc_o), False)

    o_ref[0] = jnp.concatenate([out_acc[rows[hp]] for hp in range(HP)], axis=1).astype(BF16)


def _nsa(nq, gates, kv, kc, vc):
    B, S, _ = nq.shape
    G = NSA_GROUPS
    TQ = NSA_TQ
    assert S // SEL_BLOCK <= LANES - NOTSEL_LANE0 and S % NSA_KT == 0 and S >= NSA_WK
    rows = (NSA_HPG // 2) * TQ
    qblk = pl.BlockSpec((1, TQ, NSA_GQ_W), lambda b, g, t: (b, t, g))
    gblk = pl.BlockSpec((1, TQ, LANES), lambda b, g, t: (b, t, g))
    kvblk = lambda j: pl.BlockSpec((1, 1, S, LANES), lambda b, g, t: (g, b, 0, j))
    cblk = pl.BlockSpec((1, 1, LANES, LANES), lambda b, g, t: (b, g, 0, 0))
    wide = pltpu.VMEM((rows, LANES), F32)
    return pl.pallas_call(
        _nsa_kernel,
        grid=(B, G, S // TQ),
        in_specs=[qblk, gblk] + [kvblk(j) for j in range(KV_TILES)] + [cblk, cblk],
        out_specs=qblk,
        out_shape=jax.ShapeDtypeStruct((B, S, NSA_Q_W), BF16),
        scratch_shapes=[wide] * 5 + [pltpu.VMEM((rows, NSA_KT), F32)],
        compiler_params=pltpu.CompilerParams(dimension_semantics=("arbitrary",) * 3,
                                             vmem_limit_bytes=VMEM_LIMIT),
        name="nsa",
    )(nq, gates, *([kv] * KV_TILES), kc, vc)


def _tail_kernel(final, x_ref, yr_ref, yn_ref, p_ref, gmix_ref, gmlp_ref, gple_ref, gfin_ref,
                 wmg_ref, wro_ref, wno_ref, wout_ref, wup_ref, wdn_ref, wpg_ref, wpp_ref, o_ref):
    x = x_ref[...]
    h = _rms(x, gmix_ref[...]).astype(BF16)
    o_ret = _dot(yr_ref[...], wro_ref[...])
    o_nsa = _dot(yn_ref[...], wno_ref[...])
    g_ret = jax.nn.sigmoid(_dot(h, wmg_ref[:, 0:D_MODEL]))
    g_nsa = jax.nn.sigmoid(_dot(h, wmg_ref[:, D_MODEL:2 * D_MODEL]))
    mix = (g_ret * o_ret + g_nsa * o_nsa).astype(BF16)
    x = x + _dot(mix, wout_ref[...])
    h2 = _rms(x, gmlp_ref[...]).astype(BF16)
    mlp = jnp.zeros_like(x)
    step = 1024
    for c0 in range(0, MLP_HIDDEN, step):
        up = jnp.maximum(_dot(h2, wup_ref[:, c0:c0 + step]), 0.0)
        mlp = mlp + _dot((up * up).astype(BF16), wdn_ref[c0:c0 + step, :])
    x = x + mlp
    h3 = _rms(x, gple_ref[...]).astype(BF16)
    ple_gate = jax.nn.sigmoid(_dot(h3, wpg_ref[...]))
    x = x + _dot(p_ref[...].astype(BF16), wpp_ref[...]) * ple_gate
    if final:
        x = _rms(x, gfin_ref[...])
    o_ref[...] = x


def _tail(final, x2, y_ret, y_nsa, p2, g_mix, g_mlp, g_ple, g_fin, w_mg, w_ro, w_no, w_out, w_up, w_dn,
          w_pg, w_pp):
    T = x2.shape[0]
    tm = TAIL_TM
    row = lambda w: pl.BlockSpec((tm, w), lambda i: (i, 0))
    gains = [_resident((1, D_MODEL))] * 4
    weights = [_resident(w.shape) for w in (w_mg, w_ro, w_no, w_out, w_up, w_dn, w_pg, w_pp)]
    return pl.pallas_call(
        functools.partial(_tail_kernel, final),
        grid=(T // tm,),
        in_specs=[row(D_MODEL), row(RET_V_W), row(NSA_Q_W), row(PLE_DIM)] + gains + weights,
        out_specs=row(D_MODEL),
        out_shape=jax.ShapeDtypeStruct((T, D_MODEL), F32),
        compiler_params=pltpu.CompilerParams(dimension_semantics=("arbitrary",),
                                             vmem_limit_bytes=VMEM_LIMIT),
        name="tail",
    )(x2, y_ret, y_nsa, p2, g_mix, g_mlp, g_ple, g_fin, w_mg, w_ro, w_no, w_out, w_up, w_dn, w_pg, w_pp)


def _pack_w_in(w):
    gate = w[:, C_GATE:C_GATE + 3 * NSA_HEADS]
    parts = [w[:, :C_GATE]]
    for g in range(NSA_GROUPS):
        cols = [j * NSA_HEADS + g * NSA_HPG + r for j in range(3) for r in range(NSA_HPG)]
        parts.append(jnp.pad(gate[:, jnp.array(cols)], ((0, 0), (0, LANES - len(cols)))))
    return jnp.concatenate(parts, axis=1).astype(BF16)


def kernel(x, p, positions, norm_mix_g, w_in, ret_gn_g, w_ret_o, cmp_pe_k, cmp_k_w1, cmp_k_w2, cmp_pe_v, cmp_v_w1, cmp_v_w2, w_nsa_o, w_merge_gate, w_out, norm_mlp_g, w_mlp_up, w_mlp_down, norm_ple_g, w_ple_gate, w_ple_proj, norm_final_g):
    B, S, D = x.shape
    depth = p.shape[0]
    T = B * S
    G = NSA_GROUPS
    bf = lambda a: a.astype(BF16)
    row = lambda a: a.reshape(1, -1)
    posf = positions.reshape(T, 1).astype(F32)
    inv_r = ROPE_THETA ** (-jnp.arange(0, RET_DK, 2, dtype=F32) / RET_DK)
    inv_n = ROPE_THETA ** (-jnp.arange(0, NSA_DH, 2, dtype=F32) / NSA_DH)
    inv_r = jnp.tile(inv_r, LANES // inv_r.shape[0]).reshape(1, LANES)
    inv_n = jnp.tile(inv_n, LANES // inv_n.shape[0]).reshape(1, LANES)
    n_rows = S // CMP_STRIDE

    def strides(t):
        t = t.reshape(B, S, G, NSA_DH).transpose(0, 2, 1, 3)
        return t.reshape(B, G, n_rows, CMP_STRIDE * NSA_DH)

    def pe_rows(pe):
        return jnp.broadcast_to(bf(pe).reshape(1, -1), (8, CMP_LEN * NSA_DH))

    dup_cols = lambda w: bf(jnp.concatenate([w, w], axis=1))
    pad_cols = lambda w: bf(jnp.concatenate([w, jnp.zeros_like(w)], axis=1))

    x2 = x.reshape(T, D)
    for i in range(depth):
        rq, rk, rv, rg, nq, cmp_kv, kv, gates = _proj(S, x2, posf, row(norm_mix_g[i]), inv_r, inv_n,
                                                       _pack_w_in(w_in[i]))
        kc, vc = _compress(strides(cmp_kv[:, :LANES]), strides(cmp_kv[:, LANES:]),
                           bf(cmp_k_w1[i]), pad_cols(cmp_k_w2[i]), pe_rows(cmp_pe_k[i]),
                           bf(cmp_v_w1[i]), dup_cols(cmp_v_w2[i]), pe_rows(cmp_pe_v[i]))
        sh3 = lambda a: a.reshape(B, S, a.shape[-1])
        y_ret = _retention(sh3(rq), sh3(rk), sh3(rv), sh3(rg), row(ret_gn_g[i]))
        y_nsa = _nsa(sh3(nq), sh3(gates), kv.reshape(G, B, S, kv.shape[-1]), kc, vc)
        x2 = _tail(i == depth - 1, x2, y_ret.reshape(T, RET_V_W), y_nsa.reshape(T, NSA_Q_W),
                   p[i].reshape(T, PLE_DIM), row(norm_mix_g[i]), row(norm_mlp_g[i]), row(norm_ple_g[i]),
                   row(norm_final_g), bf(w_merge_gate[i]), bf(w_ret_o[i]), bf(w_nsa_o[i]), bf(w_out[i]),
                   bf(w_mlp_up[i]), bf(w_mlp_down[i]), bf(w_ple_gate[i]), bf(w_ple_proj[i]))
    return x2.reshape(B, S, D)
```

```python
import functools
import math

import jax
import jax.numpy as jnp
from jax import lax
from jax.experimental import pallas as pl
from jax.experimental.pallas import tpu as pltpu

F32 = jnp.float32
BF16 = jnp.bfloat16

D_MODEL = 1024
PLE_DIM = 256
RMS_EPS = 1e-6
ROPE_THETA = 10000.0
RET_HEADS = 8
RET_DK = 128
RET_DV = 256
RET_CHUNK = 128
RET_STEP_CHUNKS = 4
RET_QK_W = RET_HEADS * RET_DK
RET_V_W = RET_HEADS * RET_DV
NSA_HEADS = 16
NSA_GROUPS = 2
NSA_HPG = 8
NSA_DH = 64
NSA_Q_W = NSA_HEADS * NSA_DH
NSA_GQ_W = NSA_HPG * NSA_DH
HEAD_PAIRS = NSA_HPG // 2
CMP_LEN = 32
CMP_STRIDE = 16
CMP_HIDDEN = 256
SEL_BLOCK = 64
SEL_TOPN = 8
WINDOW = 512
FORCE_SCORE = 1e6
MLP_HIDDEN = 4 * D_MODEL

LANES = 128
MASK_BIAS = -1e30
MAX_FLOOR = -1e29
LOG2E = math.log2(math.e)
VMEM_LIMIT = 56 * 1024 * 1024

PROJ_TM = 256
TAIL_TM = 256
NSA_TQ = 256
NSA_KT = 512
NSA_WK = WINDOW + NSA_TQ

C_RQ, C_RK, C_RV, C_RG = 0, 1024, 2048, 4096
C_NQ, C_KV, C_GATE, C_END = 6144, 7168, 7936, 8192
GATE_W = NSA_GROUPS * LANES
KV_TILE = {2: 0, 3: 1, 4: 3, 5: 4}
KV_TILES = 6


def _resident(shape):
    nd = len(shape)
    return pl.BlockSpec(shape, lambda *_: (0,) * nd, pipeline_mode=pl.Buffered(1))


def _rms(x, g):
    return x * lax.rsqrt(jnp.mean(x * x, axis=-1, keepdims=True) + RMS_EPS) * g


def _dot(a, b):
    return jnp.dot(a, b, preferred_element_type=F32)


def _dot_tb(a, b):
    return lax.dot_general(a, b, (((1,), (1,)), ((), ())), preferred_element_type=F32)


def _dot_ta(a, b):
    return lax.dot_general(a, b, (((0,), (0,)), ((), ())), preferred_element_type=F32)


def _proj_kernel(seq_len, x_ref, pos_ref, g_ref, invr_ref, invn_ref, w_ref,
                 rq_ref, rk_ref, rv_ref, rg_ref, nq_ref, cmp_ref, kv_ref, gate_ref):
    tm = x_ref.shape[0]
    h = _rms(x_ref[...], g_ref[...]).astype(BF16)
    pos = pos_ref[...]
    lane = lax.broadcasted_iota(jnp.int32, (tm, LANES), 1)
    ang_r = pos * invr_ref[...]
    cos_r = jnp.cos(ang_r)
    sin_r = jnp.sin(ang_r)
    sin_r = jnp.where(lane < 64, -sin_r, sin_r)
    ang_n = pos * invn_ref[...]
    cos_n = jnp.cos(ang_n)
    sin_n = jnp.sin(ang_n)
    low = (lane & 32) == 0
    sin_n = jnp.where(low, -sin_n, sin_n)

    def rope_r(y):
        return y * cos_r + pltpu.roll(y, 64, 1) * sin_r

    def rope_n(y):
        partner = jnp.where(low, pltpu.roll(y, 96, 1), pltpu.roll(y, 32, 1))
        return y * cos_n + partner * sin_n

    seq_pos = (pl.program_id(0) * tm) % seq_len + lax.broadcasted_iota(jnp.int32, (tm, LANES), 0)
    sel_tag = jnp.where(lane - NSA_DH == seq_pos // SEL_BLOCK, MASK_BIAS, 0.0)

    k_scale = RET_DK ** -0.5
    q_scale = NSA_DH ** -0.5 * LOG2E
    chunk = 512
    for c0 in range(0, C_END, chunk):
        y = _dot(h, w_ref[:, c0:c0 + chunk])
        for j in range(chunk // LANES):
            col = c0 + j * LANES
            piece = y[:, j * LANES:(j + 1) * LANES]
            if col < C_RK:
                rq_ref[:, col - C_RQ:col - C_RQ + LANES] = rope_r(piece).astype(BF16)
            elif col < C_RV:
                rk_ref[:, col - C_RK:col - C_RK + LANES] = (rope_r(piece) * k_scale).astype(BF16)
            elif col < C_RG:
                rv_ref[:, col - C_RV:col - C_RV + LANES] = piece.astype(BF16)
            elif col < C_NQ:
                rg_ref[:, col - C_RG:col - C_RG + LANES] = piece.astype(BF16)
            elif col < C_KV:
                val = rope_n(piece) * q_scale
                pair = (col - C_NQ) // LANES
                base = (pair // HEAD_PAIRS) * 2 * NSA_GQ_W + (pair % HEAD_PAIRS) * LANES
                zeros = jnp.zeros_like(val)
                nq_ref[:, base:base + LANES] = jnp.where(lane < NSA_DH, val, zeros).astype(BF16)
                nq_ref[:, base + NSA_GQ_W:base + NSA_GQ_W + LANES] = jnp.where(
                    lane < NSA_DH, pltpu.roll(val, NSA_DH, 1), zeros).astype(BF16)
            elif col < C_GATE:
                j_kv = (col - C_KV) // LANES
                is_key = j_kv % 2 == 0
                val = rope_n(piece) if is_key else piece
                if j_kv < 2:
                    cmp_ref[:, j_kv * LANES:(j_kv + 1) * LANES] = val.astype(BF16)
                    continue
                swapped = pltpu.roll(val, NSA_DH, 1)
                lower = lane < NSA_DH
                grouped = (val, swapped), (swapped, val)
                t0 = KV_TILE[j_kv]
                for g in range(NSA_GROUPS):
                    lo, hi = grouped[g]
                    if is_key:
                        fill = sel_tag if j_kv == 2 else jnp.zeros_like(val)
                        tiles = (jnp.where(lower, lo, fill),)
                    else:
                        tiles = (jnp.where(lower, 1.0, hi), jnp.where(lower, lo, 1.0))
                    for k, tile in enumerate(tiles):
                        kv_ref[g, :, (t0 + k) * LANES:(t0 + k + 1) * LANES] = tile.astype(BF16)
            else:
                gate_ref[:, col - C_GATE:col - C_GATE + LANES] = jax.nn.sigmoid(piece)


def _proj(seq_len, x2, posf, g_mix, inv_r, inv_n, w_all):
    T = x2.shape[0]
    tm = PROJ_TM
    assert seq_len % tm == 0
    row = lambda w: pl.BlockSpec((tm, w), lambda i: (i, 0))
    kv_w = KV_TILES * LANES
    out_shapes = [
        jax.ShapeDtypeStruct((T, RET_QK_W), BF16),
        jax.ShapeDtypeStruct((T, RET_QK_W), BF16),
        jax.ShapeDtypeStruct((T, RET_V_W), BF16),
        jax.ShapeDtypeStruct((T, RET_V_W), BF16),
        jax.ShapeDtypeStruct((T, 2 * NSA_Q_W), BF16),
        jax.ShapeDtypeStruct((T, 2 * LANES), BF16),
        jax.ShapeDtypeStruct((NSA_GROUPS, T, kv_w), BF16),
        jax.ShapeDtypeStruct((T, GATE_W), F32),
    ]
    return pl.pallas_call(
        functools.partial(_proj_kernel, seq_len),
        grid=(T // tm,),
        in_specs=[row(D_MODEL), row(1), _resident((1, D_MODEL)), _resident((1, LANES)),
                  _resident((1, LANES)), _resident((D_MODEL, C_END))],
        out_specs=[row(RET_QK_W), row(RET_QK_W), row(RET_V_W), row(RET_V_W), row(2 * NSA_Q_W),
                   row(2 * LANES), pl.BlockSpec((NSA_GROUPS, tm, kv_w), lambda i: (0, i, 0)), row(GATE_W)],
        out_shape=out_shapes,
        compiler_params=pltpu.CompilerParams(dimension_semantics=("arbitrary",),
                                             vmem_limit_bytes=VMEM_LIMIT),
        name="proj",
    )(x2, posf, g_mix, inv_r, inv_n, w_all)


def _compress_kernel(k16_ref, v16_ref, w1k_ref, w2k_ref, pek_ref, w1v_ref, w2v_ref, pev_ref,
                     kc_ref, vc_ref):
    half = CMP_STRIDE * NSA_DH
    for x_ref, w1_ref, w2_ref, pe_ref, o_ref in ((k16_ref, w1k_ref, w2k_ref, pek_ref, kc_ref),
                                                 (v16_ref, w1v_ref, w2v_ref, pev_ref, vc_ref)):
        x = x_ref[0, 0]
        first = _dot(x, w1_ref[0:half, :])
        second = _dot(x, w1_ref[half:2 * half, :])
        pe_term = _dot(pe_ref[...], w1_ref[...])[0:1, :]
        hidden = first + pltpu.roll(second, second.shape[0] - 1, 0) + pe_term
        act = jax.nn.gelu(hidden).astype(BF16)
        o_ref[0, 0] = _dot(act, w2_ref[...]).astype(BF16)


def _compress(k16, v16, w1k, w2k, pek, w1v, w2v, pev):
    B, G, R, W = k16.shape
    blk = pl.BlockSpec((1, 1, R, W), lambda b, g: (b, g, 0, 0))
    oblk = pl.BlockSpec((1, 1, R, LANES), lambda b, g: (b, g, 0, 0))
    out = jax.ShapeDtypeStruct((B, G, R, LANES), BF16)
    wspecs = [_resident(w1k.shape), _resident(w2k.shape), _resident(pek.shape)]
    return pl.pallas_call(
        _compress_kernel,
        grid=(B, G),
        in_specs=[blk, blk] + wspecs + wspecs,
        out_specs=[oblk, oblk],
        out_shape=[out, out],
        compiler_params=pltpu.CompilerParams(dimension_semantics=("arbitrary", "arbitrary"),
                                             vmem_limit_bytes=VMEM_LIMIT),
        name="compress",
    )(k16, v16, w1k, w2k, pek, w1v, w2v, pev)


_RET_LOG_G = [math.log(1.0 - 2.0 ** (-5.0 - h)) for h in range(RET_HEADS)]


def _retention_kernel(q_ref, k_ref, v_ref, g_ref, gn_ref, o_ref, state_ref, decay_ref, xi_ref, zeta_ref):
    n = pl.program_id(1)
    C = RET_CHUNK

    @pl.when((pl.program_id(0) == 0) & (n == 0))
    def _tables():
        r = lax.broadcasted_iota(jnp.int32, (C, C), 0).astype(F32)
        c = lax.broadcasted_iota(jnp.int32, (C, C), 1).astype(F32)
        diff = r - c
        for h in range(RET_HEADS):
            lg = _RET_LOG_G[h]
            decay_ref[h] = jnp.where(diff >= 0, jnp.exp(jnp.maximum(diff, 0.0) * lg), 0.0)
            xi_ref[h] = jnp.exp((r + 1.0) * lg)
            zeta_ref[h] = jnp.exp((C - 1.0 - r) * lg)

    @pl.when(n == 0)
    def _reset():
        state_ref[...] = jnp.zeros_like(state_ref)

    for h in range(RET_HEADS):
        qs = slice(h * RET_DK, (h + 1) * RET_DK)
        vs = slice(h * RET_DV, (h + 1) * RET_DV)
        xi = xi_ref[h]
        xi = jnp.concatenate([xi, xi], axis=1)
        state = state_ref[h]
        for c in range(RET_STEP_CHUNKS):
            tok = slice(c * C, (c + 1) * C)
            qh = q_ref[0, tok, qs]
            kh = k_ref[0, tok, qs]
            vh = v_ref[0, tok, vs]
            inner = (_dot_tb(qh, kh) * decay_ref[h]).astype(BF16)
            o = _dot(inner, vh) + _dot(qh, state.astype(BF16)) * xi
            kz = (kh.astype(F32) * zeta_ref[h]).astype(BF16)
            state = math.exp(C * _RET_LOG_G[h]) * state + _dot_ta(kz, vh)
            mu = jnp.mean(o, axis=-1, keepdims=True)
            d = o - mu
            var = jnp.mean(d * d, axis=-1, keepdims=True)
            y = d * lax.rsqrt(var + RMS_EPS) * gn_ref[:, vs]
            g = g_ref[0, tok, vs].astype(F32)
            o_ref[0, tok, vs] = (y * (g * jax.nn.sigmoid(g))).astype(BF16)
        state_ref[h] = state


def _retention(rq, rk, rv, rg, gn_g):
    B, S, _ = rq.shape
    C = RET_CHUNK
    rows = RET_STEP_CHUNKS * C
    assert S % rows == 0
    blk = lambda w: pl.BlockSpec((1, rows, w), lambda b, n: (b, n, 0))
    return pl.pallas_call(
        _retention_kernel,
        grid=(B, S // rows),
        in_specs=[blk(RET_QK_W), blk(RET_QK_W), blk(RET_V_W), blk(RET_V_W), _resident((1, RET_V_W))],
        out_specs=blk(RET_V_W),
        out_shape=jax.ShapeDtypeStruct((B, S, RET_V_W), BF16),
        scratch_shapes=[pltpu.VMEM((RET_HEADS, RET_DK, RET_DV), F32),
                        pltpu.VMEM((RET_HEADS, C, C), F32),
                        pltpu.VMEM((RET_HEADS, C, C), F32),
                        pltpu.VMEM((RET_HEADS, C, C), F32)],
        compiler_params=pltpu.CompilerParams(dimension_semantics=("arbitrary", "arbitrary"),
                                             vmem_limit_bytes=VMEM_LIMIT),
        name="retention",
    )(rq, rk, rv, rg, gn_g)


NOTSEL_LANE0 = NSA_DH


def _nsa_kernel(q_ref, gate_ref, ksel_ref, vsel_e_ref, vsel_o_ref, kwin_ref, vwin_e_ref, vwin_o_ref,
                kc_ref, vc_ref, wbias_ref, cbias_ref, o_ref,
                acc_e, acc_o, m_e, m_o, out_acc, s_next, gate_tiles):
    TQ, KT, WK = NSA_TQ, NSA_KT, NSA_WK
    HP = HEAD_PAIRS
    n_blk = ksel_ref.shape[2] // SEL_BLOCK
    qt = pl.program_id(2)
    q0 = qt * TQ
    q_plain = tuple(jnp.concatenate([q_ref[0, :, (x * HP + hp) * LANES:(x * HP + hp + 1) * LANES]
                                     for hp in range(HP)], axis=0) for x in range(2))
    t_col = q0 + lax.broadcasted_iota(jnp.int32, (TQ, 1), 0)
    lane = lax.broadcasted_iota(jnp.int32, (TQ, LANES), 1)
    lower_half = lane < NSA_DH
    rows = [slice(hp * TQ, (hp + 1) * TQ) for hp in range(HP)]
    gates = gate_ref[0]

    for branch in range(3):
        for hp in range(HP):
            c = branch * NSA_HPG + hp * 2
            gate_tiles[branch * HP + hp] = jnp.where(lower_half, gates[:, c + 1:c + 2], gates[:, c:c + 1])

    def emit(branch, first, acc_pair, normalised):
        for hp in range(HP):
            a_e = acc_pair[0][rows[hp]]
            a_o = acc_pair[1][rows[hp]]
            weight = gate_tiles[branch * HP + hp]
            if not normalised:
                weight = weight / pltpu.roll(jnp.where(lower_half, a_e, a_o), NSA_DH, 1)
            contrib = jnp.where(lower_half, a_o, a_e) * weight
            if first:
                out_acc[rows[hp]] = contrib
            else:
                out_acc[rows[hp]] += contrib

    c_bias = jnp.where(lane * CMP_STRIDE + (CMP_LEN - 1) <= t_col, 0.0, MASK_BIAS)
    kc = kc_ref[0, 0]
    vc = vc_ref[0, 0]
    p_sum = jnp.zeros((TQ, LANES), F32)
    cmp_acc = []
    for qx in q_plain:
        s = _dot_tb(qx, kc)
        ps = []
        for hp in range(HP):
            sh = s[rows[hp]] + c_bias
            m = jnp.maximum(jnp.max(sh, axis=-1, keepdims=True), MAX_FLOOR)
            e = jnp.exp2(sh - m)
            l = jnp.sum(e, axis=-1, keepdims=True)
            p = e * jnp.where(l > 0, 1.0 / l, 0.0)
            p_sum = p_sum + p
            ps.append(p.astype(BF16))
        cmp_acc.append(_dot(jnp.concatenate(ps, axis=0), vc))
    emit(0, True, cmp_acc, True)

    w0 = pl.multiple_of(jnp.maximum(q0 - WINDOW, 0), TQ)
    w_bias = wbias_ref[jnp.minimum(qt, WINDOW // TQ)]
    kk = kwin_ref[0, 0, pl.ds(w0, WK), :]
    win_acc = []
    for qx, v_ref in zip(q_plain, (vwin_e_ref, vwin_o_ref)):
        s = _dot_tb(qx, kk)
        es = []
        for hp in range(HP):
            sh = s[rows[hp]] + w_bias
            es.append(jnp.exp2(sh - jnp.max(sh, axis=-1, keepdims=True)).astype(BF16))
        win_acc.append(_dot(jnp.concatenate(es, axis=0), v_ref[0, 0, pl.ds(w0, WK), :]))
    emit(2, False, win_acc, False)

    ni = lax.broadcasted_iota(jnp.int32, (LANES, LANES), 0)
    ci = lax.broadcasted_iota(jnp.int32, (LANES, LANES), 1)
    overlap_t = ((ci * CMP_STRIDE < ni * SEL_BLOCK + SEL_BLOCK)
                 & (ci * CMP_STRIDE + CMP_LEN - 1 >= ni * SEL_BLOCK) & (ni < n_blk))
    overlap_t = jnp.where(overlap_t, 1.0, 0.0).astype(BF16)
    p_hi = p_sum.astype(BF16)
    p_lo = (p_sum - p_hi.astype(F32)).astype(BF16)
    imp_t = (_dot_tb(overlap_t, p_hi) + _dot_tb(overlap_t, p_lo))[0:n_blk]
    blk_id = lax.broadcasted_iota(jnp.int32, (n_blk, TQ), 0)
    cur = (q0 + lax.broadcasted_iota(jnp.int32, (n_blk, TQ), 1)) >> 6
    forced = (blk_id == 0) | (blk_id == cur) | (blk_id == cur - 1)
    score = jnp.where(forced, FORCE_SCORE, jnp.where(blk_id <= cur, imp_t, -1.0))
    rank = jnp.zeros((n_blk, TQ), jnp.int32)
    for mblk in range(n_blk):
        other = score[mblk:mblk + 1, :]
        ahead = (other > score) | ((other == score) & (blk_id > mblk))
        rank = rank + jnp.where(ahead, 1, 0)
    not_sel_t = jnp.where(rank < SEL_TOPN, 0.0, 1.0)
    padded = jnp.concatenate([jnp.zeros((NOTSEL_LANE0, TQ), F32), not_sel_t,
                              jnp.zeros((LANES - NOTSEL_LANE0 - n_blk, TQ), F32)], axis=0)
    not_sel = jnp.concatenate([padded.T.astype(BF16)] * HP, axis=0)
    q_aug = tuple(qx + not_sel for qx in q_plain)

    for ref in (acc_e, acc_o):
        ref[...] = jnp.zeros_like(ref)
    for ref in (m_e, m_o):
        ref[...] = jnp.full_like(ref, MASK_BIAS)

    last_k0 = ksel_ref.shape[2] - KT

    def even_scores(kt):
        k0 = pl.multiple_of(jnp.minimum(kt * KT, last_k0), KT)
        return _dot_tb(q_aug[0], ksel_ref[0, 0, pl.ds(k0, KT), :])

    def softmax_pv(get_scores, bias, vv, acc, m_ref):
        es, alphas = [], []
        for hp in range(HP):
            sh = get_scores(hp) + bias
            tiles = [sh[:, j * LANES:(j + 1) * LANES] for j in range(KT // LANES)]
            m_old = m_ref[rows[hp]]
            m_new = jnp.maximum(m_old, jnp.max(functools.reduce(jnp.maximum, tiles), axis=-1, keepdims=True))
            m_ref[rows[hp]] = m_new
            alphas.append(jnp.exp2(m_old - m_new))
            es.append(jnp.concatenate([jnp.exp2(t - m_new).astype(BF16) for t in tiles], axis=1))
        pv = _dot(jnp.concatenate(es, axis=0), vv)
        for hp in range(HP):
            acc[rows[hp]] = alphas[hp] * acc[rows[hp]] + pv[rows[hp]]

    s_next[...] = even_scores(0)

    def sel_tile(kt, carry):
        k0 = pl.multiple_of(kt * KT, KT)
        bias = cbias_ref[jnp.minimum(qt - kt * (KT // TQ), KT // TQ)]
        s_odd = _dot_tb(q_aug[1], ksel_ref[0, 0, pl.ds(k0, KT), :])
        softmax_pv(lambda hp: s_next[rows[hp], :], bias, vsel_e_ref[0, 0, pl.ds(k0, KT), :], acc_e, m_e)
        s_next[...] = even_scores(kt + 1)
        softmax_pv(lambda hp: s_odd[rows[hp]], bias, vsel_o_ref[0, 0, pl.ds(k0, KT), :], acc_o, m_o)
        return carry

    lax.fori_loop(0, (q0 + TQ + KT - 1) // KT, sel_tile, 0)
    emit(1, False, (acc_e, acc_o), False)

    o_ref[0] = jnp.concatenate([out_acc[rows[hp]] for hp in range(HP)], axis=1).astype(BF16)


def _nsa(nq, gates, kv, kc, vc):
    B, S, _ = nq.shape
    G = NSA_GROUPS
    TQ = NSA_TQ
    assert S // SEL_BLOCK <= LANES - NOTSEL_LANE0 and S % NSA_KT == 0 and S >= NSA_WK
    assert NSA_KT % TQ == 0 and WINDOW % TQ == 0
    rows = HEAD_PAIRS * TQ
    i = jnp.arange(TQ, dtype=jnp.int32)[None, :, None]
    off = lambda n: jnp.arange(n + 1, dtype=jnp.int32)[:, None, None] * TQ
    j = jnp.arange(NSA_WK, dtype=jnp.int32)[None, None, :]
    t_rel = off(WINDOW // TQ) + i
    win_bias = jnp.where((j <= t_rel) & (j > t_rel - WINDOW), 0.0, MASK_BIAS).astype(F32)
    j = jnp.arange(NSA_KT, dtype=jnp.int32)[None, None, :]
    causal_bias = jnp.where(j <= off(NSA_KT // TQ) + i, 0.0, MASK_BIAS).astype(F32)
    qblk = pl.BlockSpec((1, TQ, 2 * NSA_GQ_W), lambda b, g, t: (b, t, g))
    oblk = pl.BlockSpec((1, TQ, NSA_GQ_W), lambda b, g, t: (b, t, g))
    gblk = pl.BlockSpec((1, TQ, LANES), lambda b, g, t: (b, t, g))
    kvblk = lambda j: pl.BlockSpec((1, 1, S, LANES), lambda b, g, t: (g, b, 0, j))
    cblk = pl.BlockSpec((1, 1, LANES, LANES), lambda b, g, t: (b, g, 0, 0))
    wide = pltpu.VMEM((rows, LANES), F32)
    return pl.pallas_call(
        _nsa_kernel,
        grid=(B, G, S // TQ),
        in_specs=([qblk, gblk] + [kvblk(j) for j in range(KV_TILES)] + [cblk, cblk]
                  + [_resident(win_bias.shape), _resident(causal_bias.shape)]),
        out_specs=oblk,
        out_shape=jax.ShapeDtypeStruct((B, S, NSA_Q_W), BF16),
        scratch_shapes=[wide] * 5 + [pltpu.VMEM((rows, NSA_KT), F32),
                                     pltpu.VMEM((3 * NSA_HPG // 2, TQ, LANES), F32)],
        compiler_params=pltpu.CompilerParams(dimension_semantics=("arbitrary",) * 3,
                                             vmem_limit_bytes=VMEM_LIMIT),
        name="nsa",
    )(nq, gates, *([kv] * KV_TILES), kc, vc, win_bias, causal_bias)


def _tail_kernel(final, x_ref, yr_ref, yn_ref, p_ref, gmix_ref, gmlp_ref, gple_ref, gfin_ref,
                 wmg_ref, wro_ref, wno_ref, wout_ref, wup_ref, wdn_ref, wpg_ref, wpp_ref, o_ref):
    x = x_ref[...]
    h = _rms(x, gmix_ref[...]).astype(BF16)
    o_ret = _dot(yr_ref[...], wro_ref[...])
    o_nsa = _dot(yn_ref[...], wno_ref[...])
    g_ret = jax.nn.sigmoid(_dot(h, wmg_ref[:, 0:D_MODEL]))
    g_nsa = jax.nn.sigmoid(_dot(h, wmg_ref[:, D_MODEL:2 * D_MODEL]))
    mix = (g_ret * o_ret + g_nsa * o_nsa).astype(BF16)
    x = x + _dot(mix, wout_ref[...])
    h2 = _rms(x, gmlp_ref[...]).astype(BF16)
    mlp = jnp.zeros_like(x)
    step = 1024
    for c0 in range(0, MLP_HIDDEN, step):
        up = jnp.maximum(_dot(h2, wup_ref[:, c0:c0 + step]), 0.0)
        mlp = mlp + _dot((up * up).astype(BF16), wdn_ref[c0:c0 + step, :])
    x = x + mlp
    h3 = _rms(x, gple_ref[...]).astype(BF16)
    ple_gate = jax.nn.sigmoid(_dot(h3, wpg_ref[...]))
    x = x + _dot(p_ref[...].astype(BF16), wpp_ref[...]) * ple_gate
    if final:
        x = _rms(x, gfin_ref[...])
    o_ref[...] = x


def _tail(final, x2, y_ret, y_nsa, p2, g_mix, g_mlp, g_ple, g_fin, w_mg, w_ro, w_no, w_out, w_up, w_dn,
          w_pg, w_pp):
    T = x2.shape[0]
    tm = TAIL_TM
    row = lambda w: pl.BlockSpec((tm, w), lambda i: (i, 0))
    gains = [_resident((1, D_MODEL))] * 4
    weights = [_resident(w.shape) for w in (w_mg, w_ro, w_no, w_out, w_up, w_dn, w_pg, w_pp)]
    return pl.pallas_call(
        functools.partial(_tail_kernel, final),
        grid=(T // tm,),
        in_specs=[row(D_MODEL), row(RET_V_W), row(NSA_Q_W), row(PLE_DIM)] + gains + weights,
        out_specs=row(D_MODEL),
        out_shape=jax.ShapeDtypeStruct((T, D_MODEL), F32),
        compiler_params=pltpu.CompilerParams(dimension_semantics=("arbitrary",),
                                             vmem_limit_bytes=VMEM_LIMIT),
        name="tail",
    )(x2, y_ret, y_nsa, p2, g_mix, g_mlp, g_ple, g_fin, w_mg, w_ro, w_no, w_out, w_up, w_dn, w_pg, w_pp)


def _pack_w_in(w):
    gate = w[:, C_GATE:C_GATE + 3 * NSA_HEADS]
    parts = [w[:, :C_GATE]]
    for g in range(NSA_GROUPS):
        cols = [j * NSA_HEADS + g * NSA_HPG + r for j in range(3) for r in range(NSA_HPG)]
        parts.append(jnp.pad(gate[:, jnp.array(cols)], ((0, 0), (0, LANES - len(cols)))))
    return jnp.concatenate(parts, axis=1).astype(BF16)


def kernel(x, p, positions, norm_mix_g, w_in, ret_gn_g, w_ret_o, cmp_pe_k, cmp_k_w1, cmp_k_w2, cmp_pe_v, cmp_v_w1, cmp_v_w2, w_nsa_o, w_merge_gate, w_out, norm_mlp_g, w_mlp_up, w_mlp_down, norm_ple_g, w_ple_gate, w_ple_proj, norm_final_g):
    B, S, D = x.shape
    depth = p.shape[0]
    T = B * S
    G = NSA_GROUPS
    bf = lambda a: a.astype(BF16)
    row = lambda a: a.reshape(1, -1)
    posf = positions.reshape(T, 1).astype(F32)
    inv_r = ROPE_THETA ** (-jnp.arange(0, RET_DK, 2, dtype=F32) / RET_DK)
    inv_n = ROPE_THETA ** (-jnp.arange(0, NSA_DH, 2, dtype=F32) / NSA_DH)
    inv_r = jnp.tile(inv_r, LANES // inv_r.shape[0]).reshape(1, LANES)
    inv_n = jnp.tile(inv_n, LANES // inv_n.shape[0]).reshape(1, LANES)
    n_rows = S // CMP_STRIDE

    def strides(t):
        t = t.reshape(B, S, G, NSA_DH).transpose(0, 2, 1, 3)
        return t.reshape(B, G, n_rows, CMP_STRIDE * NSA_DH)

    def pe_rows(pe):
        return jnp.broadcast_to(bf(pe).reshape(1, -1), (8, CMP_LEN * NSA_DH))

    def swap_pairs(w):
        return w.reshape(NSA_HEADS // 2, 2, NSA_DH, -1)[:, ::-1].reshape(w.shape)

    dup_cols = lambda w: bf(jnp.concatenate([w, w], axis=1))
    pad_cols = lambda w: bf(jnp.concatenate([w, jnp.zeros_like(w)], axis=1))

    x2 = x.reshape(T, D)
    for i in range(depth):
        rq, rk, rv, rg, nq, cmp_kv, kv, gates = _proj(S, x2, posf, row(norm_mix_g[i]), inv_r, inv_n,
                                                       _pack_w_in(w_in[i]))
        kc, vc = _compress(strides(cmp_kv[:, :LANES]), strides(cmp_kv[:, LANES:]),
                           bf(cmp_k_w1[i]), pad_cols(cmp_k_w2[i]), pe_rows(cmp_pe_k[i]),
                           bf(cmp_v_w1[i]), dup_cols(cmp_v_w2[i]), pe_rows(cmp_pe_v[i]))
        sh3 = lambda a: a.reshape(B, S, a.shape[-1])
        y_ret = _retention(sh3(rq), sh3(rk), sh3(rv), sh3(rg), row(ret_gn_g[i]))
        y_nsa = _nsa(sh3(nq), sh3(gates), kv.reshape(G, B, S, kv.shape[-1]), kc, vc)
        x2 = _tail(i == depth - 1, x2, y_ret.reshape(T, RET_V_W), y_nsa.reshape(T, NSA_Q_W),
                   p[i].reshape(T, PLE_DIM), row(norm_mix_g[i]), row(norm_mlp_g[i]), row(norm_ple_g[i]),
                   row(norm_final_g), bf(w_merge_gate[i]), bf(w_ret_o[i]), bf(swap_pairs(w_nsa_o[i])), bf(w_out[i]),
                   bf(w_mlp_up[i]), bf(w_mlp_down[i]), bf(w_ple_gate[i]), bf(w_ple_proj[i]))
    return x2.reshape(B, S, D)
```

```python
import functools
import math

import jax
import jax.numpy as jnp
from jax import lax
from jax.experimental import pallas as pl
from jax.experimental.pallas import tpu as pltpu

F32 = jnp.float32
BF16 = jnp.bfloat16

D_MODEL = 1024
PLE_DIM = 256
RMS_EPS = 1e-6
ROPE_THETA = 10000.0
RET_HEADS = 8
RET_DK = 128
RET_DV = 256
RET_CHUNK = 128
RET_STEP_CHUNKS = 4
RET_QK_W = RET_HEADS * RET_DK
RET_V_W = RET_HEADS * RET_DV
NSA_HEADS = 16
NSA_GROUPS = 2
NSA_HPG = 8
NSA_DH = 64
NSA_Q_W = NSA_HEADS * NSA_DH
NSA_GQ_W = NSA_HPG * NSA_DH
HEAD_PAIRS = NSA_HPG // 2
CMP_LEN = 32
CMP_STRIDE = 16
CMP_HIDDEN = 256
SEL_BLOCK = 64
SEL_TOPN = 8
WINDOW = 512
FORCE_SCORE = 1e6
MLP_HIDDEN = 4 * D_MODEL

LANES = 128
MASK_BIAS = -1e30
MAX_FLOOR = -1e29
LOG2E = math.log2(math.e)
VMEM_LIMIT = 56 * 1024 * 1024

PROJ_TM = 512
TAIL_TM = 512
NSA_TQ = 256
NSA_KT = 512
NSA_WK = WINDOW + NSA_TQ

C_RQ, C_RK, C_RV, C_RG = 0, 1024, 2048, 4096
C_NQ, C_KV, C_GATE, C_END = 6144, 7168, 7936, 8192
GATE_W = NSA_GROUPS * LANES
KV_TILE = {2: 0, 3: 1, 4: 3, 5: 4}
KV_TILES = 6


def _resident(shape):
    nd = len(shape)
    return pl.BlockSpec(shape, lambda *_: (0,) * nd, pipeline_mode=pl.Buffered(1))


def _rms(x, g):
    return x * lax.rsqrt(jnp.mean(x * x, axis=-1, keepdims=True) + RMS_EPS) * g


def _dot(a, b):
    return jnp.dot(a, b, preferred_element_type=F32)


def _dot_tb(a, b):
    return lax.dot_general(a, b, (((1,), (1,)), ((), ())), preferred_element_type=F32)


def _dot_ta(a, b):
    return lax.dot_general(a, b, (((0,), (0,)), ((), ())), preferred_element_type=F32)


def _proj_kernel(seq_len, x_ref, pos_ref, g_ref, inv_ref, w_ref,
                 rq_ref, rk_ref, rv_ref, rg_ref, nq_ref, cmp_ref, kv_ref, gate_ref):
    tm = x_ref.shape[0]
    h = _rms(x_ref[...], g_ref[...]).astype(BF16)
    pos = pos_ref[...]
    lane = lax.broadcasted_iota(jnp.int32, (tm, LANES), 1)
    ang = pos * inv_ref[...]
    cos_a = jnp.cos(ang)
    sin_a = jnp.sin(ang)
    cos_r = jnp.where(lane < 64, cos_a, pltpu.roll(cos_a, 64, 1))
    sin_r = jnp.where(lane < 64, -sin_a, pltpu.roll(sin_a, 64, 1))

    def tile_nsa(t):
        return jnp.where(lane < 32, pltpu.roll(t, 64, 1),
                         jnp.where(lane < 64, pltpu.roll(t, 96, 1), jnp.where(lane < 96, t, pltpu.roll(t, 32, 1))))

    low = (lane & 32) == 0
    cos_n = tile_nsa(cos_a)
    sin_n = tile_nsa(sin_a)
    sin_n = jnp.where(low, -sin_n, sin_n)

    def rope_r(y):
        return y * cos_r + pltpu.roll(y, 64, 1) * sin_r

    def rope_n(y):
        partner = jnp.where(low, pltpu.roll(y, 96, 1), pltpu.roll(y, 32, 1))
        return y * cos_n + partner * sin_n

    seq_pos = (pl.program_id(0) * tm) % seq_len + lax.broadcasted_iota(jnp.int32, (tm, LANES), 0)
    sel_tag = jnp.where(lane - NSA_DH == seq_pos // SEL_BLOCK, MASK_BIAS, 0.0)

    k_scale = RET_DK ** -0.5
    q_scale = NSA_DH ** -0.5 * LOG2E
    chunk = 512
    order = (list(range(C_RV, C_RG, chunk)) + list(range(C_RQ, C_RV, chunk)) + list(range(C_NQ, C_END, chunk))
             + list(range(C_RG, C_NQ, chunk)))
    for c0 in order:
        y = _dot(h, w_ref[:, c0:c0 + chunk])
        for j in range(chunk // LANES):
            col = c0 + j * LANES
            piece = y[:, j * LANES:(j + 1) * LANES]
            if col < C_RK:
                rq_ref[:, col - C_RQ:col - C_RQ + LANES] = rope_r(piece).astype(BF16)
            elif col < C_RV:
                rk_ref[:, col - C_RK:col - C_RK + LANES] = (rope_r(piece) * k_scale).astype(BF16)
            elif col < C_RG:
                rv_ref[:, col - C_RV:col - C_RV + LANES] = piece.astype(BF16)
            elif col < C_NQ:
                rg_ref[:, col - C_RG:col - C_RG + LANES] = piece.astype(BF16)
            elif col < C_KV:
                val = rope_n(piece) * q_scale
                pair = (col - C_NQ) // LANES
                base = (pair // HEAD_PAIRS) * 2 * NSA_GQ_W + (pair % HEAD_PAIRS) * LANES
                zeros = jnp.zeros_like(val)
                nq_ref[:, base:base + LANES] = jnp.where(lane < NSA_DH, val, zeros).astype(BF16)
                nq_ref[:, base + NSA_GQ_W:base + NSA_GQ_W + LANES] = jnp.where(
                    lane < NSA_DH, pltpu.roll(val, NSA_DH, 1), zeros).astype(BF16)
            elif col < C_GATE:
                j_kv = (col - C_KV) // LANES
                is_key = j_kv % 2 == 0
                val = rope_n(piece) if is_key else piece
                if j_kv < 2:
                    cmp_ref[:, j_kv * LANES:(j_kv + 1) * LANES] = val.astype(BF16)
                    continue
                swapped = pltpu.roll(val, NSA_DH, 1)
                lower = lane < NSA_DH
                grouped = (val, swapped), (swapped, val)
                t0 = KV_TILE[j_kv]
                for g in range(NSA_GROUPS):
                    lo, hi = grouped[g]
                    if is_key:
                        fill = sel_tag if j_kv == 2 else jnp.zeros_like(val)
                        tiles = (jnp.where(lower, lo, fill),)
                    else:
                        tiles = (jnp.where(lower, 1.0, hi), jnp.where(lower, lo, 1.0))
                    for k, tile in enumerate(tiles):
                        kv_ref[g, :, (t0 + k) * LANES:(t0 + k + 1) * LANES] = tile.astype(BF16)
            else:
                gate_ref[:, col - C_GATE:col - C_GATE + LANES] = jax.nn.sigmoid(piece)


def _proj(seq_len, x2, posf, g_mix, inv_freq, w_all):
    T = x2.shape[0]
    tm = PROJ_TM
    assert seq_len % tm == 0
    row = lambda w: pl.BlockSpec((tm, w), lambda i: (i, 0))
    kv_w = KV_TILES * LANES
    out_shapes = [
        jax.ShapeDtypeStruct((T, RET_QK_W), BF16),
        jax.ShapeDtypeStruct((T, RET_QK_W), BF16),
        jax.ShapeDtypeStruct((T, RET_V_W), BF16),
        jax.ShapeDtypeStruct((T, RET_V_W), BF16),
        jax.ShapeDtypeStruct((T, 2 * NSA_Q_W), BF16),
        jax.ShapeDtypeStruct((T, 2 * LANES), BF16),
        jax.ShapeDtypeStruct((NSA_GROUPS, T, kv_w), BF16),
        jax.ShapeDtypeStruct((T, GATE_W), F32),
    ]
    return pl.pallas_call(
        functools.partial(_proj_kernel, seq_len),
        grid=(T // tm,),
        in_specs=[row(D_MODEL), row(1), _resident((1, D_MODEL)), _resident((1, LANES)),
                  _resident((D_MODEL, C_END))],
        out_specs=[row(RET_QK_W), row(RET_QK_W), row(RET_V_W), row(RET_V_W), row(2 * NSA_Q_W),
                   row(2 * LANES), pl.BlockSpec((NSA_GROUPS, tm, kv_w), lambda i: (0, i, 0)), row(GATE_W)],
        out_shape=out_shapes,
        compiler_params=pltpu.CompilerParams(dimension_semantics=("arbitrary",),
                                             vmem_limit_bytes=VMEM_LIMIT),
        name="proj",
    )(x2, posf, g_mix, inv_freq, w_all)


def _compress_kernel(k16_ref, v16_ref, w1k_ref, w2k_ref, pek_ref, w1v_ref, w2v_ref, pev_ref,
                     kc_ref, vc_ref):
    half = CMP_STRIDE * NSA_DH
    for x_ref, w1_ref, w2_ref, pe_ref, o_ref in ((k16_ref, w1k_ref, w2k_ref, pek_ref, kc_ref),
                                                 (v16_ref, w1v_ref, w2v_ref, pev_ref, vc_ref)):
        x = x_ref[0, 0]
        first = _dot(x, w1_ref[0:half, :])
        second = _dot(x, w1_ref[half:2 * half, :])
        pe_term = _dot(pe_ref[...], w1_ref[...])[0:1, :]
        hidden = first + pltpu.roll(second, second.shape[0] - 1, 0) + pe_term
        act = jax.nn.gelu(hidden).astype(BF16)
        o_ref[0, 0] = _dot(act, w2_ref[...]).astype(BF16)


def _compress(k16, v16, w1k, w2k, pek, w1v, w2v, pev):
    B, G, R, W = k16.shape
    blk = pl.BlockSpec((1, 1, R, W), lambda b, g: (b, g, 0, 0))
    oblk = pl.BlockSpec((1, 1, R, LANES), lambda b, g: (b, g, 0, 0))
    out = jax.ShapeDtypeStruct((B, G, R, LANES), BF16)
    wspecs = [_resident(w1k.shape), _resident(w2k.shape), _resident(pek.shape)]
    return pl.pallas_call(
        _compress_kernel,
        grid=(B, G),
        in_specs=[blk, blk] + wspecs + wspecs,
        out_specs=[oblk, oblk],
        out_shape=[out, out],
        compiler_params=pltpu.CompilerParams(dimension_semantics=("arbitrary", "arbitrary"),
                                             vmem_limit_bytes=VMEM_LIMIT),
        name="compress",
    )(k16, v16, w1k, w2k, pek, w1v, w2v, pev)


_RET_LOG_G = [math.log(1.0 - 2.0 ** (-5.0 - h)) for h in range(RET_HEADS)]


def _retention_kernel(q_ref, k_ref, v_ref, g_ref, gn_ref, o_ref, state_ref, decay_ref, xi_ref, zeta_ref):
    n = pl.program_id(1)
    C = RET_CHUNK

    @pl.when((pl.program_id(0) == 0) & (n == 0))
    def _tables():
        r = lax.broadcasted_iota(jnp.int32, (C, C), 0).astype(F32)
        c = lax.broadcasted_iota(jnp.int32, (C, C), 1).astype(F32)
        diff = r - c
        for h in range(RET_HEADS):
            lg = _RET_LOG_G[h]
            decay_ref[h] = jnp.where(diff >= 0, jnp.exp(jnp.maximum(diff, 0.0) * lg), 0.0)
            xi_ref[h] = jnp.exp((r + 1.0) * lg)
            zeta_ref[h] = jnp.exp((C - 1.0 - r) * lg)

    @pl.when(n == 0)
    def _reset():
        state_ref[...] = jnp.zeros_like(state_ref)

    for h in range(RET_HEADS):
        qs = slice(h * RET_DK, (h + 1) * RET_DK)
        vs = slice(h * RET_DV, (h + 1) * RET_DV)
        xi = xi_ref[h]
        xi = jnp.concatenate([xi, xi], axis=1)
        state = state_ref[h]
        for c in range(RET_STEP_CHUNKS):
            tok = slice(c * C, (c + 1) * C)
            qh = q_ref[0, tok, qs]
            kh = k_ref[0, tok, qs]
            vh = v_ref[0, tok, vs]
            inner = (_dot_tb(qh, kh) * decay_ref[h]).astype(BF16)
            o = _dot(inner, vh) + _dot(qh, state.astype(BF16)) * xi
            kz = (kh.astype(F32) * zeta_ref[h]).astype(BF16)
            state = math.exp(C * _RET_LOG_G[h]) * state + _dot_ta(kz, vh)
            mu = jnp.mean(o, axis=-1, keepdims=True)
            d = o - mu
            var = jnp.mean(d * d, axis=-1, keepdims=True)
            y = d * lax.rsqrt(var + RMS_EPS) * gn_ref[:, vs]
            g = g_ref[0, tok, vs].astype(F32)
            o_ref[0, tok, vs] = (y * (g * jax.nn.sigmoid(g))).astype(BF16)
        state_ref[h] = state


def _retention(rq, rk, rv, rg, gn_g):
    B, S, _ = rq.shape
    C = RET_CHUNK
    rows = RET_STEP_CHUNKS * C
    assert S % rows == 0
    blk = lambda w: pl.BlockSpec((1, rows, w), lambda b, n: (b, n, 0))
    return pl.pallas_call(
        _retention_kernel,
        grid=(B, S // rows),
        in_specs=[blk(RET_QK_W), blk(RET_QK_W), blk(RET_V_W), blk(RET_V_W), _resident((1, RET_V_W))],
        out_specs=blk(RET_V_W),
        out_shape=jax.ShapeDtypeStruct((B, S, RET_V_W), BF16),
        scratch_shapes=[pltpu.VMEM((RET_HEADS, RET_DK, RET_DV), F32),
                        pltpu.VMEM((RET_HEADS, C, C), F32),
                        pltpu.VMEM((RET_HEADS, C, C), F32),
                        pltpu.VMEM((RET_HEADS, C, C), F32)],
        compiler_params=pltpu.CompilerParams(dimension_semantics=("arbitrary", "arbitrary"),
                                             vmem_limit_bytes=VMEM_LIMIT),
        name="retention",
    )(rq, rk, rv, rg, gn_g)


NOTSEL_LANE0 = NSA_DH


def _nsa_kernel(q_ref, gate_ref, ksel_ref, vsel_e_ref, vsel_o_ref, kwin_ref, vwin_e_ref, vwin_o_ref,
                kc_ref, vc_ref, wbias_ref, cbias_ref, o_ref,
                acc_e, acc_o, m_e, m_o, out_acc, s_next, gate_tiles):
    TQ, KT, WK = NSA_TQ, NSA_KT, NSA_WK
    HP = HEAD_PAIRS
    n_blk = ksel_ref.shape[2] // SEL_BLOCK
    qt = pl.program_id(2)
    q0 = qt * TQ
    q_plain = tuple(jnp.concatenate([q_ref[0, :, (x * HP + hp) * LANES:(x * HP + hp + 1) * LANES]
                                     for hp in range(HP)], axis=0) for x in range(2))
    t_col = q0 + lax.broadcasted_iota(jnp.int32, (TQ, 1), 0)
    lane = lax.broadcasted_iota(jnp.int32, (TQ, LANES), 1)
    lower_half = lane < NSA_DH
    rows = [slice(hp * TQ, (hp + 1) * TQ) for hp in range(HP)]
    gates = gate_ref[0]

    for branch in range(3):
        for hp in range(HP):
            c = branch * NSA_HPG + hp * 2
            gate_tiles[branch * HP + hp] = jnp.where(lower_half, gates[:, c + 1:c + 2], gates[:, c:c + 1])

    def emit(branch, first, acc_pair, normalised):
        for hp in range(HP):
            a_e = acc_pair[0][rows[hp]]
            a_o = acc_pair[1][rows[hp]]
            weight = gate_tiles[branch * HP + hp]
            if not normalised:
                weight = weight / pltpu.roll(jnp.where(lower_half, a_e, a_o), NSA_DH, 1)
            contrib = jnp.where(lower_half, a_o, a_e) * weight
            if first:
                out_acc[rows[hp]] = contrib
            else:
                out_acc[rows[hp]] += contrib

    c_bias = jnp.where(lane * CMP_STRIDE + (CMP_LEN - 1) <= t_col, 0.0, MASK_BIAS)
    kc = kc_ref[0, 0]
    vc = vc_ref[0, 0]
    p_sum = jnp.zeros((TQ, LANES), F32)
    cmp_acc = []
    for qx in q_plain:
        s = _dot_tb(qx, kc)
        ps = []
        for hp in range(HP):
            sh = s[rows[hp]] + c_bias
            m = jnp.maximum(jnp.max(sh, axis=-1, keepdims=True), MAX_FLOOR)
            e = jnp.exp2(sh - m)
            l = jnp.sum(e, axis=-1, keepdims=True)
            p = e * jnp.where(l > 0, 1.0 / l, 0.0)
            p_sum = p_sum + p
            ps.append(p.astype(BF16))
        cmp_acc.append(_dot(jnp.concatenate(ps, axis=0), vc))
    emit(0, True, cmp_acc, True)

    w0 = pl.multiple_of(jnp.maximum(q0 - WINDOW, 0), TQ)
    w_bias = wbias_ref[jnp.minimum(qt, WINDOW // TQ)]
    kk = kwin_ref[0, 0, pl.ds(w0, WK), :]
    win_acc = []
    for qx, v_ref in zip(q_plain, (vwin_e_ref, vwin_o_ref)):
        s = _dot_tb(qx, kk)
        es = []
        for hp in range(HP):
            sh = s[rows[hp]] + w_bias
            es.append(jnp.exp2(sh - jnp.max(sh, axis=-1, keepdims=True)).astype(BF16))
        win_acc.append(_dot(jnp.concatenate(es, axis=0), v_ref[0, 0, pl.ds(w0, WK), :]))
    emit(2, False, win_acc, False)

    ni = lax.broadcasted_iota(jnp.int32, (LANES, LANES), 0)
    ci = lax.broadcasted_iota(jnp.int32, (LANES, LANES), 1)
    overlap_t = ((ci * CMP_STRIDE < ni * SEL_BLOCK + SEL_BLOCK)
                 & (ci * CMP_STRIDE + CMP_LEN - 1 >= ni * SEL_BLOCK) & (ni < n_blk))
    overlap_t = jnp.where(overlap_t, 1.0, 0.0).astype(BF16)
    p_hi = p_sum.astype(BF16)
    p_lo = (p_sum - p_hi.astype(F32)).astype(BF16)
    imp_t = (_dot_tb(overlap_t, p_hi) + _dot_tb(overlap_t, p_lo))[0:n_blk]
    blk_id = lax.broadcasted_iota(jnp.int32, (n_blk, TQ), 0)
    cur = (q0 + lax.broadcasted_iota(jnp.int32, (n_blk, TQ), 1)) >> 6
    forced = (blk_id == 0) | (blk_id == cur) | (blk_id == cur - 1)
    score = jnp.where(forced, FORCE_SCORE, jnp.where(blk_id <= cur, imp_t, -1.0))
    rank = jnp.zeros((n_blk, TQ), jnp.int32)
    for mblk in range(n_blk):
        other = score[mblk:mblk + 1, :]
        ahead = (other > score) | ((other == score) & (blk_id > mblk))
        rank = rank + jnp.where(ahead, 1, 0)
    not_sel_t = jnp.where(rank < SEL_TOPN, 0.0, 1.0)
    padded = jnp.concatenate([jnp.zeros((NOTSEL_LANE0, TQ), F32), not_sel_t,
                              jnp.zeros((LANES - NOTSEL_LANE0 - n_blk, TQ), F32)], axis=0)
    not_sel = jnp.concatenate([padded.T.astype(BF16)] * HP, axis=0)
    q_aug = tuple(qx + not_sel for qx in q_plain)

    for ref in (acc_e, acc_o):
        ref[...] = jnp.zeros_like(ref)
    for ref in (m_e, m_o):
        ref[...] = jnp.full_like(ref, MASK_BIAS)

    last_k0 = ksel_ref.shape[2] - KT

    def even_scores(kt):
        k0 = pl.multiple_of(jnp.minimum(kt * KT, last_k0), KT)
        return _dot_tb(q_aug[0], ksel_ref[0, 0, pl.ds(k0, KT), :])

    def softmax_pv(get_scores, bias, vv, acc, m_ref):
        es, alphas = [], []
        for hp in range(HP):
            sh = get_scores(hp) + bias
            tiles = [sh[:, j * LANES:(j + 1) * LANES] for j in range(KT // LANES)]
            m_old = m_ref[rows[hp]]
            m_new = jnp.maximum(m_old, jnp.max(functools.reduce(jnp.maximum, tiles), axis=-1, keepdims=True))
            m_ref[rows[hp]] = m_new
            alphas.append(jnp.exp2(m_old - m_new))
            es.append(jnp.concatenate([jnp.exp2(t - m_new).astype(BF16) for t in tiles], axis=1))
        pv = _dot(jnp.concatenate(es, axis=0), vv)
        for hp in range(HP):
            acc[rows[hp]] = alphas[hp] * acc[rows[hp]] + pv[rows[hp]]

    s_next[...] = even_scores(0)

    def sel_tile(kt, carry):
        k0 = pl.multiple_of(kt * KT, KT)
        bias = cbias_ref[jnp.minimum(qt - kt * (KT // TQ), KT // TQ)]
        s_odd = _dot_tb(q_aug[1], ksel_ref[0, 0, pl.ds(k0, KT), :])
        softmax_pv(lambda hp: s_next[rows[hp], :], bias, vsel_e_ref[0, 0, pl.ds(k0, KT), :], acc_e, m_e)
        s_next[...] = even_scores(kt + 1)
        softmax_pv(lambda hp: s_odd[rows[hp]], bias, vsel_o_ref[0, 0, pl.ds(k0, KT), :], acc_o, m_o)
        return carry

    lax.fori_loop(0, (q0 + TQ + KT - 1) // KT, sel_tile, 0)
    emit(1, False, (acc_e, acc_o), False)

    o_ref[0] = jnp.concatenate([out_acc[rows[hp]] for hp in range(HP)], axis=1).astype(BF16)


def _nsa(nq, gates, kv, kc, vc):
    B, S, _ = nq.shape
    G = NSA_GROUPS
    TQ = NSA_TQ
    assert S // SEL_BLOCK <= LANES - NOTSEL_LANE0 and S % NSA_KT == 0 and S >= NSA_WK
    assert NSA_KT % TQ == 0 and WINDOW % TQ == 0
    rows = HEAD_PAIRS * TQ
    i = jnp.arange(TQ, dtype=jnp.int32)[None, :, None]
    off = lambda n: jnp.arange(n + 1, dtype=jnp.int32)[:, None, None] * TQ
    j = jnp.arange(NSA_WK, dtype=jnp.int32)[None, None, :]
    t_rel = off(WINDOW // TQ) + i
    win_bias = jnp.where((j <= t_rel) & (j > t_rel - WINDOW), 0.0, MASK_BIAS).astype(F32)
    j = jnp.arange(NSA_KT, dtype=jnp.int32)[None, None, :]
    causal_bias = jnp.where(j <= off(NSA_KT // TQ) + i, 0.0, MASK_BIAS).astype(F32)
    qblk = pl.BlockSpec((1, TQ, 2 * NSA_GQ_W), lambda b, g, t: (b, t, g))
    oblk = pl.BlockSpec((1, TQ, NSA_GQ_W), lambda b, g, t: (b, t, g))
    gblk = pl.BlockSpec((1, TQ, LANES), lambda b, g, t: (b, t, g))
    kvblk = lambda j: pl.BlockSpec((1, 1, S, LANES), lambda b, g, t: (g, b, 0, j))
    cblk = pl.BlockSpec((1, 1, LANES, LANES), lambda b, g, t: (b, g, 0, 0))
    wide = pltpu.VMEM((rows, LANES), F32)
    return pl.pallas_call(
        _nsa_kernel,
        grid=(B, G, S // TQ),
        in_specs=([qblk, gblk] + [kvblk(j) for j in range(KV_TILES)] + [cblk, cblk]
                  + [_resident(win_bias.shape), _resident(causal_bias.shape)]),
        out_specs=oblk,
        out_shape=jax.ShapeDtypeStruct((B, S, NSA_Q_W), BF16),
        scratch_shapes=[wide] * 5 + [pltpu.VMEM((rows, NSA_KT), F32),
                                     pltpu.VMEM((3 * NSA_HPG // 2, TQ, LANES), F32)],
        compiler_params=pltpu.CompilerParams(dimension_semantics=("arbitrary",) * 3,
                                             vmem_limit_bytes=VMEM_LIMIT),
        name="nsa",
    )(nq, gates, *([kv] * KV_TILES), kc, vc, win_bias, causal_bias)


def _tail_kernel(final, x_ref, yr_ref, yn_ref, p_ref, gmix_ref, gmlp_ref, gple_ref, gfin_ref,
                 wmg_ref, wro_ref, wno_ref, wout_ref, wup_ref, wdn_ref, wpg_ref, wpp_ref, o_ref):
    x = x_ref[...]
    h = _rms(x, gmix_ref[...]).astype(BF16)
    o_ret = _dot(yr_ref[...], wro_ref[...])
    o_nsa = _dot(yn_ref[...], wno_ref[...])
    g_ret = jax.nn.sigmoid(_dot(h, wmg_ref[:, 0:D_MODEL]))
    g_nsa = jax.nn.sigmoid(_dot(h, wmg_ref[:, D_MODEL:2 * D_MODEL]))
    mix = (g_ret * o_ret + g_nsa * o_nsa).astype(BF16)
    x = x + _dot(mix, wout_ref[...])
    h2 = _rms(x, gmlp_ref[...]).astype(BF16)
    mlp = jnp.zeros_like(x)
    step = 1024
    for c0 in range(0, MLP_HIDDEN, step):
        up = jnp.maximum(_dot(h2, wup_ref[:, c0:c0 + step]), 0.0)
        mlp = mlp + _dot((up * up).astype(BF16), wdn_ref[c0:c0 + step, :])
    x = x + mlp
    h3 = _rms(x, gple_ref[...]).astype(BF16)
    ple_gate = jax.nn.sigmoid(_dot(h3, wpg_ref[...]))
    x = x + _dot(p_ref[...].astype(BF16), wpp_ref[...]) * ple_gate
    if final:
        x = _rms(x, gfin_ref[...])
    o_ref[...] = x


def _tail(final, x2, y_ret, y_nsa, p2, g_mix, g_mlp, g_ple, g_fin, w_mg, w_ro, w_no, w_out, w_up, w_dn,
          w_pg, w_pp):
    T = x2.shape[0]
    tm = TAIL_TM
    row = lambda w: pl.BlockSpec((tm, w), lambda i: (i, 0))
    gains = [_resident((1, D_MODEL))] * 4
    weights = [_resident(w.shape) for w in (w_mg, w_ro, w_no, w_out, w_up, w_dn, w_pg, w_pp)]
    return pl.pallas_call(
        functools.partial(_tail_kernel, final),
        grid=(T // tm,),
        in_specs=[row(D_MODEL), row(RET_V_W), row(NSA_Q_W), row(PLE_DIM)] + gains + weights,
        out_specs=row(D_MODEL),
        out_shape=jax.ShapeDtypeStruct((T, D_MODEL), F32),
        compiler_params=pltpu.CompilerParams(dimension_semantics=("arbitrary",),
                                             vmem_limit_bytes=VMEM_LIMIT),
        name="tail",
    )(x2, y_ret, y_nsa, p2, g_mix, g_mlp, g_ple, g_fin, w_mg, w_ro, w_no, w_out, w_up, w_dn, w_pg, w_pp)


def _pack_w_in(w):
    gate = w[:, C_GATE:C_GATE + 3 * NSA_HEADS]
    parts = [w[:, :C_GATE]]
    for g in range(NSA_GROUPS):
        cols = [j * NSA_HEADS + g * NSA_HPG + r for j in range(3) for r in range(NSA_HPG)]
        parts.append(jnp.pad(gate[:, jnp.array(cols)], ((0, 0), (0, LANES - len(cols)))))
    return jnp.concatenate(parts, axis=1).astype(BF16)


def kernel(x, p, positions, norm_mix_g, w_in, ret_gn_g, w_ret_o, cmp_pe_k, cmp_k_w1, cmp_k_w2, cmp_pe_v, cmp_v_w1, cmp_v_w2, w_nsa_o, w_merge_gate, w_out, norm_mlp_g, w_mlp_up, w_mlp_down, norm_ple_g, w_ple_gate, w_ple_proj, norm_final_g):
    B, S, D = x.shape
    depth = p.shape[0]
    T = B * S
    G = NSA_GROUPS
    bf = lambda a: a.astype(BF16)
    row = lambda a: a.reshape(1, -1)
    posf = positions.reshape(T, 1).astype(F32)
    inv_r = ROPE_THETA ** (-jnp.arange(0, RET_DK, 2, dtype=F32) / RET_DK)
    inv_n = ROPE_THETA ** (-jnp.arange(0, NSA_DH, 2, dtype=F32) / NSA_DH)
    inv_freq = jnp.concatenate([inv_r, inv_n, jnp.zeros((LANES - inv_r.shape[0] - inv_n.shape[0],), F32)])
    inv_freq = inv_freq.reshape(1, LANES)
    n_rows = S // CMP_STRIDE

    def strides(t):
        t = t.reshape(B, S, G, NSA_DH).transpose(0, 2, 1, 3)
        return t.reshape(B, G, n_rows, CMP_STRIDE * NSA_DH)

    def pe_rows(pe):
        return jnp.broadcast_to(bf(pe).reshape(1, -1), (8, CMP_LEN * NSA_DH))

    def swap_pairs(w):
        return w.reshape(NSA_HEADS // 2, 2, NSA_DH, -1)[:, ::-1].reshape(w.shape)

    dup_cols = lambda w: bf(jnp.concatenate([w, w], axis=1))
    pad_cols = lambda w: bf(jnp.concatenate([w, jnp.zeros_like(w)], axis=1))

    x2 = x.reshape(T, D)
    for i in range(depth):
        rq, rk, rv, rg, nq, cmp_kv, kv, gates = _proj(S, x2, posf, row(norm_mix_g[i]), inv_freq,
                                                       _pack_w_in(w_in[i]))
        kc, vc = _compress(strides(cmp_kv[:, :LANES]), strides(cmp_kv[:, LANES:]),
                           bf(cmp_k_w1[i]), pad_cols(cmp_k_w2[i]), pe_rows(cmp_pe_k[i]),
                           bf(cmp_v_w1[i]), dup_cols(cmp_v_w2[i]), pe_rows(cmp_pe_v[i]))
        sh3 = lambda a: a.reshape(B, S, a.shape[-1])
        y_ret = _retention(sh3(rq), sh3(rk), sh3(rv), sh3(rg), row(ret_gn_g[i]))
        y_nsa = _nsa(sh3(nq), sh3(gates), kv.reshape(G, B, S, kv.shape[-1]), kc, vc)
        x2 = _tail(i == depth - 1, x2, y_ret.reshape(T, RET_V_W), y_nsa.reshape(T, NSA_Q_W),
                   p[i].reshape(T, PLE_DIM), row(norm_mix_g[i]), row(norm_mlp_g[i]), row(norm_ple_g[i]),
                   row(norm_final_g), bf(w_merge_gate[i]), bf(w_ret_o[i]), bf(swap_pairs(w_nsa_o[i])), bf(w_out[i]),
                   bf(w_mlp_up[i]), bf(w_mlp_down[i]), bf(w_ple_gate[i]), bf(w_ple_proj[i]))
    return x2.reshape(B, S, D)
```

```python
import functools
import math

import jax
import jax.numpy as jnp
from jax import lax
from jax.experimental import pallas as pl
from jax.experimental.pallas import tpu as pltpu

F32 = jnp.float32
BF16 = jnp.bfloat16

D_MODEL = 1024
PLE_DIM = 256
RMS_EPS = 1e-6
ROPE_THETA = 10000.0
RET_HEADS = 8
RET_DK = 128
RET_DV = 256
RET_CHUNK = 128
RET_QK_W = RET_HEADS * RET_DK
RET_V_W = RET_HEADS * RET_DV
NSA_HEADS = 16
NSA_GROUPS = 2
NSA_HPG = 8
NSA_DH = 64
NSA_Q_W = NSA_HEADS * NSA_DH
NSA_GQ_W = NSA_HPG * NSA_DH
HEAD_PAIRS = NSA_HPG // 2
CMP_LEN = 32
CMP_STRIDE = 16
CMP_HIDDEN = 256
SEL_BLOCK = 64
SEL_TOPN = 8
WINDOW = 512
FORCE_SCORE = 1e6
MLP_HIDDEN = 4 * D_MODEL

LANES = 128
MASK_BIAS = -1e30
MAX_FLOOR = -1e29
LOG2E = math.log2(math.e)
VMEM_LIMIT = 56 * 1024 * 1024

PROJ_TM = 512
TAIL_TM = 512
NSA_TQ = 256
NSA_KT = 512
NSA_WK = WINDOW + NSA_TQ

C_RQ, C_RK, C_RV, C_RG = 0, 1024, 2048, 4096
C_NQ, C_KV, C_GATE, C_END = 6144, 7168, 7936, 8192
GATE_W = NSA_GROUPS * LANES
KV_TILE = {2: 0, 3: 1, 4: 3, 5: 4}
KV_TILES = 6


def _resident(shape):
    nd = len(shape)
    return pl.BlockSpec(shape, lambda *_: (0,) * nd, pipeline_mode=pl.Buffered(1))


def _rms(x, g):
    return x * lax.rsqrt(jnp.mean(x * x, axis=-1, keepdims=True) + RMS_EPS) * g


def _dot(a, b):
    return jnp.dot(a, b, preferred_element_type=F32)


def _dot_tb(a, b):
    return lax.dot_general(a, b, (((1,), (1,)), ((), ())), preferred_element_type=F32)


def _dot_ta(a, b):
    return lax.dot_general(a, b, (((0,), (0,)), ((), ())), preferred_element_type=F32)


def _proj_kernel(seq_len, x_ref, pos_ref, g_ref, inv_ref, gn_ref, w_ref,
                 yret_ref, nq_ref, cmp_ref, kv_ref, gate_ref,
                 rq_ref, rk_ref, rv_ref, rg_ref, state_ref, decay_ref, xi_ref, zeta_ref):
    tm = x_ref.shape[0]
    first_row = pl.program_id(0) * tm

    @pl.when(first_row == 0)
    def _tables():
        _retention_tables(decay_ref, xi_ref, zeta_ref)

    h = _rms(x_ref[...], g_ref[...]).astype(BF16)
    pos = pos_ref[...]
    lane = lax.broadcasted_iota(jnp.int32, (tm, LANES), 1)
    ang = pos * inv_ref[...]
    cos_a = jnp.cos(ang)
    sin_a = jnp.sin(ang)
    cos_r = jnp.where(lane < 64, cos_a, pltpu.roll(cos_a, 64, 1))
    sin_r = jnp.where(lane < 64, -sin_a, pltpu.roll(sin_a, 64, 1))

    def tile_nsa(t):
        return jnp.where(lane < 32, pltpu.roll(t, 64, 1),
                         jnp.where(lane < 64, pltpu.roll(t, 96, 1), jnp.where(lane < 96, t, pltpu.roll(t, 32, 1))))

    low = (lane & 32) == 0
    cos_n = tile_nsa(cos_a)
    sin_n = tile_nsa(sin_a)
    sin_n = jnp.where(low, -sin_n, sin_n)

    def rope_r(y):
        return y * cos_r + pltpu.roll(y, 64, 1) * sin_r

    def rope_n(y):
        partner = jnp.where(low, pltpu.roll(y, 96, 1), pltpu.roll(y, 32, 1))
        return y * cos_n + partner * sin_n

    seq_pos = first_row % seq_len + lax.broadcasted_iota(jnp.int32, (tm, LANES), 0)
    sel_tag = jnp.where(lane - NSA_DH == seq_pos // SEL_BLOCK, MASK_BIAS, 0.0)

    k_scale = RET_DK ** -0.5
    q_scale = NSA_DH ** -0.5 * LOG2E
    chunk = 512
    heads_per_chunk = chunk // RET_DV
    plan = []
    for v_chunk in range(RET_V_W // chunk):
        plan.append(C_RV + v_chunk * chunk)
        if (v_chunk * heads_per_chunk) % (chunk // RET_DK) == 0:
            qk_chunk = v_chunk * heads_per_chunk * RET_DK
            plan += [C_RQ + qk_chunk, C_RK + qk_chunk]
        plan.append(C_RG + v_chunk * chunk)
        plan.append(tuple(range(v_chunk * heads_per_chunk, (v_chunk + 1) * heads_per_chunk)))
    plan += list(range(C_NQ, C_END, chunk))
    for c0 in plan:
        if isinstance(c0, tuple):
            _retention_heads(c0, first_row % seq_len == 0, rq_ref, rk_ref, rv_ref, rg_ref, gn_ref, yret_ref,
                             state_ref, decay_ref, xi_ref, zeta_ref)
            continue
        y = _dot(h, w_ref[:, c0:c0 + chunk])
        for j in range(chunk // LANES):
            col = c0 + j * LANES
            piece = y[:, j * LANES:(j + 1) * LANES]
            if col < C_RK:
                rq_ref[:, col - C_RQ:col - C_RQ + LANES] = rope_r(piece).astype(BF16)
            elif col < C_RV:
                rk_ref[:, col - C_RK:col - C_RK + LANES] = (rope_r(piece) * k_scale).astype(BF16)
            elif col < C_RG:
                rv_ref[:, col - C_RV:col - C_RV + LANES] = piece.astype(BF16)
            elif col < C_NQ:
                rg_ref[:, col - C_RG:col - C_RG + LANES] = piece.astype(BF16)
            elif col < C_KV:
                val = rope_n(piece) * q_scale
                pair = (col - C_NQ) // LANES
                base = (pair // HEAD_PAIRS) * 2 * NSA_GQ_W + (pair % HEAD_PAIRS) * LANES
                zeros = jnp.zeros_like(val)
                nq_ref[:, base:base + LANES] = jnp.where(lane < NSA_DH, val, zeros).astype(BF16)
                nq_ref[:, base + NSA_GQ_W:base + NSA_GQ_W + LANES] = jnp.where(
                    lane < NSA_DH, pltpu.roll(val, NSA_DH, 1), zeros).astype(BF16)
            elif col < C_GATE:
                j_kv = (col - C_KV) // LANES
                is_key = j_kv % 2 == 0
                val = rope_n(piece) if is_key else piece
                if j_kv < 2:
                    cmp_ref[:, j_kv * LANES:(j_kv + 1) * LANES] = val.astype(BF16)
                    continue
                swapped = pltpu.roll(val, NSA_DH, 1)
                lower = lane < NSA_DH
                grouped = (val, swapped), (swapped, val)
                t0 = KV_TILE[j_kv]
                for g in range(NSA_GROUPS):
                    lo, hi = grouped[g]
                    if is_key:
                        fill = sel_tag if j_kv == 2 else jnp.zeros_like(val)
                        tiles = (jnp.where(lower, lo, fill),)
                    else:
                        tiles = (jnp.where(lower, 1.0, hi), jnp.where(lower, lo, 1.0))
                    for k, tile in enumerate(tiles):
                        kv_ref[g, :, (t0 + k) * LANES:(t0 + k + 1) * LANES] = tile.astype(BF16)
            else:
                gate_ref[:, col - C_GATE:col - C_GATE + LANES] = jax.nn.sigmoid(piece)


def _proj(seq_len, x2, posf, g_mix, inv_freq, gn_g, w_all):
    T = x2.shape[0]
    tm = PROJ_TM
    C = RET_CHUNK
    assert seq_len % tm == 0 and tm % C == 0
    row = lambda w: pl.BlockSpec((tm, w), lambda i: (i, 0))
    kv_w = KV_TILES * LANES
    out_shapes = [
        jax.ShapeDtypeStruct((T, RET_V_W), BF16),
        jax.ShapeDtypeStruct((T, 2 * NSA_Q_W), BF16),
        jax.ShapeDtypeStruct((T, 2 * LANES), BF16),
        jax.ShapeDtypeStruct((NSA_GROUPS, T, kv_w), BF16),
        jax.ShapeDtypeStruct((T, GATE_W), F32),
    ]
    table = pltpu.VMEM((RET_HEADS, C, C), F32)
    return pl.pallas_call(
        functools.partial(_proj_kernel, seq_len),
        grid=(T // tm,),
        in_specs=[row(D_MODEL), row(1), _resident((1, D_MODEL)), _resident((1, LANES)), _resident((1, RET_V_W)),
                  _resident((D_MODEL, C_END))],
        out_specs=[row(RET_V_W), row(2 * NSA_Q_W), row(2 * LANES),
                   pl.BlockSpec((NSA_GROUPS, tm, kv_w), lambda i: (0, i, 0)), row(GATE_W)],
        out_shape=out_shapes,
        scratch_shapes=[pltpu.VMEM((tm, RET_QK_W), BF16), pltpu.VMEM((tm, RET_QK_W), BF16),
                        pltpu.VMEM((tm, RET_V_W), BF16), pltpu.VMEM((tm, RET_V_W), BF16),
                        pltpu.VMEM((RET_HEADS, RET_DK, RET_DV), F32), table, table, table],
        compiler_params=pltpu.CompilerParams(dimension_semantics=("arbitrary",),
                                             vmem_limit_bytes=VMEM_LIMIT),
        name="proj",
    )(x2, posf, g_mix, inv_freq, gn_g, w_all)


def _compress_kernel(k16_ref, v16_ref, w1k_ref, w2k_ref, pek_ref, w1v_ref, w2v_ref, pev_ref,
                     kc_ref, vc_ref):
    half = CMP_STRIDE * NSA_DH
    for x_ref, w1_ref, w2_ref, pe_ref, o_ref in ((k16_ref, w1k_ref, w2k_ref, pek_ref, kc_ref),
                                                 (v16_ref, w1v_ref, w2v_ref, pev_ref, vc_ref)):
        x = x_ref[0, 0]
        first = _dot(x, w1_ref[0:half, :])
        second = _dot(x, w1_ref[half:2 * half, :])
        pe_term = _dot(pe_ref[...], w1_ref[...])[0:1, :]
        hidden = first + pltpu.roll(second, second.shape[0] - 1, 0) + pe_term
        act = jax.nn.gelu(hidden).astype(BF16)
        o_ref[0, 0] = _dot(act, w2_ref[...]).astype(BF16)


def _compress(k16, v16, w1k, w2k, pek, w1v, w2v, pev):
    B, G, R, W = k16.shape
    blk = pl.BlockSpec((1, 1, R, W), lambda b, g: (b, g, 0, 0))
    oblk = pl.BlockSpec((1, 1, R, LANES), lambda b, g: (b, g, 0, 0))
    out = jax.ShapeDtypeStruct((B, G, R, LANES), BF16)
    wspecs = [_resident(w1k.shape), _resident(w2k.shape), _resident(pek.shape)]
    return pl.pallas_call(
        _compress_kernel,
        grid=(B, G),
        in_specs=[blk, blk] + wspecs + wspecs,
        out_specs=[oblk, oblk],
        out_shape=[out, out],
        compiler_params=pltpu.CompilerParams(dimension_semantics=("arbitrary", "arbitrary"),
                                             vmem_limit_bytes=VMEM_LIMIT),
        name="compress",
    )(k16, v16, w1k, w2k, pek, w1v, w2v, pev)


_RET_LOG_G = [math.log(1.0 - 2.0 ** (-5.0 - h)) for h in range(RET_HEADS)]


def _retention_tables(decay_ref, xi_ref, zeta_ref):
    C = RET_CHUNK
    r = lax.broadcasted_iota(jnp.int32, (C, C), 0).astype(F32)
    c = lax.broadcasted_iota(jnp.int32, (C, C), 1).astype(F32)
    diff = r - c
    for h in range(RET_HEADS):
        lg = _RET_LOG_G[h]
        decay_ref[h] = jnp.where(diff >= 0, jnp.exp(jnp.maximum(diff, 0.0) * lg), 0.0)
        xi_ref[h] = jnp.exp((r + 1.0) * lg)
        zeta_ref[h] = jnp.exp((C - 1.0 - r) * lg)


def _retention_heads(heads, new_sequence, q_ref, k_ref, v_ref, g_ref, gn_ref, y_ref, state_ref, decay_ref, xi_ref,
                     zeta_ref):
    C = RET_CHUNK
    n_chunks = q_ref.shape[0] // C
    for h in heads:
        qs = slice(h * RET_DK, (h + 1) * RET_DK)
        vs = slice(h * RET_DV, (h + 1) * RET_DV)
        xi = xi_ref[h]
        xi = jnp.concatenate([xi, xi], axis=1)
        state = jnp.where(new_sequence, 0.0, state_ref[h])
        for c in range(n_chunks):
            tok = slice(c * C, (c + 1) * C)
            qh = q_ref[tok, qs]
            kh = k_ref[tok, qs]
            vh = v_ref[tok, vs]
            inner = (_dot_tb(qh, kh) * decay_ref[h]).astype(BF16)
            o = _dot(inner, vh) + _dot(qh, state.astype(BF16)) * xi
            kz = (kh.astype(F32) * zeta_ref[h]).astype(BF16)
            state = math.exp(C * _RET_LOG_G[h]) * state + _dot_ta(kz, vh)
            mu = jnp.mean(o, axis=-1, keepdims=True)
            d = o - mu
            var = jnp.mean(d * d, axis=-1, keepdims=True)
            y = d * lax.rsqrt(var + RMS_EPS) * gn_ref[:, vs]
            g = g_ref[tok, vs].astype(F32)
            y_ref[tok, vs] = (y * (g * jax.nn.sigmoid(g))).astype(BF16)
        state_ref[h] = state


NOTSEL_LANE0 = NSA_DH


def _nsa_kernel(q_ref, gate_ref, ksel_ref, vsel_e_ref, vsel_o_ref, kwin_ref, vwin_e_ref, vwin_o_ref,
                kc_ref, vc_ref, wbias_ref, cbias_ref, o_ref,
                acc_e, acc_o, m_e, m_o, out_acc, s_next, gate_tiles):
    TQ, KT, WK = NSA_TQ, NSA_KT, NSA_WK
    HP = HEAD_PAIRS
    n_blk = ksel_ref.shape[2] // SEL_BLOCK
    qt = pl.program_id(2)
    q0 = qt * TQ
    q_plain = tuple(jnp.concatenate([q_ref[0, :, (x * HP + hp) * LANES:(x * HP + hp + 1) * LANES]
                                     for hp in range(HP)], axis=0) for x in range(2))
    t_col = q0 + lax.broadcasted_iota(jnp.int32, (TQ, 1), 0)
    lane = lax.broadcasted_iota(jnp.int32, (TQ, LANES), 1)
    lower_half = lane < NSA_DH
    rows = [slice(hp * TQ, (hp + 1) * TQ) for hp in range(HP)]
    gates = gate_ref[0]

    for branch in range(3):
        for hp in range(HP):
            c = branch * NSA_HPG + hp * 2
            gate_tiles[branch * HP + hp] = jnp.where(lower_half, gates[:, c + 1:c + 2], gates[:, c:c + 1])

    def emit(branch, first, acc_pair, normalised):
        for hp in range(HP):
            a_e = acc_pair[0][rows[hp]]
            a_o = acc_pair[1][rows[hp]]
            weight = gate_tiles[branch * HP + hp]
            if not normalised:
                weight = weight / pltpu.roll(jnp.where(lower_half, a_e, a_o), NSA_DH, 1)
            contrib = jnp.where(lower_half, a_o, a_e) * weight
            if first:
                out_acc[rows[hp]] = contrib
            else:
                out_acc[rows[hp]] += contrib

    c_bias = jnp.where(lane * CMP_STRIDE + (CMP_LEN - 1) <= t_col, 0.0, MASK_BIAS)
    kc = kc_ref[0, 0]
    vc = vc_ref[0, 0]
    p_sum = jnp.zeros((TQ, LANES), F32)
    cmp_acc = []
    for qx in q_plain:
        s = _dot_tb(qx, kc)
        ps = []
        for hp in range(HP):
            sh = s[rows[hp]] + c_bias
            m = jnp.maximum(jnp.max(sh, axis=-1, keepdims=True), MAX_FLOOR)
            e = jnp.exp2(sh - m)
            l = jnp.sum(e, axis=-1, keepdims=True)
            p = e * jnp.where(l > 0, 1.0 / l, 0.0)
            p_sum = p_sum + p
            ps.append(p.astype(BF16))
        cmp_acc.append(_dot(jnp.concatenate(ps, axis=0), vc))
    emit(0, True, cmp_acc, True)

    w0 = pl.multiple_of(jnp.maximum(q0 - WINDOW, 0), TQ)
    w_bias = wbias_ref[jnp.minimum(qt, WINDOW // TQ)]
    kk = kwin_ref[0, 0, pl.ds(w0, WK), :]
    win_acc = []
    for qx, v_ref in zip(q_plain, (vwin_e_ref, vwin_o_ref)):
        s = _dot_tb(qx, kk)
        es = []
        for hp in range(HP):
            sh = s[rows[hp]] + w_bias
            es.append(jnp.exp2(sh - jnp.max(sh, axis=-1, keepdims=True)).astype(BF16))
        win_acc.append(_dot(jnp.concatenate(es, axis=0), v_ref[0, 0, pl.ds(w0, WK), :]))
    emit(2, False, win_acc, False)

    ni = lax.broadcasted_iota(jnp.int32, (LANES, LANES), 0)
    ci = lax.broadcasted_iota(jnp.int32, (LANES, LANES), 1)
    overlap_t = ((ci * CMP_STRIDE < ni * SEL_BLOCK + SEL_BLOCK)
                 & (ci * CMP_STRIDE + CMP_LEN - 1 >= ni * SEL_BLOCK) & (ni < n_blk))
    overlap_t = jnp.where(overlap_t, 1.0, 0.0).astype(BF16)
    p_hi = p_sum.astype(BF16)
    p_lo = (p_sum - p_hi.astype(F32)).astype(BF16)
    imp_t = (_dot_tb(overlap_t, p_hi) + _dot_tb(overlap_t, p_lo))[0:n_blk]
    blk_id = lax.broadcasted_iota(jnp.int32, (n_blk, TQ), 0)
    cur = (q0 + lax.broadcasted_iota(jnp.int32, (n_blk, TQ), 1)) >> 6
    forced = (blk_id == 0) | (blk_id == cur) | (blk_id == cur - 1)
    score = jnp.where(forced, FORCE_SCORE, jnp.where(blk_id <= cur, imp_t, -1.0))
    rank = jnp.zeros((n_blk, TQ), jnp.int32)
    for mblk in range(n_blk):
        other = score[mblk:mblk + 1, :]
        ahead = (other > score) | ((other == score) & (blk_id > mblk))
        rank = rank + jnp.where(ahead, 1, 0)
    not_sel_t = jnp.where(rank < SEL_TOPN, 0.0, 1.0)
    padded = jnp.concatenate([jnp.zeros((NOTSEL_LANE0, TQ), F32), not_sel_t,
                              jnp.zeros((LANES - NOTSEL_LANE0 - n_blk, TQ), F32)], axis=0)
    not_sel = jnp.concatenate([padded.T.astype(BF16)] * HP, axis=0)
    q_aug = tuple(qx + not_sel for qx in q_plain)

    for ref in (acc_e, acc_o):
        ref[...] = jnp.zeros_like(ref)
    for ref in (m_e, m_o):
        ref[...] = jnp.full_like(ref, MASK_BIAS)

    last_k0 = ksel_ref.shape[2] - KT

    def even_scores(kt):
        k0 = pl.multiple_of(jnp.minimum(kt * KT, last_k0), KT)
        return _dot_tb(q_aug[0], ksel_ref[0, 0, pl.ds(k0, KT), :])

    def softmax_pv(get_scores, bias, vv, acc, m_ref):
        es, alphas = [], []
        for hp in range(HP):
            sh = get_scores(hp) + bias
            tiles = [sh[:, j * LANES:(j + 1) * LANES] for j in range(KT // LANES)]
            m_old = m_ref[rows[hp]]
            m_new = jnp.maximum(m_old, jnp.max(functools.reduce(jnp.maximum, tiles), axis=-1, keepdims=True))
            m_ref[rows[hp]] = m_new
            alphas.append(jnp.exp2(m_old - m_new))
            es.append(jnp.concatenate([jnp.exp2(t - m_new).astype(BF16) for t in tiles], axis=1))
        pv = _dot(jnp.concatenate(es, axis=0), vv)
        for hp in range(HP):
            acc[rows[hp]] = alphas[hp] * acc[rows[hp]] + pv[rows[hp]]

    s_next[...] = even_scores(0)

    def sel_tile(kt, carry):
        k0 = pl.multiple_of(kt * KT, KT)
        bias = cbias_ref[jnp.minimum(qt - kt * (KT // TQ), KT // TQ)]
        s_odd = _dot_tb(q_aug[1], ksel_ref[0, 0, pl.ds(k0, KT), :])
        softmax_pv(lambda hp: s_next[rows[hp], :], bias, vsel_e_ref[0, 0, pl.ds(k0, KT), :], acc_e, m_e)
        s_next[...] = even_scores(kt + 1)
        softmax_pv(lambda hp: s_odd[rows[hp]], bias, vsel_o_ref[0, 0, pl.ds(k0, KT), :], acc_o, m_o)
        return carry

    lax.fori_loop(0, (q0 + TQ + KT - 1) // KT, sel_tile, 0)
    emit(1, False, (acc_e, acc_o), False)

    o_ref[0] = jnp.concatenate([out_acc[rows[hp]] for hp in range(HP)], axis=1).astype(BF16)


def _nsa(nq, gates, kv, kc, vc):
    B, S, _ = nq.shape
    G = NSA_GROUPS
    TQ = NSA_TQ
    assert S // SEL_BLOCK <= LANES - NOTSEL_LANE0 and S % NSA_KT == 0 and S >= NSA_WK
    assert NSA_KT % TQ == 0 and WINDOW % TQ == 0
    rows = HEAD_PAIRS * TQ
    i = jnp.arange(TQ, dtype=jnp.int32)[None, :, None]
    off = lambda n: jnp.arange(n + 1, dtype=jnp.int32)[:, None, None] * TQ
    j = jnp.arange(NSA_WK, dtype=jnp.int32)[None, None, :]
    t_rel = off(WINDOW // TQ) + i
    win_bias = jnp.where((j <= t_rel) & (j > t_rel - WINDOW), 0.0, MASK_BIAS).astype(F32)
    j = jnp.arange(NSA_KT, dtype=jnp.int32)[None, None, :]
    causal_bias = jnp.where(j <= off(NSA_KT // TQ) + i, 0.0, MASK_BIAS).astype(F32)
    qblk = pl.BlockSpec((1, TQ, 2 * NSA_GQ_W), lambda b, g, t: (b, t, g))
    oblk = pl.BlockSpec((1, TQ, NSA_GQ_W), lambda b, g, t: (b, t, g))
    gblk = pl.BlockSpec((1, TQ, LANES), lambda b, g, t: (b, t, g))
    kvblk = lambda j: pl.BlockSpec((1, 1, S, LANES), lambda b, g, t: (g, b, 0, j))
    cblk = pl.BlockSpec((1, 1, LANES, LANES), lambda b, g, t: (b, g, 0, 0))
    wide = pltpu.VMEM((rows, LANES), F32)
    return pl.pallas_call(
        _nsa_kernel,
        grid=(B, G, S // TQ),
        in_specs=([qblk, gblk] + [kvblk(j) for j in range(KV_TILES)] + [cblk, cblk]
                  + [_resident(win_bias.shape), _resident(causal_bias.shape)]),
        out_specs=oblk,
        out_shape=jax.ShapeDtypeStruct((B, S, NSA_Q_W), BF16),
        scratch_shapes=[wide] * 5 + [pltpu.VMEM((rows, NSA_KT), F32),
                                     pltpu.VMEM((3 * NSA_HPG // 2, TQ, LANES), F32)],
        compiler_params=pltpu.CompilerParams(dimension_semantics=("arbitrary",) * 3,
                                             vmem_limit_bytes=VMEM_LIMIT),
        name="nsa",
    )(nq, gates, *([kv] * KV_TILES), kc, vc, win_bias, causal_bias)


def _tail_kernel(final, x_ref, yr_ref, yn_ref, p_ref, gmix_ref, gmlp_ref, gple_ref, gfin_ref,
                 wmg_ref, wro_ref, wno_ref, wout_ref, wup_ref, wdn_ref, wpg_ref, wpp_ref, o_ref):
    x = x_ref[...]
    h = _rms(x, gmix_ref[...]).astype(BF16)
    o_ret = _dot(yr_ref[...], wro_ref[...])
    o_nsa = _dot(yn_ref[...], wno_ref[...])
    g_ret = jax.nn.sigmoid(_dot(h, wmg_ref[:, 0:D_MODEL]))
    g_nsa = jax.nn.sigmoid(_dot(h, wmg_ref[:, D_MODEL:2 * D_MODEL]))
    mix = (g_ret * o_ret + g_nsa * o_nsa).astype(BF16)
    x = x + _dot(mix, wout_ref[...])
    h2 = _rms(x, gmlp_ref[...]).astype(BF16)
    mlp = jnp.zeros_like(x)
    step = 1024
    for c0 in range(0, MLP_HIDDEN, step):
        up = jnp.maximum(_dot(h2, wup_ref[:, c0:c0 + step]), 0.0)
        mlp = mlp + _dot((up * up).astype(BF16), wdn_ref[c0:c0 + step, :])
    x = x + mlp
    h3 = _rms(x, gple_ref[...]).astype(BF16)
    ple_gate = jax.nn.sigmoid(_dot(h3, wpg_ref[...]))
    x = x + _dot(p_ref[...].astype(BF16), wpp_ref[...]) * ple_gate
    if final:
        x = _rms(x, gfin_ref[...])
    o_ref[...] = x


def _tail(final, x2, y_ret, y_nsa, p2, g_mix, g_mlp, g_ple, g_fin, w_mg, w_ro, w_no, w_out, w_up, w_dn,
          w_pg, w_pp):
    T = x2.shape[0]
    tm = TAIL_TM
    row = lambda w: pl.BlockSpec((tm, w), lambda i: (i, 0))
    gains = [_resident((1, D_MODEL))] * 4
    weights = [_resident(w.shape) for w in (w_mg, w_ro, w_no, w_out, w_up, w_dn, w_pg, w_pp)]
    return pl.pallas_call(
        functools.partial(_tail_kernel, final),
        grid=(T // tm,),
        in_specs=[row(D_MODEL), row(RET_V_W), row(NSA_Q_W), row(PLE_DIM)] + gains + weights,
        out_specs=row(D_MODEL),
        out_shape=jax.ShapeDtypeStruct((T, D_MODEL), F32),
        compiler_params=pltpu.CompilerParams(dimension_semantics=("arbitrary",),
                                             vmem_limit_bytes=VMEM_LIMIT),
        name="tail",
    )(x2, y_ret, y_nsa, p2, g_mix, g_mlp, g_ple, g_fin, w_mg, w_ro, w_no, w_out, w_up, w_dn, w_pg, w_pp)


def _pack_w_in(w):
    gate = w[:, C_GATE:C_GATE + 3 * NSA_HEADS]
    parts = [w[:, :C_GATE]]
    for g in range(NSA_GROUPS):
        cols = [j * NSA_HEADS + g * NSA_HPG + r for j in range(3) for r in range(NSA_HPG)]
        parts.append(jnp.pad(gate[:, jnp.array(cols)], ((0, 0), (0, LANES - len(cols)))))
    return jnp.concatenate(parts, axis=1).astype(BF16)


def kernel(x, p, positions, norm_mix_g, w_in, ret_gn_g, w_ret_o, cmp_pe_k, cmp_k_w1, cmp_k_w2, cmp_pe_v, cmp_v_w1, cmp_v_w2, w_nsa_o, w_merge_gate, w_out, norm_mlp_g, w_mlp_up, w_mlp_down, norm_ple_g, w_ple_gate, w_ple_proj, norm_final_g):
    B, S, D = x.shape
    depth = p.shape[0]
    T = B * S
    G = NSA_GROUPS
    bf = lambda a: a.astype(BF16)
    row = lambda a: a.reshape(1, -1)
    posf = positions.reshape(T, 1).astype(F32)
    inv_r = ROPE_THETA ** (-jnp.arange(0, RET_DK, 2, dtype=F32) / RET_DK)
    inv_n = ROPE_THETA ** (-jnp.arange(0, NSA_DH, 2, dtype=F32) / NSA_DH)
    inv_freq = jnp.concatenate([inv_r, inv_n, jnp.zeros((LANES - inv_r.shape[0] - inv_n.shape[0],), F32)])
    inv_freq = inv_freq.reshape(1, LANES)
    n_rows = S // CMP_STRIDE

    def strides(t):
        t = t.reshape(B, S, G, NSA_DH).transpose(0, 2, 1, 3)
        return t.reshape(B, G, n_rows, CMP_STRIDE * NSA_DH)

    def pe_rows(pe):
        return jnp.broadcast_to(bf(pe).reshape(1, -1), (8, CMP_LEN * NSA_DH))

    def swap_pairs(w):
        return w.reshape(NSA_HEADS // 2, 2, NSA_DH, -1)[:, ::-1].reshape(w.shape)

    dup_cols = lambda w: bf(jnp.concatenate([w, w], axis=1))
    pad_cols = lambda w: bf(jnp.concatenate([w, jnp.zeros_like(w)], axis=1))

    x2 = x.reshape(T, D)
    for i in range(depth):
        y_ret, nq, cmp_kv, kv, gates = _proj(S, x2, posf, row(norm_mix_g[i]), inv_freq, row(ret_gn_g[i]),
                                              _pack_w_in(w_in[i]))
        kc, vc = _compress(strides(cmp_kv[:, :LANES]), strides(cmp_kv[:, LANES:]),
                           bf(cmp_k_w1[i]), pad_cols(cmp_k_w2[i]), pe_rows(cmp_pe_k[i]),
                           bf(cmp_v_w1[i]), dup_cols(cmp_v_w2[i]), pe_rows(cmp_pe_v[i]))
        sh3 = lambda a: a.reshape(B, S, a.shape[-1])
        y_nsa = _nsa(sh3(nq), sh3(gates), kv.reshape(G, B, S, kv.shape[-1]), kc, vc)
        x2 = _tail(i == depth - 1, x2, y_ret, y_nsa.reshape(T, NSA_Q_W),
                   p[i].reshape(T, PLE_DIM), row(norm_mix_g[i]), row(norm_mlp_g[i]), row(norm_ple_g[i]),
                   row(norm_final_g), bf(w_merge_gate[i]), bf(w_ret_o[i]), bf(swap_pairs(w_nsa_o[i])), bf(w_out[i]),
                   bf(w_mlp_up[i]), bf(w_mlp_down[i]), bf(w_ple_gate[i]), bf(w_ple_proj[i]))
    return x2.reshape(B, S, D)
```

```python
import functools
import math

import jax
import jax.numpy as jnp
from jax import lax
from jax.experimental import pallas as pl
from jax.experimental.pallas import tpu as pltpu

F32 = jnp.float32
BF16 = jnp.bfloat16

D_MODEL = 1024
PLE_DIM = 256
RMS_EPS = 1e-6
ROPE_THETA = 10000.0
RET_HEADS = 8
RET_DK = 128
RET_DV = 256
RET_CHUNK = 128
RET_QK_W = RET_HEADS * RET_DK
RET_V_W = RET_HEADS * RET_DV
NSA_HEADS = 16
NSA_GROUPS = 2
NSA_HPG = 8
NSA_DH = 64
NSA_Q_W = NSA_HEADS * NSA_DH
NSA_GQ_W = NSA_HPG * NSA_DH
HEAD_PAIRS = NSA_HPG // 2
Q_STACKS = 3
CMP_LEN = 32
CMP_STRIDE = 16
CMP_HIDDEN = 256
SEL_BLOCK = 64
SEL_TOPN = 8
WINDOW = 512
FORCE_SCORE = 1e6
MLP_HIDDEN = 4 * D_MODEL

LANES = 128
MASK_BIAS = -1e30
MAX_FLOOR = -1e29
LOG2E = math.log2(math.e)
VMEM_LIMIT = 56 * 1024 * 1024

PROJ_TM = 512
TAIL_TM = 512
NSA_TQ = 256
NSA_KT = 512
NSA_WK = WINDOW + NSA_TQ

C_RQ, C_RK, C_RV, C_RG = 0, 1024, 2048, 4096
C_NQ, C_KV, C_GATE, C_END = 6144, 7168, 7936, 8192
GATE_W = NSA_GROUPS * LANES
KV_TILE = {2: 0, 3: 1, 4: 3, 5: 4}
KV_TILES = 6


def _resident(shape):
    nd = len(shape)
    return pl.BlockSpec(shape, lambda *_: (0,) * nd, pipeline_mode=pl.Buffered(1))


def _rms(x, g):
    return x * lax.rsqrt(jnp.mean(x * x, axis=-1, keepdims=True) + RMS_EPS) * g


def _dot(a, b):
    return jnp.dot(a, b, preferred_element_type=F32)


def _dot_tb(a, b):
    return lax.dot_general(a, b, (((1,), (1,)), ((), ())), preferred_element_type=F32)


def _dot_ta(a, b):
    return lax.dot_general(a, b, (((0,), (0,)), ((), ())), preferred_element_type=F32)


def _proj_kernel(seq_len, x_ref, pos_ref, g_ref, inv_ref, gn_ref, w_ref,
                 yret_ref, nq_ref, cmp_ref, kv_ref, gate_ref,
                 rq_ref, rk_ref, rv_ref, rg_ref, state_ref, decay_ref, xi_ref, zeta_ref):
    tm = x_ref.shape[0]
    first_row = pl.program_id(0) * tm

    @pl.when(first_row == 0)
    def _tables():
        _retention_tables(decay_ref, xi_ref, zeta_ref)

    h = _rms(x_ref[...], g_ref[...]).astype(BF16)
    pos = pos_ref[...]
    lane = lax.broadcasted_iota(jnp.int32, (tm, LANES), 1)
    ang = pos * inv_ref[...]
    cos_a = jnp.cos(ang)
    sin_a = jnp.sin(ang)
    cos_r = jnp.where(lane < 64, cos_a, pltpu.roll(cos_a, 64, 1))
    sin_r = jnp.where(lane < 64, -sin_a, pltpu.roll(sin_a, 64, 1))

    def tile_nsa(t):
        return jnp.where(lane < 32, pltpu.roll(t, 64, 1),
                         jnp.where(lane < 64, pltpu.roll(t, 96, 1), jnp.where(lane < 96, t, pltpu.roll(t, 32, 1))))

    low = (lane & 32) == 0
    cos_n = tile_nsa(cos_a)
    sin_n = tile_nsa(sin_a)
    sin_n = jnp.where(low, -sin_n, sin_n)

    def rope_r(y):
        return y * cos_r + pltpu.roll(y, 64, 1) * sin_r

    def rope_n(y):
        partner = jnp.where(low, pltpu.roll(y, 96, 1), pltpu.roll(y, 32, 1))
        return y * cos_n + partner * sin_n

    seq_pos = first_row % seq_len + lax.broadcasted_iota(jnp.int32, (tm, LANES), 0)
    sel_tag = jnp.where(lane - NSA_DH == seq_pos // SEL_BLOCK, MASK_BIAS, 0.0)

    k_scale = RET_DK ** -0.5
    q_scale = NSA_DH ** -0.5 * LOG2E
    chunk = 512
    heads_per_chunk = chunk // RET_DV
    plan = []
    for v_chunk in range(RET_V_W // chunk):
        plan.append(C_RV + v_chunk * chunk)
        if (v_chunk * heads_per_chunk) % (chunk // RET_DK) == 0:
            qk_chunk = v_chunk * heads_per_chunk * RET_DK
            plan += [C_RQ + qk_chunk, C_RK + qk_chunk]
        plan.append(C_RG + v_chunk * chunk)
        plan.append(tuple(range(v_chunk * heads_per_chunk, (v_chunk + 1) * heads_per_chunk)))
    plan += list(range(C_NQ, C_END, chunk))
    for c0 in plan:
        if isinstance(c0, tuple):
            _retention_heads(c0, first_row % seq_len == 0, rq_ref, rk_ref, rv_ref, rg_ref, gn_ref, yret_ref,
                             state_ref, decay_ref, xi_ref, zeta_ref)
            continue
        y = _dot(h, w_ref[:, c0:c0 + chunk])
        for j in range(chunk // LANES):
            col = c0 + j * LANES
            piece = y[:, j * LANES:(j + 1) * LANES]
            if col < C_RK:
                rq_ref[:, col - C_RQ:col - C_RQ + LANES] = rope_r(piece).astype(BF16)
            elif col < C_RV:
                rk_ref[:, col - C_RK:col - C_RK + LANES] = (rope_r(piece) * k_scale).astype(BF16)
            elif col < C_RG:
                rv_ref[:, col - C_RV:col - C_RV + LANES] = piece.astype(BF16)
            elif col < C_NQ:
                rg_ref[:, col - C_RG:col - C_RG + LANES] = piece.astype(BF16)
            elif col < C_KV:
                val = rope_n(piece) * q_scale
                pair = (col - C_NQ) // LANES
                base = (pair // HEAD_PAIRS) * Q_STACKS * NSA_GQ_W + (pair % HEAD_PAIRS) * LANES
                zeros = jnp.zeros_like(val)
                nq_ref[:, base:base + LANES] = jnp.where(lane < NSA_DH, val, zeros).astype(BF16)
                nq_ref[:, base + NSA_GQ_W:base + NSA_GQ_W + LANES] = jnp.where(
                    lane < NSA_DH, pltpu.roll(val, NSA_DH, 1), zeros).astype(BF16)
                nq_ref[:, base + 2 * NSA_GQ_W:base + 2 * NSA_GQ_W + LANES] = val.astype(BF16)
            elif col < C_GATE:
                j_kv = (col - C_KV) // LANES
                is_key = j_kv % 2 == 0
                val = rope_n(piece) if is_key else piece
                if j_kv < 2:
                    cmp_ref[:, j_kv * LANES:(j_kv + 1) * LANES] = val.astype(BF16)
                    continue
                swapped = pltpu.roll(val, NSA_DH, 1)
                lower = lane < NSA_DH
                grouped = (val, swapped), (swapped, val)
                t0 = KV_TILE[j_kv]
                for g in range(NSA_GROUPS):
                    lo, hi = grouped[g]
                    if is_key:
                        fill = sel_tag if j_kv == 2 else jnp.zeros_like(val)
                        tiles = (jnp.where(lower, lo, fill),)
                    else:
                        tiles = (jnp.where(lower, 1.0, hi), jnp.where(lower, lo, 1.0))
                    for k, tile in enumerate(tiles):
                        kv_ref[g, :, (t0 + k) * LANES:(t0 + k + 1) * LANES] = tile.astype(BF16)
            else:
                gate_ref[:, col - C_GATE:col - C_GATE + LANES] = jax.nn.sigmoid(piece)


def _proj(seq_len, x2, posf, g_mix, inv_freq, gn_g, w_all):
    T = x2.shape[0]
    tm = PROJ_TM
    C = RET_CHUNK
    assert seq_len % tm == 0 and tm % C == 0
    row = lambda w: pl.BlockSpec((tm, w), lambda i: (i, 0))
    kv_w = KV_TILES * LANES
    out_shapes = [
        jax.ShapeDtypeStruct((T, RET_V_W), BF16),
        jax.ShapeDtypeStruct((T, Q_STACKS * NSA_Q_W), BF16),
        jax.ShapeDtypeStruct((T, 2 * LANES), BF16),
        jax.ShapeDtypeStruct((NSA_GROUPS, T, kv_w), BF16),
        jax.ShapeDtypeStruct((T, GATE_W), F32),
    ]
    table = pltpu.VMEM((RET_HEADS, C, C), F32)
    return pl.pallas_call(
        functools.partial(_proj_kernel, seq_len),
        grid=(T // tm,),
        in_specs=[row(D_MODEL), row(1), _resident((1, D_MODEL)), _resident((1, LANES)), _resident((1, RET_V_W)),
                  _resident((D_MODEL, C_END))],
        out_specs=[row(RET_V_W), row(Q_STACKS * NSA_Q_W), row(2 * LANES),
                   pl.BlockSpec((NSA_GROUPS, tm, kv_w), lambda i: (0, i, 0)), row(GATE_W)],
        out_shape=out_shapes,
        scratch_shapes=[pltpu.VMEM((tm, RET_QK_W), BF16), pltpu.VMEM((tm, RET_QK_W), BF16),
                        pltpu.VMEM((tm, RET_V_W), BF16), pltpu.VMEM((tm, RET_V_W), BF16),
                        pltpu.VMEM((RET_HEADS, RET_DK, RET_DV), F32), table, table, table],
        compiler_params=pltpu.CompilerParams(dimension_semantics=("arbitrary",),
                                             vmem_limit_bytes=VMEM_LIMIT),
        name="proj",
    )(x2, posf, g_mix, inv_freq, gn_g, w_all)


def _compress_kernel(k16_ref, v16_ref, w1k_ref, w2k_ref, pek_ref, w1v_ref, w2v_ref, pev_ref,
                     kc_ref, vc_ref):
    half = CMP_STRIDE * NSA_DH
    for x_ref, w1_ref, w2_ref, pe_ref, o_ref in ((k16_ref, w1k_ref, w2k_ref, pek_ref, kc_ref),
                                                 (v16_ref, w1v_ref, w2v_ref, pev_ref, vc_ref)):
        x = x_ref[0, 0]
        first = _dot(x, w1_ref[0:half, :])
        second = _dot(x, w1_ref[half:2 * half, :])
        pe_term = _dot(pe_ref[...], w1_ref[...])[0:1, :]
        hidden = first + pltpu.roll(second, second.shape[0] - 1, 0) + pe_term
        act = jax.nn.gelu(hidden).astype(BF16)
        both = _dot(act, w2_ref[...]).astype(BF16)
        n_cmp = both.shape[0]
        o_ref[0, 0, 0:n_cmp, :] = both[:, 0:LANES]
        o_ref[0, 0, n_cmp:2 * n_cmp, :] = both[:, LANES:2 * LANES]


def _compress(k16, v16, w1k, w2k, pek, w1v, w2v, pev):
    B, G, R, W = k16.shape
    blk = pl.BlockSpec((1, 1, R, W), lambda b, g: (b, g, 0, 0))
    oblk = pl.BlockSpec((1, 1, 2 * R, LANES), lambda b, g: (b, g, 0, 0))
    out = jax.ShapeDtypeStruct((B, G, 2 * R, LANES), BF16)
    wspecs = [_resident(w1k.shape), _resident(w2k.shape), _resident(pek.shape)]
    return pl.pallas_call(
        _compress_kernel,
        grid=(B, G),
        in_specs=[blk, blk] + wspecs + wspecs,
        out_specs=[oblk, oblk],
        out_shape=[out, out],
        compiler_params=pltpu.CompilerParams(dimension_semantics=("arbitrary", "arbitrary"),
                                             vmem_limit_bytes=VMEM_LIMIT),
        name="compress",
    )(k16, v16, w1k, w2k, pek, w1v, w2v, pev)


_RET_LOG_G = [math.log(1.0 - 2.0 ** (-5.0 - h)) for h in range(RET_HEADS)]


def _retention_tables(decay_ref, xi_ref, zeta_ref):
    C = RET_CHUNK
    r = lax.broadcasted_iota(jnp.int32, (C, C), 0).astype(F32)
    c = lax.broadcasted_iota(jnp.int32, (C, C), 1).astype(F32)
    diff = r - c
    for h in range(RET_HEADS):
        lg = _RET_LOG_G[h]
        decay_ref[h] = jnp.where(diff >= 0, jnp.exp(jnp.maximum(diff, 0.0) * lg), 0.0)
        xi_ref[h] = jnp.exp((r + 1.0) * lg)
        zeta_ref[h] = jnp.exp((C - 1.0 - r) * lg)


def _retention_heads(heads, new_sequence, q_ref, k_ref, v_ref, g_ref, gn_ref, y_ref, state_ref, decay_ref, xi_ref,
                     zeta_ref):
    C = RET_CHUNK
    n_chunks = q_ref.shape[0] // C
    for h in heads:
        qs = slice(h * RET_DK, (h + 1) * RET_DK)
        vs = slice(h * RET_DV, (h + 1) * RET_DV)
        xi = xi_ref[h]
        xi = jnp.concatenate([xi, xi], axis=1)
        state = jnp.where(new_sequence, 0.0, state_ref[h])
        for c in range(n_chunks):
            tok = slice(c * C, (c + 1) * C)
            qh = q_ref[tok, qs]
            kh = k_ref[tok, qs]
            vh = v_ref[tok, vs]
            inner = (_dot_tb(qh, kh) * decay_ref[h]).astype(BF16)
            o = _dot(inner, vh) + _dot(qh, state.astype(BF16)) * xi
            kz = (kh.astype(F32) * zeta_ref[h]).astype(BF16)
            state = math.exp(C * _RET_LOG_G[h]) * state + _dot_ta(kz, vh)
            mu = jnp.mean(o, axis=-1, keepdims=True)
            d = o - mu
            var = jnp.mean(d * d, axis=-1, keepdims=True)
            y = d * lax.rsqrt(var + RMS_EPS) * gn_ref[:, vs]
            g = g_ref[tok, vs].astype(F32)
            y_ref[tok, vs] = (y * (g * jax.nn.sigmoid(g))).astype(BF16)
        state_ref[h] = state


NOTSEL_LANE0 = NSA_DH


def _nsa_kernel(q_ref, gate_ref, ksel_ref, vsel_e_ref, vsel_o_ref, kwin_ref, vwin_e_ref, vwin_o_ref,
                kc_ref, vc_ref, wbias_ref, cbias_ref, o_ref,
                acc_e, acc_o, m_e, m_o, out_acc, s_next, gate_tiles):
    TQ, KT, WK = NSA_TQ, NSA_KT, NSA_WK
    HP = HEAD_PAIRS
    n_blk = ksel_ref.shape[2] // SEL_BLOCK
    qt = pl.program_id(2)
    q0 = qt * TQ
    q_plain = tuple(jnp.concatenate([q_ref[0, :, (x * HP + hp) * LANES:(x * HP + hp + 1) * LANES]
                                     for hp in range(HP)], axis=0) for x in range(2))
    t_col = q0 + lax.broadcasted_iota(jnp.int32, (TQ, 1), 0)
    lane = lax.broadcasted_iota(jnp.int32, (TQ, LANES), 1)
    lower_half = lane < NSA_DH
    rows = [slice(hp * TQ, (hp + 1) * TQ) for hp in range(HP)]
    gates = gate_ref[0]

    for branch in range(3):
        for hp in range(HP):
            c = branch * NSA_HPG + hp * 2
            gate_tiles[branch * HP + hp] = jnp.where(lower_half, gates[:, c + 1:c + 2], gates[:, c:c + 1])

    def emit(branch, acc_pair):
        for hp in range(HP):
            a_e = acc_pair[0][rows[hp]]
            a_o = acc_pair[1][rows[hp]]
            weight = gate_tiles[branch * HP + hp] / pltpu.roll(jnp.where(lower_half, a_e, a_o), NSA_DH, 1)
            out_acc[rows[hp]] += jnp.where(lower_half, a_o, a_e) * weight

    c_bias = jnp.where(lane * CMP_STRIDE + (CMP_LEN - 1) <= t_col, 0.0, MASK_BIAS)
    q_pairs = jnp.concatenate([q_ref[0, :, (2 * HP + hp) * LANES:(2 * HP + hp + 1) * LANES]
                               for hp in range(HP)], axis=0)
    s = _dot_tb(q_pairs, kc_ref[0, 0])
    p_sum = jnp.zeros((TQ, LANES), F32)
    ps = []
    for hp in range(HP):
        halves = []
        for x in range(2):
            sh = s[rows[hp], x * LANES:(x + 1) * LANES] + c_bias
            m = jnp.maximum(jnp.max(sh, axis=-1, keepdims=True), MAX_FLOOR)
            e = jnp.exp2(sh - m)
            l = jnp.sum(e, axis=-1, keepdims=True)
            p = e * jnp.where(l > 0, 1.0 / l, 0.0)
            p_sum = p_sum + p
            halves.append(p.astype(BF16))
        ps.append(jnp.concatenate(halves, axis=1))
    cmp_out = _dot(jnp.concatenate(ps, axis=0), vc_ref[0, 0])
    for hp in range(HP):
        out_acc[rows[hp]] = cmp_out[rows[hp]] * gate_tiles[hp]

    w0 = pl.multiple_of(jnp.maximum(q0 - WINDOW, 0), TQ)
    w_bias = wbias_ref[jnp.minimum(qt, WINDOW // TQ)]
    kk = kwin_ref[0, 0, pl.ds(w0, WK), :]
    win_acc = []
    for qx, v_ref in zip(q_plain, (vwin_e_ref, vwin_o_ref)):
        s = _dot_tb(qx, kk)
        es = []
        for hp in range(HP):
            sh = s[rows[hp]] + w_bias
            es.append(jnp.exp2(sh - jnp.max(sh, axis=-1, keepdims=True)).astype(BF16))
        win_acc.append(_dot(jnp.concatenate(es, axis=0), v_ref[0, 0, pl.ds(w0, WK), :]))
    emit(2, win_acc)

    ni = lax.broadcasted_iota(jnp.int32, (LANES, LANES), 0)
    ci = lax.broadcasted_iota(jnp.int32, (LANES, LANES), 1)
    overlap_t = ((ci * CMP_STRIDE < ni * SEL_BLOCK + SEL_BLOCK)
                 & (ci * CMP_STRIDE + CMP_LEN - 1 >= ni * SEL_BLOCK) & (ni < n_blk))
    overlap_t = jnp.where(overlap_t, 1.0, 0.0).astype(BF16)
    p_hi = p_sum.astype(BF16)
    p_lo = (p_sum - p_hi.astype(F32)).astype(BF16)
    imp_t = (_dot_tb(overlap_t, p_hi) + _dot_tb(overlap_t, p_lo))[0:n_blk]
    blk_id = lax.broadcasted_iota(jnp.int32, (n_blk, TQ), 0)
    cur = (q0 + lax.broadcasted_iota(jnp.int32, (n_blk, TQ), 1)) >> 6
    forced = (blk_id == 0) | (blk_id == cur) | (blk_id == cur - 1)
    score = jnp.where(forced, FORCE_SCORE, jnp.where(blk_id <= cur, imp_t, -1.0))
    rank = jnp.zeros((n_blk, TQ), jnp.int32)
    for mblk in range(n_blk):
        other = score[mblk:mblk + 1, :]
        ahead = (other > score) | ((other == score) & (blk_id > mblk))
        rank = rank + jnp.where(ahead, 1, 0)
    not_sel_t = jnp.where(rank < SEL_TOPN, 0.0, 1.0)
    padded = jnp.concatenate([jnp.zeros((NOTSEL_LANE0, TQ), F32), not_sel_t,
                              jnp.zeros((LANES - NOTSEL_LANE0 - n_blk, TQ), F32)], axis=0)
    not_sel = jnp.concatenate([padded.T.astype(BF16)] * HP, axis=0)
    q_aug = tuple(qx + not_sel for qx in q_plain)

    for ref in (acc_e, acc_o):
        ref[...] = jnp.zeros_like(ref)
    for ref in (m_e, m_o):
        ref[...] = jnp.full_like(ref, MASK_BIAS)

    last_k0 = ksel_ref.shape[2] - KT

    def even_scores(kt):
        k0 = pl.multiple_of(jnp.minimum(kt * KT, last_k0), KT)
        return _dot_tb(q_aug[0], ksel_ref[0, 0, pl.ds(k0, KT), :])

    def softmax_pv(get_scores, bias, vv, acc, m_ref):
        es, alphas = [], []
        for hp in range(HP):
            sh = get_scores(hp) + bias
            tiles = [sh[:, j * LANES:(j + 1) * LANES] for j in range(KT // LANES)]
            m_old = m_ref[rows[hp]]
            m_new = jnp.maximum(m_old, jnp.max(functools.reduce(jnp.maximum, tiles), axis=-1, keepdims=True))
            m_ref[rows[hp]] = m_new
            alphas.append(jnp.exp2(m_old - m_new))
            es.append(jnp.concatenate([jnp.exp2(t - m_new).astype(BF16) for t in tiles], axis=1))
        pv = _dot(jnp.concatenate(es, axis=0), vv)
        for hp in range(HP):
            acc[rows[hp]] = alphas[hp] * acc[rows[hp]] + pv[rows[hp]]

    s_next[...] = even_scores(0)

    def sel_tile(kt, carry):
        k0 = pl.multiple_of(kt * KT, KT)
        bias = cbias_ref[jnp.minimum(qt - kt * (KT // TQ), KT // TQ)]
        s_odd = _dot_tb(q_aug[1], ksel_ref[0, 0, pl.ds(k0, KT), :])
        softmax_pv(lambda hp: s_next[rows[hp], :], bias, vsel_e_ref[0, 0, pl.ds(k0, KT), :], acc_e, m_e)
        s_next[...] = even_scores(kt + 1)
        softmax_pv(lambda hp: s_odd[rows[hp]], bias, vsel_o_ref[0, 0, pl.ds(k0, KT), :], acc_o, m_o)
        return carry

    lax.fori_loop(0, (q0 + TQ + KT - 1) // KT, sel_tile, 0)
    emit(1, (acc_e, acc_o))

    o_ref[0] = jnp.concatenate([out_acc[rows[hp]] for hp in range(HP)], axis=1).astype(BF16)


def _nsa(nq, gates, kv, kc, vc):
    B, S, _ = nq.shape
    G = NSA_GROUPS
    TQ = NSA_TQ
    assert S // SEL_BLOCK <= LANES - NOTSEL_LANE0 and S % NSA_KT == 0 and S >= NSA_WK
    assert NSA_KT % TQ == 0 and WINDOW % TQ == 0
    rows = HEAD_PAIRS * TQ
    i = jnp.arange(TQ, dtype=jnp.int32)[None, :, None]
    off = lambda n: jnp.arange(n + 1, dtype=jnp.int32)[:, None, None] * TQ
    j = jnp.arange(NSA_WK, dtype=jnp.int32)[None, None, :]
    t_rel = off(WINDOW // TQ) + i
    win_bias = jnp.where((j <= t_rel) & (j > t_rel - WINDOW), 0.0, MASK_BIAS).astype(F32)
    j = jnp.arange(NSA_KT, dtype=jnp.int32)[None, None, :]
    causal_bias = jnp.where(j <= off(NSA_KT // TQ) + i, 0.0, MASK_BIAS).astype(F32)
    qblk = pl.BlockSpec((1, TQ, Q_STACKS * NSA_GQ_W), lambda b, g, t: (b, t, g))
    oblk = pl.BlockSpec((1, TQ, NSA_GQ_W), lambda b, g, t: (b, t, g))
    gblk = pl.BlockSpec((1, TQ, LANES), lambda b, g, t: (b, t, g))
    kvblk = lambda j: pl.BlockSpec((1, 1, S, LANES), lambda b, g, t: (g, b, 0, j))
    cblk = pl.BlockSpec((1, 1, 2 * LANES, LANES), lambda b, g, t: (b, g, 0, 0))
    wide = pltpu.VMEM((rows, LANES), F32)
    return pl.pallas_call(
        _nsa_kernel,
        grid=(B, G, S // TQ),
        in_specs=([qblk, gblk] + [kvblk(j) for j in range(KV_TILES)] + [cblk, cblk]
                  + [_resident(win_bias.shape), _resident(causal_bias.shape)]),
        out_specs=oblk,
        out_shape=jax.ShapeDtypeStruct((B, S, NSA_Q_W), BF16),
        scratch_shapes=[wide] * 5 + [pltpu.VMEM((rows, NSA_KT), F32),
                                     pltpu.VMEM((3 * NSA_HPG // 2, TQ, LANES), F32)],
        compiler_params=pltpu.CompilerParams(dimension_semantics=("arbitrary",) * 3,
                                             vmem_limit_bytes=VMEM_LIMIT),
        name="nsa",
    )(nq, gates, *([kv] * KV_TILES), kc, vc, win_bias, causal_bias)


def _tail_kernel(final, x_ref, yr_ref, yn_ref, p_ref, gmix_ref, gmlp_ref, gple_ref, gfin_ref,
                 wmg_ref, wro_ref, wno_ref, wout_ref, wup_ref, wdn_ref, wpg_ref, wpp_ref, o_ref):
    x = x_ref[...]
    h = _rms(x, gmix_ref[...]).astype(BF16)
    o_ret = _dot(yr_ref[...], wro_ref[...])
    o_nsa = _dot(yn_ref[...], wno_ref[...])
    g_ret = jax.nn.sigmoid(_dot(h, wmg_ref[:, 0:D_MODEL]))
    g_nsa = jax.nn.sigmoid(_dot(h, wmg_ref[:, D_MODEL:2 * D_MODEL]))
    mix = (g_ret * o_ret + g_nsa * o_nsa).astype(BF16)
    x = x + _dot(mix, wout_ref[...])
    h2 = _rms(x, gmlp_ref[...]).astype(BF16)
    mlp = jnp.zeros_like(x)
    step = 1024
    for c0 in range(0, MLP_HIDDEN, step):
        up = jnp.maximum(_dot(h2, wup_ref[:, c0:c0 + step]), 0.0)
        mlp = mlp + _dot((up * up).astype(BF16), wdn_ref[c0:c0 + step, :])
    x = x + mlp
    h3 = _rms(x, gple_ref[...]).astype(BF16)
    ple_gate = jax.nn.sigmoid(_dot(h3, wpg_ref[...]))
    x = x + _dot(p_ref[...].astype(BF16), wpp_ref[...]) * ple_gate
    if final:
        x = _rms(x, gfin_ref[...])
    o_ref[...] = x


def _tail(final, x2, y_ret, y_nsa, p2, g_mix, g_mlp, g_ple, g_fin, w_mg, w_ro, w_no, w_out, w_up, w_dn,
          w_pg, w_pp):
    T = x2.shape[0]
    tm = TAIL_TM
    row = lambda w: pl.BlockSpec((tm, w), lambda i: (i, 0))
    gains = [_resident((1, D_MODEL))] * 4
    weights = [_resident(w.shape) for w in (w_mg, w_ro, w_no, w_out, w_up, w_dn, w_pg, w_pp)]
    return pl.pallas_call(
        functools.partial(_tail_kernel, final),
        grid=(T // tm,),
        in_specs=[row(D_MODEL), row(RET_V_W), row(NSA_Q_W), row(PLE_DIM)] + gains + weights,
        out_specs=row(D_MODEL),
        out_shape=jax.ShapeDtypeStruct((T, D_MODEL), F32),
        compiler_params=pltpu.CompilerParams(dimension_semantics=("arbitrary",),
                                             vmem_limit_bytes=VMEM_LIMIT),
        name="tail",
    )(x2, y_ret, y_nsa, p2, g_mix, g_mlp, g_ple, g_fin, w_mg, w_ro, w_no, w_out, w_up, w_dn, w_pg, w_pp)


def _pack_w_in(w):
    gate = w[:, C_GATE:C_GATE + 3 * NSA_HEADS]
    parts = [w[:, :C_GATE]]
    for g in range(NSA_GROUPS):
        cols = [j * NSA_HEADS + g * NSA_HPG + r for j in range(3) for r in range(NSA_HPG)]
        parts.append(jnp.pad(gate[:, jnp.array(cols)], ((0, 0), (0, LANES - len(cols)))))
    return jnp.concatenate(parts, axis=1).astype(BF16)


def kernel(x, p, positions, norm_mix_g, w_in, ret_gn_g, w_ret_o, cmp_pe_k, cmp_k_w1, cmp_k_w2, cmp_pe_v, cmp_v_w1, cmp_v_w2, w_nsa_o, w_merge_gate, w_out, norm_mlp_g, w_mlp_up, w_mlp_down, norm_ple_g, w_ple_gate, w_ple_proj, norm_final_g):
    B, S, D = x.shape
    depth = p.shape[0]
    T = B * S
    G = NSA_GROUPS
    bf = lambda a: a.astype(BF16)
    row = lambda a: a.reshape(1, -1)
    posf = positions.reshape(T, 1).astype(F32)
    inv_r = ROPE_THETA ** (-jnp.arange(0, RET_DK, 2, dtype=F32) / RET_DK)
    inv_n = ROPE_THETA ** (-jnp.arange(0, NSA_DH, 2, dtype=F32) / NSA_DH)
    inv_freq = jnp.concatenate([inv_r, inv_n, jnp.zeros((LANES - inv_r.shape[0] - inv_n.shape[0],), F32)])
    inv_freq = inv_freq.reshape(1, LANES)
    n_rows = S // CMP_STRIDE

    def strides(t):
        t = t.reshape(B, S, G, NSA_DH).transpose(0, 2, 1, 3)
        return t.reshape(B, G, n_rows, CMP_STRIDE * NSA_DH)

    def pe_rows(pe):
        return jnp.broadcast_to(bf(pe).reshape(1, -1), (8, CMP_LEN * NSA_DH))

    def swap_pairs(w):
        return w.reshape(NSA_HEADS // 2, 2, NSA_DH, -1)[:, ::-1].reshape(w.shape)

    def widen(w, outer):
        z = jnp.zeros_like(w)
        return bf(jnp.concatenate([w, z, z, w] if outer else [z, w, w, z], axis=1))

    x2 = x.reshape(T, D)
    for i in range(depth):
        y_ret, nq, cmp_kv, kv, gates = _proj(S, x2, posf, row(norm_mix_g[i]), inv_freq, row(ret_gn_g[i]),
                                              _pack_w_in(w_in[i]))
        kc, vc = _compress(strides(cmp_kv[:, :LANES]), strides(cmp_kv[:, LANES:]),
                           bf(cmp_k_w1[i]), widen(cmp_k_w2[i], True), pe_rows(cmp_pe_k[i]),
                           bf(cmp_v_w1[i]), widen(cmp_v_w2[i], False), pe_rows(cmp_pe_v[i]))
        sh3 = lambda a: a.reshape(B, S, a.shape[-1])
        y_nsa = _nsa(sh3(nq), sh3(gates), kv.reshape(G, B, S, kv.shape[-1]), kc, vc)
        x2 = _tail(i == depth - 1, x2, y_ret, y_nsa.reshape(T, NSA_Q_W),
                   p[i].reshape(T, PLE_DIM), row(norm_mix_g[i]), row(norm_mlp_g[i]), row(norm_ple_g[i]),
                   row(norm_final_g), bf(w_merge_gate[i]), bf(w_ret_o[i]), bf(swap_pairs(w_nsa_o[i])), bf(w_out[i]),
                   bf(w_mlp_up[i]), bf(w_mlp_down[i]), bf(w_ple_gate[i]), bf(w_ple_proj[i]))
    return x2.reshape(B, S, D)
```

```python
import functools
import math

import jax
import jax.numpy as jnp
from jax import lax
from jax.experimental import pallas as pl
from jax.experimental.pallas import tpu as pltpu

F32 = jnp.float32
BF16 = jnp.bfloat16

D_MODEL = 1024
PLE_DIM = 256
RMS_EPS = 1e-6
ROPE_THETA = 10000.0
RET_HEADS = 8
RET_DK = 128
RET_DV = 256
RET_CHUNK = 128
RET_QK_W = RET_HEADS * RET_DK
RET_V_W = RET_HEADS * RET_DV
NSA_HEADS = 16
NSA_GROUPS = 2
NSA_HPG = 8
NSA_DH = 64
NSA_Q_W = NSA_HEADS * NSA_DH
NSA_GQ_W = NSA_HPG * NSA_DH
HEAD_PAIRS = NSA_HPG // 2
Q_STACKS = 3
CMP_LEN = 32
CMP_STRIDE = 16
CMP_HIDDEN = 256
SEL_BLOCK = 64
SEL_TOPN = 8
WINDOW = 512
FORCE_SCORE = 1e6
MLP_HIDDEN = 4 * D_MODEL

LANES = 128
MASK_BIAS = -1e30
MAX_FLOOR = -1e29
LOG2E = math.log2(math.e)
VMEM_LIMIT = 56 * 1024 * 1024

PROJ_TM = 512
TAIL_TM = 512
NSA_TQ = 256
NSA_KT = 512
NSA_WK = WINDOW + NSA_TQ

C_RQ, C_RK, C_RV, C_RG = 0, 1024, 2048, 4096
C_NQ, C_KV, C_GATE, C_END = 6144, 7168, 7936, 8192
GATE_W = NSA_GROUPS * LANES
KV_TILE = {2: 0, 3: 1, 4: 3, 5: 4}
KV_TILES = 6


def _resident(shape):
    nd = len(shape)
    return pl.BlockSpec(shape, lambda *_: (0,) * nd, pipeline_mode=pl.Buffered(1))


def _rms(x, g):
    return x * lax.rsqrt(jnp.mean(x * x, axis=-1, keepdims=True) + RMS_EPS) * g


def _dot(a, b):
    return jnp.dot(a, b, preferred_element_type=F32)


def _dot_tb(a, b):
    return lax.dot_general(a, b, (((1,), (1,)), ((), ())), preferred_element_type=F32)


def _dot_ta(a, b):
    return lax.dot_general(a, b, (((0,), (0,)), ((), ())), preferred_element_type=F32)


def _proj_kernel(seq_len, x_ref, pos_ref, g_ref, inv_ref, gn_ref, w_ref,
                 yret_ref, nq_ref, cmp_ref, kv_ref, gate_ref,
                 rq_ref, rk_ref, rv_ref, rg_ref, state_ref, decay_ref, xi_ref, zeta_ref):
    tm = x_ref.shape[0]
    first_row = pl.program_id(0) * tm

    @pl.when(first_row == 0)
    def _tables():
        _retention_tables(decay_ref, xi_ref, zeta_ref)

    h = _rms(x_ref[...], g_ref[...]).astype(BF16)
    pos = pos_ref[...]
    lane = lax.broadcasted_iota(jnp.int32, (tm, LANES), 1)
    ang = pos * inv_ref[...]
    cos_a = jnp.cos(ang)
    sin_a = jnp.sin(ang)
    cos_r = jnp.where(lane < 64, cos_a, pltpu.roll(cos_a, 64, 1))
    sin_r = jnp.where(lane < 64, -sin_a, pltpu.roll(sin_a, 64, 1))

    def tile_nsa(t):
        return jnp.where(lane < 32, pltpu.roll(t, 64, 1),
                         jnp.where(lane < 64, pltpu.roll(t, 96, 1), jnp.where(lane < 96, t, pltpu.roll(t, 32, 1))))

    low = (lane & 32) == 0
    cos_n = tile_nsa(cos_a)
    sin_n = tile_nsa(sin_a)
    sin_n = jnp.where(low, -sin_n, sin_n)

    def rope_r(y):
        return y * cos_r + pltpu.roll(y, 64, 1) * sin_r

    def rope_n(y):
        partner = jnp.where(low, pltpu.roll(y, 96, 1), pltpu.roll(y, 32, 1))
        return y * cos_n + partner * sin_n

    seq_pos = first_row % seq_len + lax.broadcasted_iota(jnp.int32, (tm, LANES), 0)
    sel_tag = jnp.where(lane - NSA_DH == seq_pos // SEL_BLOCK, MASK_BIAS, 0.0)

    k_scale = RET_DK ** -0.5
    q_scale = NSA_DH ** -0.5 * LOG2E
    chunk = 512
    heads_per_chunk = chunk // RET_DV
    plan = []
    for v_chunk in range(RET_V_W // chunk):
        plan.append(C_RV + v_chunk * chunk)
        if (v_chunk * heads_per_chunk) % (chunk // RET_DK) == 0:
            qk_chunk = v_chunk * heads_per_chunk * RET_DK
            plan += [C_RQ + qk_chunk, C_RK + qk_chunk]
        plan.append(C_RG + v_chunk * chunk)
        plan.append(tuple(range(v_chunk * heads_per_chunk, (v_chunk + 1) * heads_per_chunk)))
    plan += list(range(C_NQ, C_END, chunk))
    for c0 in plan:
        if isinstance(c0, tuple):
            _retention_heads(c0, first_row % seq_len == 0, rq_ref, rk_ref, rv_ref, rg_ref, gn_ref, yret_ref,
                             state_ref, decay_ref, xi_ref, zeta_ref)
            continue
        y = _dot(h, w_ref[:, c0:c0 + chunk])
        for j in range(chunk // LANES):
            col = c0 + j * LANES
            piece = y[:, j * LANES:(j + 1) * LANES]
            if col < C_RK:
                rq_ref[:, col - C_RQ:col - C_RQ + LANES] = rope_r(piece).astype(BF16)
            elif col < C_RV:
                rk_ref[:, col - C_RK:col - C_RK + LANES] = (rope_r(piece) * k_scale).astype(BF16)
            elif col < C_RG:
                rv_ref[:, col - C_RV:col - C_RV + LANES] = piece.astype(BF16)
            elif col < C_NQ:
                rg_ref[:, col - C_RG:col - C_RG + LANES] = piece.astype(BF16)
            elif col < C_KV:
                val = rope_n(piece) * q_scale
                pair = (col - C_NQ) // LANES
                base = (pair // HEAD_PAIRS) * Q_STACKS * NSA_GQ_W + (pair % HEAD_PAIRS) * LANES
                zeros = jnp.zeros_like(val)
                nq_ref[:, base:base + LANES] = jnp.where(lane < NSA_DH, val, zeros).astype(BF16)
                nq_ref[:, base + NSA_GQ_W:base + NSA_GQ_W + LANES] = jnp.where(
                    lane < NSA_DH, pltpu.roll(val, NSA_DH, 1), zeros).astype(BF16)
                nq_ref[:, base + 2 * NSA_GQ_W:base + 2 * NSA_GQ_W + LANES] = val.astype(BF16)
            elif col < C_GATE:
                j_kv = (col - C_KV) // LANES
                is_key = j_kv % 2 == 0
                val = rope_n(piece) if is_key else piece
                if j_kv < 2:
                    cmp_ref[:, j_kv * LANES:(j_kv + 1) * LANES] = val.astype(BF16)
                    continue
                swapped = pltpu.roll(val, NSA_DH, 1)
                lower = lane < NSA_DH
                grouped = (val, swapped), (swapped, val)
                t0 = KV_TILE[j_kv]
                for g in range(NSA_GROUPS):
                    lo, hi = grouped[g]
                    if is_key:
                        fill = sel_tag if j_kv == 2 else jnp.zeros_like(val)
                        tiles = (jnp.where(lower, lo, fill),)
                    else:
                        tiles = (jnp.where(lower, 1.0, hi), jnp.where(lower, lo, 1.0))
                    for k, tile in enumerate(tiles):
                        kv_ref[g, :, (t0 + k) * LANES:(t0 + k + 1) * LANES] = tile.astype(BF16)
            else:
                gate_ref[:, col - C_GATE:col - C_GATE + LANES] = jax.nn.sigmoid(piece)


def _proj(seq_len, x2, posf, g_mix, inv_freq, gn_g, w_all):
    T = x2.shape[0]
    tm = PROJ_TM
    C = RET_CHUNK
    assert seq_len % tm == 0 and tm % C == 0
    row = lambda w: pl.BlockSpec((tm, w), lambda i: (i, 0))
    kv_w = KV_TILES * LANES
    out_shapes = [
        jax.ShapeDtypeStruct((T, RET_V_W), BF16),
        jax.ShapeDtypeStruct((T, Q_STACKS * NSA_Q_W), BF16),
        jax.ShapeDtypeStruct((T, 2 * LANES), BF16),
        jax.ShapeDtypeStruct((NSA_GROUPS, T, kv_w), BF16),
        jax.ShapeDtypeStruct((T, GATE_W), F32),
    ]
    table = pltpu.VMEM((RET_HEADS, C, C), F32)
    return pl.pallas_call(
        functools.partial(_proj_kernel, seq_len),
        grid=(T // tm,),
        in_specs=[row(D_MODEL), row(1), _resident((1, D_MODEL)), _resident((1, LANES)), _resident((1, RET_V_W)),
                  _resident((D_MODEL, C_END))],
        out_specs=[row(RET_V_W), row(Q_STACKS * NSA_Q_W), row(2 * LANES),
                   pl.BlockSpec((NSA_GROUPS, tm, kv_w), lambda i: (0, i, 0)), row(GATE_W)],
        out_shape=out_shapes,
        scratch_shapes=[pltpu.VMEM((tm, RET_QK_W), BF16), pltpu.VMEM((tm, RET_QK_W), BF16),
                        pltpu.VMEM((tm, RET_V_W), BF16), pltpu.VMEM((tm, RET_V_W), BF16),
                        pltpu.VMEM((RET_HEADS, RET_DK, RET_DV), F32), table, table, table],
        compiler_params=pltpu.CompilerParams(dimension_semantics=("arbitrary",),
                                             vmem_limit_bytes=VMEM_LIMIT),
        name="proj",
    )(x2, posf, g_mix, inv_freq, gn_g, w_all)


def _compress_kernel(k16_ref, v16_ref, w1k_ref, w2k_ref, pek_ref, w1v_ref, w2v_ref, pev_ref,
                     kc_ref, vc_ref):
    half = CMP_STRIDE * NSA_DH
    for x_ref, w1_ref, w2_ref, pe_ref, o_ref in ((k16_ref, w1k_ref, w2k_ref, pek_ref, kc_ref),
                                                 (v16_ref, w1v_ref, w2v_ref, pev_ref, vc_ref)):
        x = x_ref[0, 0]
        first = _dot(x, w1_ref[0:half, :])
        second = _dot(x, w1_ref[half:2 * half, :])
        pe_term = _dot(pe_ref[...], w1_ref[...])[0:1, :]
        hidden = first + pltpu.roll(second, second.shape[0] - 1, 0) + pe_term
        act = jax.nn.gelu(hidden).astype(BF16)
        both = _dot(act, w2_ref[...]).astype(BF16)
        n_cmp = both.shape[0]
        o_ref[0, 0, 0:n_cmp, :] = both[:, 0:LANES]
        o_ref[0, 0, n_cmp:2 * n_cmp, :] = both[:, LANES:2 * LANES]


def _compress(k16, v16, w1k, w2k, pek, w1v, w2v, pev):
    B, G, R, W = k16.shape
    blk = pl.BlockSpec((1, 1, R, W), lambda b, g: (b, g, 0, 0))
    oblk = pl.BlockSpec((1, 1, 2 * R, LANES), lambda b, g: (b, g, 0, 0))
    out = jax.ShapeDtypeStruct((B, G, 2 * R, LANES), BF16)
    wspecs = [_resident(w1k.shape), _resident(w2k.shape), _resident(pek.shape)]
    return pl.pallas_call(
        _compress_kernel,
        grid=(B, G),
        in_specs=[blk, blk] + wspecs + wspecs,
        out_specs=[oblk, oblk],
        out_shape=[out, out],
        compiler_params=pltpu.CompilerParams(dimension_semantics=("arbitrary", "arbitrary"),
                                             vmem_limit_bytes=VMEM_LIMIT),
        name="compress",
    )(k16, v16, w1k, w2k, pek, w1v, w2v, pev)


_RET_LOG_G = [math.log(1.0 - 2.0 ** (-5.0 - h)) for h in range(RET_HEADS)]


def _retention_tables(decay_ref, xi_ref, zeta_ref):
    C = RET_CHUNK
    r = lax.broadcasted_iota(jnp.int32, (C, C), 0).astype(F32)
    c = lax.broadcasted_iota(jnp.int32, (C, C), 1).astype(F32)
    diff = r - c
    for h in range(RET_HEADS):
        lg = _RET_LOG_G[h]
        decay_ref[h] = jnp.where(diff >= 0, jnp.exp(jnp.maximum(diff, 0.0) * lg), 0.0)
        xi_ref[h] = jnp.exp((r + 1.0) * lg)
        zeta_ref[h] = jnp.exp((C - 1.0 - r) * lg)


def _retention_heads(heads, new_sequence, q_ref, k_ref, v_ref, g_ref, gn_ref, y_ref, state_ref, decay_ref, xi_ref,
                     zeta_ref):
    C = RET_CHUNK
    n_chunks = q_ref.shape[0] // C
    for h in heads:
        qs = slice(h * RET_DK, (h + 1) * RET_DK)
        vs = slice(h * RET_DV, (h + 1) * RET_DV)
        xi = xi_ref[h]
        xi = jnp.concatenate([xi, xi], axis=1)
        state = jnp.where(new_sequence, 0.0, state_ref[h])
        for c in range(n_chunks):
            tok = slice(c * C, (c + 1) * C)
            qh = q_ref[tok, qs]
            kh = k_ref[tok, qs]
            vh = v_ref[tok, vs]
            inner = (_dot_tb(qh, kh) * decay_ref[h]).astype(BF16)
            o = _dot(inner, vh) + _dot(qh, state.astype(BF16)) * xi
            kz = (kh.astype(F32) * zeta_ref[h]).astype(BF16)
            state = math.exp(C * _RET_LOG_G[h]) * state + _dot_ta(kz, vh)
            mu = jnp.mean(o, axis=-1, keepdims=True)
            d = o - mu
            var = jnp.mean(d * d, axis=-1, keepdims=True)
            y = d * lax.rsqrt(var + RMS_EPS) * gn_ref[:, vs]
            g = g_ref[tok, vs].astype(F32)
            y_ref[tok, vs] = (y * (g * jax.nn.sigmoid(g))).astype(BF16)
        state_ref[h] = state


NOTSEL_LANE0 = NSA_DH


def _nsa_kernel(q_ref, gate_ref, ksel_ref, vsel_e_ref, vsel_o_ref, kwin_ref, vwin_e_ref, vwin_o_ref,
                kc_ref, vc_ref, wbias_ref, cbias_ref, o_ref,
                acc_e, acc_o, m_e, m_o, out_acc, s_next, gate_tiles):
    TQ, KT, WK = NSA_TQ, NSA_KT, NSA_WK
    HP = HEAD_PAIRS
    n_blk = ksel_ref.shape[2] // SEL_BLOCK
    qt = pl.program_id(2)
    q0 = qt * TQ
    q_plain = tuple(jnp.concatenate([q_ref[0, :, (x * HP + hp) * LANES:(x * HP + hp + 1) * LANES]
                                     for hp in range(HP)], axis=0) for x in range(2))
    t_col = q0 + lax.broadcasted_iota(jnp.int32, (TQ, 1), 0)
    lane = lax.broadcasted_iota(jnp.int32, (TQ, LANES), 1)
    lower_half = lane < NSA_DH
    rows = [slice(hp * TQ, (hp + 1) * TQ) for hp in range(HP)]
    gates = gate_ref[0]

    for branch in range(3):
        for hp in range(HP):
            c = branch * NSA_HPG + hp * 2
            gate_tiles[branch * HP + hp] = jnp.where(lower_half, gates[:, c + 1:c + 2], gates[:, c:c + 1])

    def emit(branch, acc_pair):
        for hp in range(HP):
            a_e = acc_pair[0][rows[hp]]
            a_o = acc_pair[1][rows[hp]]
            weight = gate_tiles[branch * HP + hp] / pltpu.roll(jnp.where(lower_half, a_e, a_o), NSA_DH, 1)
            out_acc[rows[hp]] += jnp.where(lower_half, a_o, a_e) * weight

    c_bias = jnp.where(lane * CMP_STRIDE + (CMP_LEN - 1) <= t_col, 0.0, MASK_BIAS)
    q_pairs = jnp.concatenate([q_ref[0, :, (2 * HP + hp) * LANES:(2 * HP + hp + 1) * LANES]
                               for hp in range(HP)], axis=0)
    s = _dot_tb(q_pairs, kc_ref[0, 0])
    p_sum = jnp.zeros((TQ, LANES), F32)
    ps = []
    for hp in range(HP):
        halves = []
        for x in range(2):
            sh = s[rows[hp], x * LANES:(x + 1) * LANES] + c_bias
            m = jnp.maximum(jnp.max(sh, axis=-1, keepdims=True), MAX_FLOOR)
            e = jnp.exp2(sh - m)
            l = jnp.sum(e, axis=-1, keepdims=True)
            p = e * jnp.where(l > 0, 1.0 / l, 0.0)
            p_sum = p_sum + p
            halves.append(p.astype(BF16))
        ps.append(jnp.concatenate(halves, axis=1))
    cmp_out = _dot(jnp.concatenate(ps, axis=0), vc_ref[0, 0])
    for hp in range(HP):
        out_acc[rows[hp]] = cmp_out[rows[hp]] * gate_tiles[hp]

    w0 = pl.multiple_of(jnp.maximum(q0 - WINDOW, 0), TQ)
    w_bias = wbias_ref[jnp.minimum(qt, WINDOW // TQ)]
    kk = kwin_ref[0, 0, pl.ds(w0, WK), :]
    win_acc = []
    for qx, v_ref in zip(q_plain, (vwin_e_ref, vwin_o_ref)):
        s = _dot_tb(qx, kk)
        es = []
        for hp in range(HP):
            sh = s[rows[hp]] + w_bias
            es.append(jnp.exp2(sh - jnp.max(sh, axis=-1, keepdims=True)).astype(BF16))
        win_acc.append(_dot(jnp.concatenate(es, axis=0), v_ref[0, 0, pl.ds(w0, WK), :]))
    emit(2, win_acc)

    ni = lax.broadcasted_iota(jnp.int32, (LANES, LANES), 0)
    ci = lax.broadcasted_iota(jnp.int32, (LANES, LANES), 1)
    overlap_t = ((ci * CMP_STRIDE < ni * SEL_BLOCK + SEL_BLOCK)
                 & (ci * CMP_STRIDE + CMP_LEN - 1 >= ni * SEL_BLOCK) & (ni < n_blk))
    overlap_t = jnp.where(overlap_t, 1.0, 0.0).astype(BF16)
    p_hi = p_sum.astype(BF16)
    p_lo = (p_sum - p_hi.astype(F32)).astype(BF16)
    imp_t = (_dot_tb(overlap_t, p_hi) + _dot_tb(overlap_t, p_lo))[0:n_blk]
    blk_id = lax.broadcasted_iota(jnp.int32, (n_blk, TQ), 0)
    cur = (q0 + lax.broadcasted_iota(jnp.int32, (n_blk, TQ), 1)) >> 6
    forced = (blk_id == 0) | (blk_id == cur) | (blk_id == cur - 1)
    score = jnp.where(forced, FORCE_SCORE, jnp.where(blk_id <= cur, imp_t, -1.0))
    rank = jnp.zeros((n_blk, TQ), jnp.int32)
    for mblk in range(n_blk):
        other = score[mblk:mblk + 1, :]
        ahead = (other > score) | ((other == score) & (blk_id > mblk))
        rank = rank + jnp.where(ahead, 1, 0)
    not_sel_t = jnp.where(rank < SEL_TOPN, 0.0, 1.0)
    padded = jnp.concatenate([jnp.zeros((NOTSEL_LANE0, TQ), F32), not_sel_t,
                              jnp.zeros((LANES - NOTSEL_LANE0 - n_blk, TQ), F32)], axis=0)
    not_sel = jnp.concatenate([padded.T.astype(BF16)] * HP, axis=0)
    q_aug = tuple(qx + not_sel for qx in q_plain)

    for ref in (acc_e, acc_o):
        ref[...] = jnp.zeros_like(ref)
    for ref in (m_e, m_o):
        ref[...] = jnp.full_like(ref, MASK_BIAS)

    last_k0 = ksel_ref.shape[2] - KT

    def even_scores(kt):
        k0 = pl.multiple_of(jnp.minimum(kt * KT, last_k0), KT)
        return _dot_tb(q_aug[0], ksel_ref[0, 0, pl.ds(k0, KT), :])

    def softmax_pv(get_scores, bias, vv, acc, m_ref):
        es, alphas = [], []
        for hp in range(HP):
            sh = get_scores(hp) + bias
            tiles = [sh[:, j * LANES:(j + 1) * LANES] for j in range(bias.shape[1] // LANES)]
            m_old = m_ref[rows[hp]]
            m_new = jnp.maximum(m_old, jnp.max(functools.reduce(jnp.maximum, tiles), axis=-1, keepdims=True))
            m_ref[rows[hp]] = m_new
            alphas.append(jnp.exp2(m_old - m_new))
            es.append(jnp.concatenate([jnp.exp2(t - m_new).astype(BF16) for t in tiles], axis=1))
        pv = _dot(jnp.concatenate(es, axis=0), vv)
        for hp in range(HP):
            acc[rows[hp]] = alphas[hp] * acc[rows[hp]] + pv[rows[hp]]

    s_next[...] = even_scores(0)

    def sel_tile(kt, carry):
        k0 = pl.multiple_of(kt * KT, KT)
        bias = cbias_ref[jnp.minimum(qt - kt * (KT // TQ), KT // TQ)]
        s_odd = _dot_tb(q_aug[1], ksel_ref[0, 0, pl.ds(k0, KT), :])
        softmax_pv(lambda hp: s_next[rows[hp], :], bias, vsel_e_ref[0, 0, pl.ds(k0, KT), :], acc_e, m_e)
        s_next[...] = even_scores(kt + 1)
        softmax_pv(lambda hp: s_odd[rows[hp]], bias, vsel_o_ref[0, 0, pl.ds(k0, KT), :], acc_o, m_o)
        return carry

    n_whole = (q0 + TQ) // KT
    lax.fori_loop(0, n_whole, sel_tile, 0)

    @pl.when(n_whole * KT < q0 + TQ)
    def _diagonal_remainder():
        k_rem = pl.multiple_of(q0, TQ)
        kk = ksel_ref[0, 0, pl.ds(k_rem, TQ), :]
        bias = cbias_ref[0][:, 0:TQ]
        for qx, v_ref, acc, m_ref in ((q_aug[0], vsel_e_ref, acc_e, m_e), (q_aug[1], vsel_o_ref, acc_o, m_o)):
            s = _dot_tb(qx, kk)
            softmax_pv(lambda hp: s[rows[hp]], bias, v_ref[0, 0, pl.ds(k_rem, TQ), :], acc, m_ref)

    emit(1, (acc_e, acc_o))

    o_ref[0] = jnp.concatenate([out_acc[rows[hp]] for hp in range(HP)], axis=1).astype(BF16)


def _nsa(nq, gates, kv, kc, vc):
    B, S, _ = nq.shape
    G = NSA_GROUPS
    TQ = NSA_TQ
    assert S // SEL_BLOCK <= LANES - NOTSEL_LANE0 and S % NSA_KT == 0 and S >= NSA_WK
    assert NSA_KT % TQ == 0 and WINDOW % TQ == 0
    rows = HEAD_PAIRS * TQ
    i = jnp.arange(TQ, dtype=jnp.int32)[None, :, None]
    off = lambda n: jnp.arange(n + 1, dtype=jnp.int32)[:, None, None] * TQ
    j = jnp.arange(NSA_WK, dtype=jnp.int32)[None, None, :]
    t_rel = off(WINDOW // TQ) + i
    win_bias = jnp.where((j <= t_rel) & (j > t_rel - WINDOW), 0.0, MASK_BIAS).astype(F32)
    j = jnp.arange(NSA_KT, dtype=jnp.int32)[None, None, :]
    causal_bias = jnp.where(j <= off(NSA_KT // TQ) + i, 0.0, MASK_BIAS).astype(F32)
    qblk = pl.BlockSpec((1, TQ, Q_STACKS * NSA_GQ_W), lambda b, g, t: (b, t, g))
    oblk = pl.BlockSpec((1, TQ, NSA_GQ_W), lambda b, g, t: (b, t, g))
    gblk = pl.BlockSpec((1, TQ, LANES), lambda b, g, t: (b, t, g))
    kvblk = lambda j: pl.BlockSpec((1, 1, S, LANES), lambda b, g, t: (g, b, 0, j))
    cblk = pl.BlockSpec((1, 1, 2 * LANES, LANES), lambda b, g, t: (b, g, 0, 0))
    wide = pltpu.VMEM((rows, LANES), F32)
    return pl.pallas_call(
        _nsa_kernel,
        grid=(B, G, S // TQ),
        in_specs=([qblk, gblk] + [kvblk(j) for j in range(KV_TILES)] + [cblk, cblk]
                  + [_resident(win_bias.shape), _resident(causal_bias.shape)]),
        out_specs=oblk,
        out_shape=jax.ShapeDtypeStruct((B, S, NSA_Q_W), BF16),
        scratch_shapes=[wide] * 5 + [pltpu.VMEM((rows, NSA_KT), F32),
                                     pltpu.VMEM((3 * NSA_HPG // 2, TQ, LANES), F32)],
        compiler_params=pltpu.CompilerParams(dimension_semantics=("arbitrary",) * 3,
                                             vmem_limit_bytes=VMEM_LIMIT),
        name="nsa",
    )(nq, gates, *([kv] * KV_TILES), kc, vc, win_bias, causal_bias)


def _tail_kernel(final, x_ref, yr_ref, yn_ref, p_ref, gmix_ref, gmlp_ref, gple_ref, gfin_ref,
                 wmg_ref, wro_ref, wno_ref, wout_ref, wup_ref, wdn_ref, wpg_ref, wpp_ref, o_ref):
    x = x_ref[...]
    h = _rms(x, gmix_ref[...]).astype(BF16)
    o_ret = _dot(yr_ref[...], wro_ref[...])
    o_nsa = _dot(yn_ref[...], wno_ref[...])
    g_ret = jax.nn.sigmoid(_dot(h, wmg_ref[:, 0:D_MODEL]))
    g_nsa = jax.nn.sigmoid(_dot(h, wmg_ref[:, D_MODEL:2 * D_MODEL]))
    mix = (g_ret * o_ret + g_nsa * o_nsa).astype(BF16)
    x = x + _dot(mix, wout_ref[...])
    h2 = _rms(x, gmlp_ref[...]).astype(BF16)
    mlp = jnp.zeros_like(x)
    step = 1024
    for c0 in range(0, MLP_HIDDEN, step):
        up = jnp.maximum(_dot(h2, wup_ref[:, c0:c0 + step]), 0.0)
        mlp = mlp + _dot((up * up).astype(BF16), wdn_ref[c0:c0 + step, :])
    x = x + mlp
    h3 = _rms(x, gple_ref[...]).astype(BF16)
    ple_gate = jax.nn.sigmoid(_dot(h3, wpg_ref[...]))
    x = x + _dot(p_ref[...].astype(BF16), wpp_ref[...]) * ple_gate
    if final:
        x = _rms(x, gfin_ref[...])
    o_ref[...] = x


def _tail(final, x2, y_ret, y_nsa, p2, g_mix, g_mlp, g_ple, g_fin, w_mg, w_ro, w_no, w_out, w_up, w_dn,
          w_pg, w_pp):
    T = x2.shape[0]
    tm = TAIL_TM
    row = lambda w: pl.BlockSpec((tm, w), lambda i: (i, 0))
    gains = [_resident((1, D_MODEL))] * 4
    weights = [_resident(w.shape) for w in (w_mg, w_ro, w_no, w_out, w_up, w_dn, w_pg, w_pp)]
    return pl.pallas_call(
        functools.partial(_tail_kernel, final),
        grid=(T // tm,),
        in_specs=[row(D_MODEL), row(RET_V_W), row(NSA_Q_W), row(PLE_DIM)] + gains + weights,
        out_specs=row(D_MODEL),
        out_shape=jax.ShapeDtypeStruct((T, D_MODEL), F32),
        compiler_params=pltpu.CompilerParams(dimension_semantics=("arbitrary",),
                                             vmem_limit_bytes=VMEM_LIMIT),
        name="tail",
    )(x2, y_ret, y_nsa, p2, g_mix, g_mlp, g_ple, g_fin, w_mg, w_ro, w_no, w_out, w_up, w_dn, w_pg, w_pp)


def _pack_w_in(w):
    gate = w[:, C_GATE:C_GATE + 3 * NSA_HEADS]
    parts = [w[:, :C_GATE]]
    for g in range(NSA_GROUPS):
        cols = [j * NSA_HEADS + g * NSA_HPG + r for j in range(3) for r in range(NSA_HPG)]
        parts.append(jnp.pad(gate[:, jnp.array(cols)], ((0, 0), (0, LANES - len(cols)))))
    return jnp.concatenate(parts, axis=1).astype(BF16)


def kernel(x, p, positions, norm_mix_g, w_in, ret_gn_g, w_ret_o, cmp_pe_k, cmp_k_w1, cmp_k_w2, cmp_pe_v, cmp_v_w1, cmp_v_w2, w_nsa_o, w_merge_gate, w_out, norm_mlp_g, w_mlp_up, w_mlp_down, norm_ple_g, w_ple_gate, w_ple_proj, norm_final_g):
    B, S, D = x.shape
    depth = p.shape[0]
    T = B * S
    G = NSA_GROUPS
    bf = lambda a: a.astype(BF16)
    row = lambda a: a.reshape(1, -1)
    posf = positions.reshape(T, 1).astype(F32)
    inv_r = ROPE_THETA ** (-jnp.arange(0, RET_DK, 2, dtype=F32) / RET_DK)
    inv_n = ROPE_THETA ** (-jnp.arange(0, NSA_DH, 2, dtype=F32) / NSA_DH)
    inv_freq = jnp.concatenate([inv_r, inv_n, jnp.zeros((LANES - inv_r.shape[0] - inv_n.shape[0],), F32)])
    inv_freq = inv_freq.reshape(1, LANES)
    n_rows = S // CMP_STRIDE

    def strides(t):
        t = t.reshape(B, S, G, NSA_DH).transpose(0, 2, 1, 3)
        return t.reshape(B, G, n_rows, CMP_STRIDE * NSA_DH)

    def pe_rows(pe):
        return jnp.broadcast_to(bf(pe).reshape(1, -1), (8, CMP_LEN * NSA_DH))

    def swap_pairs(w):
        return w.reshape(NSA_HEADS // 2, 2, NSA_DH, -1)[:, ::-1].reshape(w.shape)

    def widen(w, outer):
        z = jnp.zeros_like(w)
        return bf(jnp.concatenate([w, z, z, w] if outer else [z, w, w, z], axis=1))

    x2 = x.reshape(T, D)
    for i in range(depth):
        y_ret, nq, cmp_kv, kv, gates = _proj(S, x2, posf, row(norm_mix_g[i]), inv_freq, row(ret_gn_g[i]),
                                              _pack_w_in(w_in[i]))
        kc, vc = _compress(strides(cmp_kv[:, :LANES]), strides(cmp_kv[:, LANES:]),
                           bf(cmp_k_w1[i]), widen(cmp_k_w2[i], True), pe_rows(cmp_pe_k[i]),
                           bf(cmp_v_w1[i]), widen(cmp_v_w2[i], False), pe_rows(cmp_pe_v[i]))
        sh3 = lambda a: a.reshape(B, S, a.shape[-1])
        y_nsa = _nsa(sh3(nq), sh3(gates), kv.reshape(G, B, S, kv.shape[-1]), kc, vc)
        x2 = _tail(i == depth - 1, x2, y_ret, y_nsa.reshape(T, NSA_Q_W),
                   p[i].reshape(T, PLE_DIM), row(norm_mix_g[i]), row(norm_mlp_g[i]), row(norm_ple_g[i]),
                   row(norm_final_g), bf(w_merge_gate[i]), bf(w_ret_o[i]), bf(swap_pairs(w_nsa_o[i])), bf(w_out[i]),
                   bf(w_mlp_up[i]), bf(w_mlp_down[i]), bf(w_ple_gate[i]), bf(w_ple_proj[i]))
    return x2.reshape(B, S, D)
```

```python
import functools
import math

import jax
import jax.numpy as jnp
from jax import lax
from jax.experimental import pallas as pl
from jax.experimental.pallas import tpu as pltpu

F32 = jnp.float32
BF16 = jnp.bfloat16

D_MODEL = 1024
PLE_DIM = 256
RMS_EPS = 1e-6
ROPE_THETA = 10000.0
RET_HEADS = 8
RET_DK = 128
RET_DV = 256
RET_CHUNK = 256
RET_QK_W = RET_HEADS * RET_DK
RET_V_W = RET_HEADS * RET_DV
NSA_HEADS = 16
NSA_GROUPS = 2
NSA_HPG = 8
NSA_DH = 64
NSA_Q_W = NSA_HEADS * NSA_DH
NSA_GQ_W = NSA_HPG * NSA_DH
HEAD_PAIRS = NSA_HPG // 2
Q_STACKS = 3
CMP_LEN = 32
CMP_STRIDE = 16
CMP_HIDDEN = 256
SEL_BLOCK = 64
SEL_TOPN = 8
WINDOW = 512
FORCE_SCORE = 1e6
MLP_HIDDEN = 4 * D_MODEL

LANES = 128
MASK_BIAS = -1e30
MAX_FLOOR = -1e29
LOG2E = math.log2(math.e)
VMEM_LIMIT = 56 * 1024 * 1024

PROJ_TM = 256
TAIL_TM = 512
NSA_TQ = 256
NSA_KT = 512
NSA_WK = WINDOW + NSA_TQ

C_RQ, C_RK, C_RV, C_RG = 0, 1024, 2048, 4096
C_NQ, C_KV, C_GATE, C_END = 6144, 7168, 7936, 8192
GATE_W = NSA_GROUPS * LANES
KV_TILE = {2: 0, 3: 1, 4: 3, 5: 4}
KV_TILES = 6


def _resident(shape):
    nd = len(shape)
    return pl.BlockSpec(shape, lambda *_: (0,) * nd, pipeline_mode=pl.Buffered(1))


def _rms(x, g):
    return x * lax.rsqrt(jnp.mean(x * x, axis=-1, keepdims=True) + RMS_EPS) * g


def _dot(a, b):
    return jnp.dot(a, b, preferred_element_type=F32)


def _dot_tb(a, b):
    return lax.dot_general(a, b, (((1,), (1,)), ((), ())), preferred_element_type=F32)


def _dot_ta(a, b):
    return lax.dot_general(a, b, (((0,), (0,)), ((), ())), preferred_element_type=F32)


def _proj_kernel(seq_len, x_ref, pos_ref, g_ref, inv_ref, gn_ref, w_ref,
                 yret_ref, nq_ref, cmp_ref, kv_ref, gate_ref,
                 rq_ref, rk_ref, rv_ref, rg_ref, state_ref, decay_ref, xi_ref, zeta_ref):
    tm = x_ref.shape[0]
    first_row = pl.program_id(0) * tm

    @pl.when(first_row == 0)
    def _tables():
        _retention_tables(decay_ref, xi_ref, zeta_ref)

    h = _rms(x_ref[...], g_ref[...]).astype(BF16)
    pos = pos_ref[...]
    lane = lax.broadcasted_iota(jnp.int32, (tm, LANES), 1)
    ang = pos * inv_ref[...]
    cos_a = jnp.cos(ang)
    sin_a = jnp.sin(ang)
    cos_r = jnp.where(lane < 64, cos_a, pltpu.roll(cos_a, 64, 1))
    sin_r = jnp.where(lane < 64, -sin_a, pltpu.roll(sin_a, 64, 1))

    def tile_nsa(t):
        return jnp.where(lane < 32, pltpu.roll(t, 64, 1),
                         jnp.where(lane < 64, pltpu.roll(t, 96, 1), jnp.where(lane < 96, t, pltpu.roll(t, 32, 1))))

    low = (lane & 32) == 0
    cos_n = tile_nsa(cos_a)
    sin_n = tile_nsa(sin_a)
    sin_n = jnp.where(low, -sin_n, sin_n)

    def rope_r(y):
        return y * cos_r + pltpu.roll(y, 64, 1) * sin_r

    def rope_n(y):
        partner = jnp.where(low, pltpu.roll(y, 96, 1), pltpu.roll(y, 32, 1))
        return y * cos_n + partner * sin_n

    seq_pos = first_row % seq_len + lax.broadcasted_iota(jnp.int32, (tm, LANES), 0)
    sel_tag = jnp.where(lane - NSA_DH == seq_pos // SEL_BLOCK, MASK_BIAS, 0.0)

    k_scale = RET_DK ** -0.5
    q_scale = NSA_DH ** -0.5 * LOG2E
    chunk = 512
    heads_per_chunk = chunk // RET_DV
    plan = []
    for v_chunk in range(RET_V_W // chunk):
        plan.append(C_RV + v_chunk * chunk)
        if (v_chunk * heads_per_chunk) % (chunk // RET_DK) == 0:
            qk_chunk = v_chunk * heads_per_chunk * RET_DK
            plan += [C_RQ + qk_chunk, C_RK + qk_chunk]
        plan.append(C_RG + v_chunk * chunk)
        plan.append(tuple(range(v_chunk * heads_per_chunk, (v_chunk + 1) * heads_per_chunk)))
    plan += list(range(C_NQ, C_END, chunk))
    for c0 in plan:
        if isinstance(c0, tuple):
            _retention_heads(c0, first_row % seq_len == 0, rq_ref, rk_ref, rv_ref, rg_ref, gn_ref, yret_ref,
                             state_ref, decay_ref, xi_ref, zeta_ref)
            continue
        y = _dot(h, w_ref[:, c0:c0 + chunk])
        for j in range(chunk // LANES):
            col = c0 + j * LANES
            piece = y[:, j * LANES:(j + 1) * LANES]
            if col < C_RK:
                rq_ref[:, col - C_RQ:col - C_RQ + LANES] = rope_r(piece).astype(BF16)
            elif col < C_RV:
                rk_ref[:, col - C_RK:col - C_RK + LANES] = (rope_r(piece) * k_scale).astype(BF16)
            elif col < C_RG:
                rv_ref[:, col - C_RV:col - C_RV + LANES] = piece.astype(BF16)
            elif col < C_NQ:
                rg_ref[:, col - C_RG:col - C_RG + LANES] = piece.astype(BF16)
            elif col < C_KV:
                val = rope_n(piece) * q_scale
                pair = (col - C_NQ) // LANES
                base = (pair // HEAD_PAIRS) * Q_STACKS * NSA_GQ_W + (pair % HEAD_PAIRS) * LANES
                zeros = jnp.zeros_like(val)
                nq_ref[:, base:base + LANES] = jnp.where(lane < NSA_DH, val, zeros).astype(BF16)
                nq_ref[:, base + NSA_GQ_W:base + NSA_GQ_W + LANES] = jnp.where(
                    lane < NSA_DH, pltpu.roll(val, NSA_DH, 1), zeros).astype(BF16)
                nq_ref[:, base + 2 * NSA_GQ_W:base + 2 * NSA_GQ_W + LANES] = val.astype(BF16)
            elif col < C_GATE:
                j_kv = (col - C_KV) // LANES
                is_key = j_kv % 2 == 0
                val = rope_n(piece) if is_key else piece
                if j_kv < 2:
                    cmp_ref[:, j_kv * LANES:(j_kv + 1) * LANES] = val.astype(BF16)
                    continue
                swapped = pltpu.roll(val, NSA_DH, 1)
                lower = lane < NSA_DH
                grouped = (val, swapped), (swapped, val)
                t0 = KV_TILE[j_kv]
                for g in range(NSA_GROUPS):
                    lo, hi = grouped[g]
                    if is_key:
                        fill = sel_tag if j_kv == 2 else jnp.zeros_like(val)
                        tiles = (jnp.where(lower, lo, fill),)
                    else:
                        tiles = (jnp.where(lower, 1.0, hi), jnp.where(lower, lo, 1.0))
                    for k, tile in enumerate(tiles):
                        kv_ref[g, :, (t0 + k) * LANES:(t0 + k + 1) * LANES] = tile.astype(BF16)
            else:
                gate_ref[:, col - C_GATE:col - C_GATE + LANES] = jax.nn.sigmoid(piece)


def _proj(seq_len, x2, posf, g_mix, inv_freq, gn_g, w_all):
    T = x2.shape[0]
    tm = PROJ_TM
    C = RET_CHUNK
    assert seq_len % tm == 0 and tm % C == 0
    row = lambda w: pl.BlockSpec((tm, w), lambda i: (i, 0))
    kv_w = KV_TILES * LANES
    out_shapes = [
        jax.ShapeDtypeStruct((T, RET_V_W), BF16),
        jax.ShapeDtypeStruct((T, Q_STACKS * NSA_Q_W), BF16),
        jax.ShapeDtypeStruct((T, 2 * LANES), BF16),
        jax.ShapeDtypeStruct((NSA_GROUPS, T, kv_w), BF16),
        jax.ShapeDtypeStruct((T, GATE_W), F32),
    ]
    tables = [pltpu.VMEM((RET_HEADS, C, w), F32) for w in (C, RET_DV, RET_DK)]
    return pl.pallas_call(
        functools.partial(_proj_kernel, seq_len),
        grid=(T // tm,),
        in_specs=[row(D_MODEL), row(1), _resident((1, D_MODEL)), _resident((1, LANES)), _resident((1, RET_V_W)),
                  _resident((D_MODEL, C_END))],
        out_specs=[row(RET_V_W), row(Q_STACKS * NSA_Q_W), row(2 * LANES),
                   pl.BlockSpec((NSA_GROUPS, tm, kv_w), lambda i: (0, i, 0)), row(GATE_W)],
        out_shape=out_shapes,
        scratch_shapes=[pltpu.VMEM((tm, RET_QK_W), BF16), pltpu.VMEM((tm, RET_QK_W), BF16),
                        pltpu.VMEM((tm, RET_V_W), BF16), pltpu.VMEM((tm, RET_V_W), BF16),
                        pltpu.VMEM((RET_HEADS, RET_DK, RET_DV), F32)] + tables,
        compiler_params=pltpu.CompilerParams(dimension_semantics=("arbitrary",),
                                             vmem_limit_bytes=VMEM_LIMIT),
        name="proj",
    )(x2, posf, g_mix, inv_freq, gn_g, w_all)


def _compress_kernel(k16_ref, v16_ref, w1k_ref, w2k_ref, pek_ref, w1v_ref, w2v_ref, pev_ref,
                     kc_ref, vc_ref):
    half = CMP_STRIDE * NSA_DH
    for x_ref, w1_ref, w2_ref, pe_ref, o_ref in ((k16_ref, w1k_ref, w2k_ref, pek_ref, kc_ref),
                                                 (v16_ref, w1v_ref, w2v_ref, pev_ref, vc_ref)):
        x = x_ref[0, 0]
        first = _dot(x, w1_ref[0:half, :])
        second = _dot(x, w1_ref[half:2 * half, :])
        pe_term = _dot(pe_ref[...], w1_ref[...])[0:1, :]
        hidden = first + pltpu.roll(second, second.shape[0] - 1, 0) + pe_term
        act = jax.nn.gelu(hidden).astype(BF16)
        both = _dot(act, w2_ref[...]).astype(BF16)
        n_cmp = both.shape[0]
        o_ref[0, 0, 0:n_cmp, :] = both[:, 0:LANES]
        o_ref[0, 0, n_cmp:2 * n_cmp, :] = both[:, LANES:2 * LANES]


def _compress(k16, v16, w1k, w2k, pek, w1v, w2v, pev):
    B, G, R, W = k16.shape
    blk = pl.BlockSpec((1, 1, R, W), lambda b, g: (b, g, 0, 0))
    oblk = pl.BlockSpec((1, 1, 2 * R, LANES), lambda b, g: (b, g, 0, 0))
    out = jax.ShapeDtypeStruct((B, G, 2 * R, LANES), BF16)
    wspecs = [_resident(w1k.shape), _resident(w2k.shape), _resident(pek.shape)]
    return pl.pallas_call(
        _compress_kernel,
        grid=(B, G),
        in_specs=[blk, blk] + wspecs + wspecs,
        out_specs=[oblk, oblk],
        out_shape=[out, out],
        compiler_params=pltpu.CompilerParams(dimension_semantics=("arbitrary", "arbitrary"),
                                             vmem_limit_bytes=VMEM_LIMIT),
        name="compress",
    )(k16, v16, w1k, w2k, pek, w1v, w2v, pev)


_RET_LOG_G = [math.log(1.0 - 2.0 ** (-5.0 - h)) for h in range(RET_HEADS)]


def _retention_tables(decay_ref, xi_ref, zeta_ref):
    C = RET_CHUNK
    r = lax.broadcasted_iota(jnp.int32, (C, C), 0).astype(F32)
    c = lax.broadcasted_iota(jnp.int32, (C, C), 1).astype(F32)
    diff = r - c
    r_out = lax.broadcasted_iota(jnp.int32, (C, RET_DV), 0).astype(F32)
    r_key = lax.broadcasted_iota(jnp.int32, (C, RET_DK), 0).astype(F32)
    for h in range(RET_HEADS):
        lg = _RET_LOG_G[h]
        decay_ref[h] = jnp.where(diff >= 0, jnp.exp(jnp.maximum(diff, 0.0) * lg), 0.0)
        xi_ref[h] = jnp.exp((r_out + 1.0) * lg)
        zeta_ref[h] = jnp.exp((C - 1.0 - r_key) * lg)


def _retention_heads(heads, new_sequence, q_ref, k_ref, v_ref, g_ref, gn_ref, y_ref, state_ref, decay_ref, xi_ref,
                     zeta_ref):
    C = RET_CHUNK
    n_chunks = q_ref.shape[0] // C
    for h in heads:
        qs = slice(h * RET_DK, (h + 1) * RET_DK)
        vs = slice(h * RET_DV, (h + 1) * RET_DV)
        xi = xi_ref[h]
        state = jnp.where(new_sequence, 0.0, state_ref[h])
        for c in range(n_chunks):
            tok = slice(c * C, (c + 1) * C)
            qh = q_ref[tok, qs]
            kh = k_ref[tok, qs]
            vh = v_ref[tok, vs]
            inner = (_dot_tb(qh, kh) * decay_ref[h]).astype(BF16)
            o = _dot(inner, vh) + _dot(qh, state.astype(BF16)) * xi
            kz = (kh.astype(F32) * zeta_ref[h]).astype(BF16)
            state = math.exp(C * _RET_LOG_G[h]) * state + _dot_ta(kz, vh)
            mu = jnp.mean(o, axis=-1, keepdims=True)
            d = o - mu
            var = jnp.mean(d * d, axis=-1, keepdims=True)
            y = d * lax.rsqrt(var + RMS_EPS) * gn_ref[:, vs]
            g = g_ref[tok, vs].astype(F32)
            y_ref[tok, vs] = (y * (g * jax.nn.sigmoid(g))).astype(BF16)
        state_ref[h] = state


NOTSEL_LANE0 = NSA_DH


def _nsa_kernel(q_ref, gate_ref, ksel_ref, vsel_e_ref, vsel_o_ref, kwin_ref, vwin_e_ref, vwin_o_ref,
                kc_ref, vc_ref, wbias_ref, cbias_ref, o_ref,
                acc_e, acc_o, m_e, m_o, out_acc, s_next, gate_tiles):
    TQ, KT, WK = NSA_TQ, NSA_KT, NSA_WK
    HP = HEAD_PAIRS
    n_blk = ksel_ref.shape[2] // SEL_BLOCK
    qt = pl.program_id(2)
    q0 = qt * TQ
    q_plain = tuple(jnp.concatenate([q_ref[0, :, (x * HP + hp) * LANES:(x * HP + hp + 1) * LANES]
                                     for hp in range(HP)], axis=0) for x in range(2))
    t_col = q0 + lax.broadcasted_iota(jnp.int32, (TQ, 1), 0)
    lane = lax.broadcasted_iota(jnp.int32, (TQ, LANES), 1)
    lower_half = lane < NSA_DH
    rows = [slice(hp * TQ, (hp + 1) * TQ) for hp in range(HP)]
    gates = gate_ref[0]

    for branch in range(3):
        for hp in range(HP):
            c = branch * NSA_HPG + hp * 2
            gate_tiles[branch * HP + hp] = jnp.where(lower_half, gates[:, c + 1:c + 2], gates[:, c:c + 1])

    def emit(branch, acc_pair):
        for hp in range(HP):
            a_e = acc_pair[0][rows[hp]]
            a_o = acc_pair[1][rows[hp]]
            weight = gate_tiles[branch * HP + hp] / pltpu.roll(jnp.where(lower_half, a_e, a_o), NSA_DH, 1)
            out_acc[rows[hp]] += jnp.where(lower_half, a_o, a_e) * weight

    c_bias = jnp.where(lane * CMP_STRIDE + (CMP_LEN - 1) <= t_col, 0.0, MASK_BIAS)
    q_pairs = jnp.concatenate([q_ref[0, :, (2 * HP + hp) * LANES:(2 * HP + hp + 1) * LANES]
                               for hp in range(HP)], axis=0)
    s = _dot_tb(q_pairs, kc_ref[0, 0])
    p_sum = jnp.zeros((TQ, LANES), F32)
    ps = []
    for hp in range(HP):
        halves = []
        for x in range(2):
            sh = s[rows[hp], x * LANES:(x + 1) * LANES] + c_bias
            m = jnp.maximum(jnp.max(sh, axis=-1, keepdims=True), MAX_FLOOR)
            e = jnp.exp2(sh - m)
            l = jnp.sum(e, axis=-1, keepdims=True)
            p = e * jnp.where(l > 0, 1.0 / l, 0.0)
            p_sum = p_sum + p
            halves.append(p.astype(BF16))
        ps.append(jnp.concatenate(halves, axis=1))
    cmp_out = _dot(jnp.concatenate(ps, axis=0), vc_ref[0, 0])
    for hp in range(HP):
        out_acc[rows[hp]] = cmp_out[rows[hp]] * gate_tiles[hp]

    w0 = pl.multiple_of(jnp.maximum(q0 - WINDOW, 0), TQ)
    w_bias = wbias_ref[jnp.minimum(qt, WINDOW // TQ)]
    kk = kwin_ref[0, 0, pl.ds(w0, WK), :]
    win_acc = []
    for qx, v_ref in zip(q_plain, (vwin_e_ref, vwin_o_ref)):
        s = _dot_tb(qx, kk)
        es = []
        for hp in range(HP):
            sh = s[rows[hp]] + w_bias
            es.append(jnp.exp2(sh - jnp.max(sh, axis=-1, keepdims=True)).astype(BF16))
        win_acc.append(_dot(jnp.concatenate(es, axis=0), v_ref[0, 0, pl.ds(w0, WK), :]))
    emit(2, win_acc)

    ni = lax.broadcasted_iota(jnp.int32, (LANES, LANES), 0)
    ci = lax.broadcasted_iota(jnp.int32, (LANES, LANES), 1)
    overlap_t = ((ci * CMP_STRIDE < ni * SEL_BLOCK + SEL_BLOCK)
                 & (ci * CMP_STRIDE + CMP_LEN - 1 >= ni * SEL_BLOCK) & (ni < n_blk))
    overlap_t = jnp.where(overlap_t, 1.0, 0.0).astype(BF16)
    p_hi = p_sum.astype(BF16)
    p_lo = (p_sum - p_hi.astype(F32)).astype(BF16)
    imp_t = (_dot_tb(overlap_t, p_hi) + _dot_tb(overlap_t, p_lo))[0:n_blk]
    blk_id = lax.broadcasted_iota(jnp.int32, (n_blk, TQ), 0)
    cur = (q0 + lax.broadcasted_iota(jnp.int32, (n_blk, TQ), 1)) >> 6
    forced = (blk_id == 0) | (blk_id == cur) | (blk_id == cur - 1)
    score = jnp.where(forced, FORCE_SCORE, jnp.where(blk_id <= cur, imp_t, -1.0))
    rank = jnp.zeros((n_blk, TQ), jnp.int32)
    for mblk in range(n_blk):
        other = score[mblk:mblk + 1, :]
        ahead = (other > score) | ((other == score) & (blk_id > mblk))
        rank = rank + jnp.where(ahead, 1, 0)
    not_sel_t = jnp.where(rank < SEL_TOPN, 0.0, 1.0)
    padded = jnp.concatenate([jnp.zeros((NOTSEL_LANE0, TQ), F32), not_sel_t,
                              jnp.zeros((LANES - NOTSEL_LANE0 - n_blk, TQ), F32)], axis=0)
    not_sel = jnp.concatenate([padded.T.astype(BF16)] * HP, axis=0)
    q_aug = tuple(qx + not_sel for qx in q_plain)

    for ref in (acc_e, acc_o):
        ref[...] = jnp.zeros_like(ref)
    for ref in (m_e, m_o):
        ref[...] = jnp.full_like(ref, MASK_BIAS)

    last_k0 = ksel_ref.shape[2] - KT

    def even_scores(kt):
        k0 = pl.multiple_of(jnp.minimum(kt * KT, last_k0), KT)
        return _dot_tb(q_aug[0], ksel_ref[0, 0, pl.ds(k0, KT), :])

    def softmax_pv(get_scores, bias, vv, acc, m_ref):
        es, alphas = [], []
        for hp in range(HP):
            sh = get_scores(hp) + bias
            tiles = [sh[:, j * LANES:(j + 1) * LANES] for j in range(bias.shape[1] // LANES)]
            m_old = m_ref[rows[hp]]
            m_new = jnp.maximum(m_old, jnp.max(functools.reduce(jnp.maximum, tiles), axis=-1, keepdims=True))
            m_ref[rows[hp]] = m_new
            alphas.append(jnp.exp2(m_old - m_new))
            es.append(jnp.concatenate([jnp.exp2(t - m_new).astype(BF16) for t in tiles], axis=1))
        pv = _dot(jnp.concatenate(es, axis=0), vv)
        for hp in range(HP):
            acc[rows[hp]] = alphas[hp] * acc[rows[hp]] + pv[rows[hp]]

    s_next[...] = even_scores(0)

    def sel_tile(kt, carry):
        k0 = pl.multiple_of(kt * KT, KT)
        bias = cbias_ref[jnp.minimum(qt - kt * (KT // TQ), KT // TQ)]
        s_odd = _dot_tb(q_aug[1], ksel_ref[0, 0, pl.ds(k0, KT), :])
        softmax_pv(lambda hp: s_next[rows[hp], :], bias, vsel_e_ref[0, 0, pl.ds(k0, KT), :], acc_e, m_e)
        s_next[...] = even_scores(kt + 1)
        softmax_pv(lambda hp: s_odd[rows[hp]], bias, vsel_o_ref[0, 0, pl.ds(k0, KT), :], acc_o, m_o)
        return carry

    n_whole = (q0 + TQ) // KT
    lax.fori_loop(0, n_whole, sel_tile, 0)

    @pl.when(n_whole * KT < q0 + TQ)
    def _diagonal_remainder():
        k_rem = pl.multiple_of(q0, TQ)
        kk = ksel_ref[0, 0, pl.ds(k_rem, TQ), :]
        bias = cbias_ref[0][:, 0:TQ]
        for qx, v_ref, acc, m_ref in ((q_aug[0], vsel_e_ref, acc_e, m_e), (q_aug[1], vsel_o_ref, acc_o, m_o)):
            s = _dot_tb(qx, kk)
            softmax_pv(lambda hp: s[rows[hp]], bias, v_ref[0, 0, pl.ds(k_rem, TQ), :], acc, m_ref)

    emit(1, (acc_e, acc_o))

    o_ref[0] = jnp.concatenate([out_acc[rows[hp]] for hp in range(HP)], axis=1).astype(BF16)


def _nsa(nq, gates, kv, kc, vc):
    B, S, _ = nq.shape
    G = NSA_GROUPS
    TQ = NSA_TQ
    assert S // SEL_BLOCK <= LANES - NOTSEL_LANE0 and S % NSA_KT == 0 and S >= NSA_WK
    assert NSA_KT % TQ == 0 and WINDOW % TQ == 0
    rows = HEAD_PAIRS * TQ
    i = jnp.arange(TQ, dtype=jnp.int32)[None, :, None]
    off = lambda n: jnp.arange(n + 1, dtype=jnp.int32)[:, None, None] * TQ
    j = jnp.arange(NSA_WK, dtype=jnp.int32)[None, None, :]
    t_rel = off(WINDOW // TQ) + i
    win_bias = jnp.where((j <= t_rel) & (j > t_rel - WINDOW), 0.0, MASK_BIAS).astype(F32)
    j = jnp.arange(NSA_KT, dtype=jnp.int32)[None, None, :]
    causal_bias = jnp.where(j <= off(NSA_KT // TQ) + i, 0.0, MASK_BIAS).astype(F32)
    qblk = pl.BlockSpec((1, TQ, Q_STACKS * NSA_GQ_W), lambda b, g, t: (b, t, g))
    oblk = pl.BlockSpec((1, TQ, NSA_GQ_W), lambda b, g, t: (b, t, g))
    gblk = pl.BlockSpec((1, TQ, LANES), lambda b, g, t: (b, t, g))
    kvblk = lambda j: pl.BlockSpec((1, 1, S, LANES), lambda b, g, t: (g, b, 0, j))
    cblk = pl.BlockSpec((1, 1, 2 * LANES, LANES), lambda b, g, t: (b, g, 0, 0))
    wide = pltpu.VMEM((rows, LANES), F32)
    return pl.pallas_call(
        _nsa_kernel,
        grid=(B, G, S // TQ),
        in_specs=([qblk, gblk] + [kvblk(j) for j in range(KV_TILES)] + [cblk, cblk]
                  + [_resident(win_bias.shape), _resident(causal_bias.shape)]),
        out_specs=oblk,
        out_shape=jax.ShapeDtypeStruct((B, S, NSA_Q_W), BF16),
        scratch_shapes=[wide] * 5 + [pltpu.VMEM((rows, NSA_KT), F32),
                                     pltpu.VMEM((3 * NSA_HPG // 2, TQ, LANES), F32)],
        compiler_params=pltpu.CompilerParams(dimension_semantics=("arbitrary",) * 3,
                                             vmem_limit_bytes=VMEM_LIMIT),
        name="nsa",
    )(nq, gates, *([kv] * KV_TILES), kc, vc, win_bias, causal_bias)


def _tail_kernel(final, x_ref, yr_ref, yn_ref, p_ref, gmix_ref, gmlp_ref, gple_ref, gfin_ref,
                 wmg_ref, wro_ref, wno_ref, wout_ref, wup_ref, wdn_ref, wpg_ref, wpp_ref, o_ref):
    x = x_ref[...]
    h = _rms(x, gmix_ref[...]).astype(BF16)
    o_ret = _dot(yr_ref[...], wro_ref[...])
    o_nsa = _dot(yn_ref[...], wno_ref[...])
    g_ret = jax.nn.sigmoid(_dot(h, wmg_ref[:, 0:D_MODEL]))
    g_nsa = jax.nn.sigmoid(_dot(h, wmg_ref[:, D_MODEL:2 * D_MODEL]))
    mix = (g_ret * o_ret + g_nsa * o_nsa).astype(BF16)
    x = x + _dot(mix, wout_ref[...])
    h2 = _rms(x, gmlp_ref[...]).astype(BF16)
    mlp = jnp.zeros_like(x)
    step = 1024
    for c0 in range(0, MLP_HIDDEN, step):
        up = jnp.maximum(_dot(h2, wup_ref[:, c0:c0 + step]), 0.0)
        mlp = mlp + _dot((up * up).astype(BF16), wdn_ref[c0:c0 + step, :])
    x = x + mlp
    h3 = _rms(x, gple_ref[...]).astype(BF16)
    ple_gate = jax.nn.sigmoid(_dot(h3, wpg_ref[...]))
    x = x + _dot(p_ref[...].astype(BF16), wpp_ref[...]) * ple_gate
    if final:
        x = _rms(x, gfin_ref[...])
    o_ref[...] = x


def _tail(final, x2, y_ret, y_nsa, p2, g_mix, g_mlp, g_ple, g_fin, w_mg, w_ro, w_no, w_out, w_up, w_dn,
          w_pg, w_pp):
    T = x2.shape[0]
    tm = TAIL_TM
    row = lambda w: pl.BlockSpec((tm, w), lambda i: (i, 0))
    gains = [_resident((1, D_MODEL))] * 4
    weights = [_resident(w.shape) for w in (w_mg, w_ro, w_no, w_out, w_up, w_dn, w_pg, w_pp)]
    return pl.pallas_call(
        functools.partial(_tail_kernel, final),
        grid=(T // tm,),
        in_specs=[row(D_MODEL), row(RET_V_W), row(NSA_Q_W), row(PLE_DIM)] + gains + weights,
        out_specs=row(D_MODEL),
        out_shape=jax.ShapeDtypeStruct((T, D_MODEL), F32),
        compiler_params=pltpu.CompilerParams(dimension_semantics=("arbitrary",),
                                             vmem_limit_bytes=VMEM_LIMIT),
        name="tail",
    )(x2, y_ret, y_nsa, p2, g_mix, g_mlp, g_ple, g_fin, w_mg, w_ro, w_no, w_out, w_up, w_dn, w_pg, w_pp)


def _pack_w_in(w):
    gate = w[:, C_GATE:C_GATE + 3 * NSA_HEADS]
    parts = [w[:, :C_GATE]]
    for g in range(NSA_GROUPS):
        cols = [j * NSA_HEADS + g * NSA_HPG + r for j in range(3) for r in range(NSA_HPG)]
        parts.append(jnp.pad(gate[:, jnp.array(cols)], ((0, 0), (0, LANES - len(cols)))))
    return jnp.concatenate(parts, axis=1).astype(BF16)


def kernel(x, p, positions, norm_mix_g, w_in, ret_gn_g, w_ret_o, cmp_pe_k, cmp_k_w1, cmp_k_w2, cmp_pe_v, cmp_v_w1, cmp_v_w2, w_nsa_o, w_merge_gate, w_out, norm_mlp_g, w_mlp_up, w_mlp_down, norm_ple_g, w_ple_gate, w_ple_proj, norm_final_g):
    B, S, D = x.shape
    depth = p.shape[0]
    T = B * S
    G = NSA_GROUPS
    bf = lambda a: a.astype(BF16)
    row = lambda a: a.reshape(1, -1)
    posf = positions.reshape(T, 1).astype(F32)
    inv_r = ROPE_THETA ** (-jnp.arange(0, RET_DK, 2, dtype=F32) / RET_DK)
    inv_n = ROPE_THETA ** (-jnp.arange(0, NSA_DH, 2, dtype=F32) / NSA_DH)
    inv_freq = jnp.concatenate([inv_r, inv_n, jnp.zeros((LANES - inv_r.shape[0] - inv_n.shape[0],), F32)])
    inv_freq = inv_freq.reshape(1, LANES)
    n_rows = S // CMP_STRIDE

    def strides(t):
        t = t.reshape(B, S, G, NSA_DH).transpose(0, 2, 1, 3)
        return t.reshape(B, G, n_rows, CMP_STRIDE * NSA_DH)

    def pe_rows(pe):
        return jnp.broadcast_to(bf(pe).reshape(1, -1), (8, CMP_LEN * NSA_DH))

    def swap_pairs(w):
        return w.reshape(NSA_HEADS // 2, 2, NSA_DH, -1)[:, ::-1].reshape(w.shape)

    def widen(w, outer):
        z = jnp.zeros_like(w)
        return bf(jnp.concatenate([w, z, z, w] if outer else [z, w, w, z], axis=1))

    x2 = x.reshape(T, D)
    for i in range(depth):
        y_ret, nq, cmp_kv, kv, gates = _proj(S, x2, posf, row(norm_mix_g[i]), inv_freq, row(ret_gn_g[i]),
                                              _pack_w_in(w_in[i]))
        kc, vc = _compress(strides(cmp_kv[:, :LANES]), strides(cmp_kv[:, LANES:]),
                           bf(cmp_k_w1[i]), widen(cmp_k_w2[i], True), pe_rows(cmp_pe_k[i]),
                           bf(cmp_v_w1[i]), widen(cmp_v_w2[i], False), pe_rows(cmp_pe_v[i]))
        sh3 = lambda a: a.reshape(B, S, a.shape[-1])
        y_nsa = _nsa(sh3(nq), sh3(gates), kv.reshape(G, B, S, kv.shape[-1]), kc, vc)
        x2 = _tail(i == depth - 1, x2, y_ret, y_nsa.reshape(T, NSA_Q_W),
                   p[i].reshape(T, PLE_DIM), row(norm_mix_g[i]), row(norm_mlp_g[i]), row(norm_ple_g[i]),
                   row(norm_final_g), bf(w_merge_gate[i]), bf(w_ret_o[i]), bf(swap_pairs(w_nsa_o[i])), bf(w_out[i]),
                   bf(w_mlp_up[i]), bf(w_mlp_down[i]), bf(w_ple_gate[i]), bf(w_ple_proj[i]))
    return x2.reshape(B, S, D)
```

```python
import functools
import math

import jax
import jax.numpy as jnp
from jax import lax
from jax.experimental import pallas as pl
from jax.experimental.pallas import tpu as pltpu

F32 = jnp.float32
BF16 = jnp.bfloat16

D_MODEL = 1024
PLE_DIM = 256
RMS_EPS = 1e-6
ROPE_THETA = 10000.0
RET_HEADS = 8
RET_DK = 128
RET_DV = 256
RET_CHUNK = 256
RET_QK_W = RET_HEADS * RET_DK
RET_V_W = RET_HEADS * RET_DV
NSA_HEADS = 16
NSA_GROUPS = 2
NSA_HPG = 8
NSA_DH = 64
NSA_Q_W = NSA_HEADS * NSA_DH
NSA_GQ_W = NSA_HPG * NSA_DH
HEAD_PAIRS = NSA_HPG // 2
Q_STACKS = 3
CMP_LEN = 32
CMP_STRIDE = 16
CMP_HIDDEN = 256
SEL_BLOCK = 64
SEL_TOPN = 8
WINDOW = 512
FORCE_SCORE = 1e6
MLP_HIDDEN = 4 * D_MODEL

LANES = 128
MASK_BIAS = -1e30
MAX_FLOOR = -1e29
LOG2E = math.log2(math.e)
VMEM_LIMIT = 56 * 1024 * 1024

PROJ_TM = 256
TAIL_TM = 512
NSA_TQ = 256
NSA_KT = 512
NSA_WK = WINDOW + NSA_TQ

C_RQ, C_RK, C_RV, C_RG = 0, 1024, 2048, 4096
C_NQ, C_KV, C_GATE, C_END = 6144, 7168, 7936, 8192
GATE_W = NSA_GROUPS * LANES
KV_TILE = {2: 0, 3: 1, 4: 3, 5: 4}
KV_TILES = 6


def _resident(shape):
    nd = len(shape)
    return pl.BlockSpec(shape, lambda *_: (0,) * nd, pipeline_mode=pl.Buffered(1))


def _rms(x, g):
    return x * lax.rsqrt(jnp.mean(x * x, axis=-1, keepdims=True) + RMS_EPS) * g


def _dot(a, b):
    return jnp.dot(a, b, preferred_element_type=F32)


def _dot_tb(a, b):
    return lax.dot_general(a, b, (((1,), (1,)), ((), ())), preferred_element_type=F32)


def _dot_ta(a, b):
    return lax.dot_general(a, b, (((0,), (0,)), ((), ())), preferred_element_type=F32)


def _proj_kernel(seq_len, x_ref, pos_ref, g_ref, inv_ref, gn_ref, w_ref,
                 yret_ref, nq_ref, cmp_ref, kv_ref, gate_ref,
                 rq_ref, rk_ref, rv_ref, rg_ref, state_ref, decay_ref, xi_ref, zeta_ref):
    tm = x_ref.shape[0]
    first_row = pl.program_id(0) * tm

    @pl.when(first_row == 0)
    def _tables():
        _retention_tables(decay_ref, xi_ref, zeta_ref)

    h = _rms(x_ref[...], g_ref[...]).astype(BF16)
    pos = pos_ref[...]
    lane = lax.broadcasted_iota(jnp.int32, (tm, LANES), 1)
    ang = pos * inv_ref[...]
    cos_a = jnp.cos(ang)
    sin_a = jnp.sin(ang)
    cos_r = jnp.where(lane < 64, cos_a, pltpu.roll(cos_a, 64, 1))
    sin_r = jnp.where(lane < 64, -sin_a, pltpu.roll(sin_a, 64, 1))

    def tile_nsa(t):
        return jnp.where(lane < 32, pltpu.roll(t, 64, 1),
                         jnp.where(lane < 64, pltpu.roll(t, 96, 1), jnp.where(lane < 96, t, pltpu.roll(t, 32, 1))))

    low = (lane & 32) == 0
    cos_n = tile_nsa(cos_a)
    sin_n = tile_nsa(sin_a)
    sin_n = jnp.where(low, -sin_n, sin_n)

    def rope_r(y):
        return y * cos_r + pltpu.roll(y, 64, 1) * sin_r

    def rope_n(y):
        partner = jnp.where(low, pltpu.roll(y, 96, 1), pltpu.roll(y, 32, 1))
        return y * cos_n + partner * sin_n

    seq_pos = first_row % seq_len + lax.broadcasted_iota(jnp.int32, (tm, LANES), 0)
    sel_tag = jnp.where(lane - NSA_DH == seq_pos // SEL_BLOCK, MASK_BIAS, 0.0)

    k_scale = RET_DK ** -0.5
    q_scale = NSA_DH ** -0.5 * LOG2E
    chunk = 512
    heads_per_chunk = chunk // RET_DV
    plan = []
    for v_chunk in range(RET_V_W // chunk):
        plan.append(C_RV + v_chunk * chunk)
        if (v_chunk * heads_per_chunk) % (chunk // RET_DK) == 0:
            qk_chunk = v_chunk * heads_per_chunk * RET_DK
            plan += [C_RQ + qk_chunk, C_RK + qk_chunk]
        plan.append(C_RG + v_chunk * chunk)
        plan.append(tuple(range(v_chunk * heads_per_chunk, (v_chunk + 1) * heads_per_chunk)))
    plan += list(range(C_NQ, C_END, chunk))
    for c0 in plan:
        if isinstance(c0, tuple):
            _retention_heads(c0, first_row % seq_len == 0, rq_ref, rk_ref, rv_ref, rg_ref, gn_ref, yret_ref,
                             state_ref, decay_ref, xi_ref, zeta_ref)
            continue
        y = _dot(h, w_ref[:, c0:c0 + chunk])
        for j in range(chunk // LANES):
            col = c0 + j * LANES
            piece = y[:, j * LANES:(j + 1) * LANES]
            if col < C_RK:
                rq_ref[:, col - C_RQ:col - C_RQ + LANES] = rope_r(piece).astype(BF16)
            elif col < C_RV:
                rk_ref[:, col - C_RK:col - C_RK + LANES] = (rope_r(piece) * k_scale).astype(BF16)
            elif col < C_RG:
                rv_ref[:, col - C_RV:col - C_RV + LANES] = piece.astype(BF16)
            elif col < C_NQ:
                rg_ref[:, col - C_RG:col - C_RG + LANES] = piece.astype(BF16)
            elif col < C_KV:
                val = rope_n(piece) * q_scale
                pair = (col - C_NQ) // LANES
                base = (pair // HEAD_PAIRS) * Q_STACKS * NSA_GQ_W + (pair % HEAD_PAIRS) * LANES
                zeros = jnp.zeros_like(val)
                nq_ref[:, base:base + LANES] = jnp.where(lane < NSA_DH, val, zeros).astype(BF16)
                nq_ref[:, base + NSA_GQ_W:base + NSA_GQ_W + LANES] = jnp.where(
                    lane < NSA_DH, pltpu.roll(val, NSA_DH, 1), zeros).astype(BF16)
                nq_ref[:, base + 2 * NSA_GQ_W:base + 2 * NSA_GQ_W + LANES] = val.astype(BF16)
            elif col < C_GATE:
                j_kv = (col - C_KV) // LANES
                is_key = j_kv % 2 == 0
                val = rope_n(piece) if is_key else piece
                if j_kv < 2:
                    cmp_ref[:, j_kv * LANES:(j_kv + 1) * LANES] = val.astype(BF16)
                    continue
                swapped = pltpu.roll(val, NSA_DH, 1)
                lower = lane < NSA_DH
                grouped = (val, swapped), (swapped, val)
                t0 = KV_TILE[j_kv]
                for g in range(NSA_GROUPS):
                    lo, hi = grouped[g]
                    if is_key:
                        fill = sel_tag if j_kv == 2 else jnp.zeros_like(val)
                        tiles = (jnp.where(lower, lo, fill),)
                    else:
                        tiles = (jnp.where(lower, 1.0, hi), jnp.where(lower, lo, 1.0))
                    for k, tile in enumerate(tiles):
                        kv_ref[g, :, (t0 + k) * LANES:(t0 + k + 1) * LANES] = tile.astype(BF16)
            else:
                gate_ref[:, col - C_GATE:col - C_GATE + LANES] = jax.nn.sigmoid(piece)


def _proj(seq_len, x2, posf, g_mix, inv_freq, gn_g, w_all):
    T = x2.shape[0]
    tm = PROJ_TM
    C = RET_CHUNK
    assert seq_len % tm == 0 and tm % C == 0
    row = lambda w: pl.BlockSpec((tm, w), lambda i: (i, 0))
    kv_w = KV_TILES * LANES
    out_shapes = [
        jax.ShapeDtypeStruct((T, RET_V_W), BF16),
        jax.ShapeDtypeStruct((T, Q_STACKS * NSA_Q_W), BF16),
        jax.ShapeDtypeStruct((T, 2 * LANES), BF16),
        jax.ShapeDtypeStruct((NSA_GROUPS, T, kv_w), BF16),
        jax.ShapeDtypeStruct((T, GATE_W), F32),
    ]
    tables = [pltpu.VMEM((RET_HEADS, C, w), F32) for w in (C, RET_DV, RET_DK)]
    return pl.pallas_call(
        functools.partial(_proj_kernel, seq_len),
        grid=(T // tm,),
        in_specs=[row(D_MODEL), row(1), _resident((1, D_MODEL)), _resident((1, LANES)), _resident((1, RET_V_W)),
                  _resident((D_MODEL, C_END))],
        out_specs=[row(RET_V_W), row(Q_STACKS * NSA_Q_W), row(2 * LANES),
                   pl.BlockSpec((NSA_GROUPS, tm, kv_w), lambda i: (0, i, 0)), row(GATE_W)],
        out_shape=out_shapes,
        scratch_shapes=[pltpu.VMEM((tm, RET_QK_W), BF16), pltpu.VMEM((tm, RET_QK_W), BF16),
                        pltpu.VMEM((tm, RET_V_W), BF16), pltpu.VMEM((tm, RET_V_W), BF16),
                        pltpu.VMEM((RET_HEADS, RET_DK, RET_DV), F32)] + tables,
        compiler_params=pltpu.CompilerParams(dimension_semantics=("arbitrary",),
                                             vmem_limit_bytes=VMEM_LIMIT),
        name="proj",
    )(x2, posf, g_mix, inv_freq, gn_g, w_all)


def _compress_kernel(k16_ref, v16_ref, w1k_ref, w2k_ref, pek_ref, w1v_ref, w2v_ref, pev_ref,
                     kc_ref, vc_ref):
    half = CMP_STRIDE * NSA_DH
    for x_ref, w1_ref, w2_ref, pe_ref, o_ref in ((k16_ref, w1k_ref, w2k_ref, pek_ref, kc_ref),
                                                 (v16_ref, w1v_ref, w2v_ref, pev_ref, vc_ref)):
        x = x_ref[0, 0]
        first = _dot(x, w1_ref[0:half, :])
        second = _dot(x, w1_ref[half:2 * half, :])
        pe_term = _dot(pe_ref[...], w1_ref[...])[0:1, :]
        hidden = first + pltpu.roll(second, second.shape[0] - 1, 0) + pe_term
        act = jax.nn.gelu(hidden).astype(BF16)
        both = _dot(act, w2_ref[...]).astype(BF16)
        n_cmp = both.shape[0]
        o_ref[0, 0, 0:n_cmp, :] = both[:, 0:LANES]
        o_ref[0, 0, n_cmp:2 * n_cmp, :] = both[:, LANES:2 * LANES]


def _compress(k16, v16, w1k, w2k, pek, w1v, w2v, pev):
    B, G, R, W = k16.shape
    blk = pl.BlockSpec((1, 1, R, W), lambda b, g: (b, g, 0, 0))
    oblk = pl.BlockSpec((1, 1, 2 * R, LANES), lambda b, g: (b, g, 0, 0))
    out = jax.ShapeDtypeStruct((B, G, 2 * R, LANES), BF16)
    wspecs = [_resident(w1k.shape), _resident(w2k.shape), _resident(pek.shape)]
    return pl.pallas_call(
        _compress_kernel,
        grid=(B, G),
        in_specs=[blk, blk] + wspecs + wspecs,
        out_specs=[oblk, oblk],
        out_shape=[out, out],
        compiler_params=pltpu.CompilerParams(dimension_semantics=("arbitrary", "arbitrary"),
                                             vmem_limit_bytes=VMEM_LIMIT),
        name="compress",
    )(k16, v16, w1k, w2k, pek, w1v, w2v, pev)


_RET_LOG_G = [math.log(1.0 - 2.0 ** (-5.0 - h)) for h in range(RET_HEADS)]


def _retention_tables(decay_ref, xi_ref, zeta_ref):
    C = RET_CHUNK
    r = lax.broadcasted_iota(jnp.int32, (C, C), 0).astype(F32)
    c = lax.broadcasted_iota(jnp.int32, (C, C), 1).astype(F32)
    diff = r - c
    r_out = lax.broadcasted_iota(jnp.int32, (C, RET_DV), 0).astype(F32)
    r_key = lax.broadcasted_iota(jnp.int32, (C, RET_DK), 0).astype(F32)
    for h in range(RET_HEADS):
        lg = _RET_LOG_G[h]
        decay_ref[h] = jnp.where(diff >= 0, jnp.exp(jnp.maximum(diff, 0.0) * lg), 0.0)
        xi_ref[h] = jnp.exp((r_out + 1.0) * lg)
        zeta_ref[h] = jnp.exp((C - 1.0 - r_key) * lg)


def _retention_heads(heads, new_sequence, q_ref, k_ref, v_ref, g_ref, gn_ref, y_ref, state_ref, decay_ref, xi_ref,
                     zeta_ref):
    C = RET_CHUNK
    n_chunks = q_ref.shape[0] // C
    for h in heads:
        qs = slice(h * RET_DK, (h + 1) * RET_DK)
        vs = slice(h * RET_DV, (h + 1) * RET_DV)
        xi = xi_ref[h]
        state = jnp.where(new_sequence, 0.0, state_ref[h])
        for c in range(n_chunks):
            tok = slice(c * C, (c + 1) * C)
            qh = q_ref[tok, qs]
            kh = k_ref[tok, qs]
            vh = v_ref[tok, vs]
            inner = (_dot_tb(qh, kh) * decay_ref[h]).astype(BF16)
            o = _dot(inner, vh) + _dot(qh, state.astype(BF16)) * xi
            kz = (kh.astype(F32) * zeta_ref[h]).astype(BF16)
            state = math.exp(C * _RET_LOG_G[h]) * state + _dot_ta(kz, vh)
            mu = jnp.mean(o, axis=-1, keepdims=True)
            d = o - mu
            var = jnp.mean(d * d, axis=-1, keepdims=True)
            y = d * lax.rsqrt(var + RMS_EPS) * gn_ref[:, vs]
            g = g_ref[tok, vs].astype(F32)
            y_ref[tok, vs] = (y * (g * jax.nn.sigmoid(g))).astype(BF16)
        state_ref[h] = state


NOTSEL_LANE0 = NSA_DH


def _nsa_kernel(q_ref, gate_ref, ksel_ref, vsel_e_ref, vsel_o_ref, kwin_ref, vwin_e_ref, vwin_o_ref,
                kc_ref, vc_ref, wbias_ref, cbias_ref, o_ref,
                acc_e, acc_o, m_e, m_o, out_acc, s_next, gate_tiles):
    TQ, KT, WK = NSA_TQ, NSA_KT, NSA_WK
    HP = HEAD_PAIRS
    n_blk = ksel_ref.shape[2] // SEL_BLOCK
    qt = pl.program_id(2)
    q0 = qt * TQ
    q_plain = tuple(jnp.concatenate([q_ref[0, :, (x * HP + hp) * LANES:(x * HP + hp + 1) * LANES]
                                     for hp in range(HP)], axis=0) for x in range(2))
    t_col = q0 + lax.broadcasted_iota(jnp.int32, (TQ, 1), 0)
    lane = lax.broadcasted_iota(jnp.int32, (TQ, LANES), 1)
    lower_half = lane < NSA_DH
    rows = [slice(hp * TQ, (hp + 1) * TQ) for hp in range(HP)]
    gates = gate_ref[0]

    for branch in range(3):
        for hp in range(HP):
            c = branch * NSA_HPG + hp * 2
            gate_tiles[branch * HP + hp] = jnp.take_along_axis(gates, jnp.where(lower_half, c + 1, c), axis=1)

    def emit(branch, acc_pair):
        for hp in range(HP):
            a_e = acc_pair[0][rows[hp]]
            a_o = acc_pair[1][rows[hp]]
            weight = gate_tiles[branch * HP + hp] / pltpu.roll(jnp.where(lower_half, a_e, a_o), NSA_DH, 1)
            out_acc[rows[hp]] += jnp.where(lower_half, a_o, a_e) * weight

    c_bias = jnp.where(lane * CMP_STRIDE + (CMP_LEN - 1) <= t_col, 0.0, MASK_BIAS)
    q_pairs = jnp.concatenate([q_ref[0, :, (2 * HP + hp) * LANES:(2 * HP + hp + 1) * LANES]
                               for hp in range(HP)], axis=0)
    s = _dot_tb(q_pairs, kc_ref[0, 0])
    p_sum = jnp.zeros((TQ, LANES), F32)
    ps = []
    for hp in range(HP):
        halves = []
        for x in range(2):
            sh = s[rows[hp], x * LANES:(x + 1) * LANES] + c_bias
            m = jnp.maximum(jnp.max(sh, axis=-1, keepdims=True), MAX_FLOOR)
            e = jnp.exp2(sh - m)
            l = jnp.sum(e, axis=-1, keepdims=True)
            p = e * jnp.where(l > 0, 1.0 / l, 0.0)
            p_sum = p_sum + p
            halves.append(p.astype(BF16))
        ps.append(jnp.concatenate(halves, axis=1))
    cmp_out = _dot(jnp.concatenate(ps, axis=0), vc_ref[0, 0])
    for hp in range(HP):
        out_acc[rows[hp]] = cmp_out[rows[hp]] * gate_tiles[hp]

    w0 = pl.multiple_of(jnp.maximum(q0 - WINDOW, 0), TQ)
    w_bias = wbias_ref[jnp.minimum(qt, WINDOW // TQ)]
    kk = kwin_ref[0, 0, pl.ds(w0, WK), :]
    win_acc = []
    for qx, v_ref in zip(q_plain, (vwin_e_ref, vwin_o_ref)):
        s = _dot_tb(qx, kk)
        es = []
        for hp in range(HP):
            sh = s[rows[hp]] + w_bias
            es.append(jnp.exp2(sh - jnp.max(sh, axis=-1, keepdims=True)).astype(BF16))
        win_acc.append(_dot(jnp.concatenate(es, axis=0), v_ref[0, 0, pl.ds(w0, WK), :]))
    emit(2, win_acc)

    ni = lax.broadcasted_iota(jnp.int32, (LANES, LANES), 0)
    ci = lax.broadcasted_iota(jnp.int32, (LANES, LANES), 1)
    overlap_t = ((ci * CMP_STRIDE < ni * SEL_BLOCK + SEL_BLOCK)
                 & (ci * CMP_STRIDE + CMP_LEN - 1 >= ni * SEL_BLOCK) & (ni < n_blk))
    overlap_t = jnp.where(overlap_t, 1.0, 0.0).astype(BF16)
    p_hi = p_sum.astype(BF16)
    p_lo = (p_sum - p_hi.astype(F32)).astype(BF16)
    imp_t = (_dot_tb(overlap_t, p_hi) + _dot_tb(overlap_t, p_lo))[0:n_blk]
    blk_id = lax.broadcasted_iota(jnp.int32, (n_blk, TQ), 0)
    cur = (q0 + lax.broadcasted_iota(jnp.int32, (n_blk, TQ), 1)) >> 6
    forced = (blk_id == 0) | (blk_id == cur) | (blk_id == cur - 1)
    score = jnp.where(forced, FORCE_SCORE, jnp.where(blk_id <= cur, imp_t, -1.0))
    rank = jnp.zeros((n_blk, TQ), jnp.int32)
    for mblk in range(n_blk):
        other = score[mblk:mblk + 1, :]
        ahead = (other > score) | ((other == score) & (blk_id > mblk))
        rank = rank + jnp.where(ahead, 1, 0)
    not_sel_t = jnp.where(rank < SEL_TOPN, 0.0, 1.0)
    padded = jnp.concatenate([jnp.zeros((NOTSEL_LANE0, TQ), F32), not_sel_t,
                              jnp.zeros((LANES - NOTSEL_LANE0 - n_blk, TQ), F32)], axis=0)
    not_sel = jnp.concatenate([padded.T.astype(BF16)] * HP, axis=0)
    q_aug = tuple(qx + not_sel for qx in q_plain)

    for ref in (acc_e, acc_o):
        ref[...] = jnp.zeros_like(ref)
    for ref in (m_e, m_o):
        ref[...] = jnp.full_like(ref, MASK_BIAS)

    last_k0 = ksel_ref.shape[2] - KT

    def even_scores(kt):
        k0 = pl.multiple_of(jnp.minimum(kt * KT, last_k0), KT)
        return _dot_tb(q_aug[0], ksel_ref[0, 0, pl.ds(k0, KT), :])

    def softmax_pv(get_scores, bias, vv, acc, m_ref):
        es, alphas = [], []
        for hp in range(HP):
            sh = get_scores(hp) + bias
            tiles = [sh[:, j * LANES:(j + 1) * LANES] for j in range(bias.shape[1] // LANES)]
            m_old = m_ref[rows[hp]]
            m_new = jnp.maximum(m_old, jnp.max(functools.reduce(jnp.maximum, tiles), axis=-1, keepdims=True))
            m_ref[rows[hp]] = m_new
            alphas.append(jnp.exp2(m_old - m_new))
            es.append(jnp.concatenate([jnp.exp2(t - m_new).astype(BF16) for t in tiles], axis=1))
        pv = _dot(jnp.concatenate(es, axis=0), vv)
        for hp in range(HP):
            acc[rows[hp]] = alphas[hp] * acc[rows[hp]] + pv[rows[hp]]

    s_next[...] = even_scores(0)

    def sel_tile(kt):
        k0 = pl.multiple_of(kt * KT, KT)
        bias = cbias_ref[jnp.minimum(qt - kt * (KT // TQ), KT // TQ)]
        s_odd = _dot_tb(q_aug[1], ksel_ref[0, 0, pl.ds(k0, KT), :])
        softmax_pv(lambda hp: s_next[rows[hp], :], bias, vsel_e_ref[0, 0, pl.ds(k0, KT), :], acc_e, m_e)
        s_next[...] = even_scores(kt + 1)
        softmax_pv(lambda hp: s_odd[rows[hp]], bias, vsel_o_ref[0, 0, pl.ds(k0, KT), :], acc_o, m_o)

    n_whole = (q0 + TQ) // KT

    def tile_pair(i, carry):
        sel_tile(2 * i)
        sel_tile(2 * i + 1)
        return carry

    lax.fori_loop(0, n_whole // 2, tile_pair, 0)

    @pl.when(n_whole % 2 == 1)
    def _odd_tile():
        sel_tile(n_whole - 1)

    @pl.when(n_whole * KT < q0 + TQ)
    def _diagonal_remainder():
        k_rem = pl.multiple_of(q0, TQ)
        kk = ksel_ref[0, 0, pl.ds(k_rem, TQ), :]
        bias = cbias_ref[0][:, 0:TQ]
        for qx, v_ref, acc, m_ref in ((q_aug[0], vsel_e_ref, acc_e, m_e), (q_aug[1], vsel_o_ref, acc_o, m_o)):
            s = _dot_tb(qx, kk)
            softmax_pv(lambda hp: s[rows[hp]], bias, v_ref[0, 0, pl.ds(k_rem, TQ), :], acc, m_ref)

    emit(1, (acc_e, acc_o))

    o_ref[0] = jnp.concatenate([out_acc[rows[hp]] for hp in range(HP)], axis=1).astype(BF16)


def _nsa(nq, gates, kv, kc, vc):
    B, S, _ = nq.shape
    G = NSA_GROUPS
    TQ = NSA_TQ
    assert S // SEL_BLOCK <= LANES - NOTSEL_LANE0 and S % NSA_KT == 0 and S >= NSA_WK
    assert NSA_KT % TQ == 0 and WINDOW % TQ == 0
    rows = HEAD_PAIRS * TQ
    i = jnp.arange(TQ, dtype=jnp.int32)[None, :, None]
    off = lambda n: jnp.arange(n + 1, dtype=jnp.int32)[:, None, None] * TQ
    j = jnp.arange(NSA_WK, dtype=jnp.int32)[None, None, :]
    t_rel = off(WINDOW // TQ) + i
    win_bias = jnp.where((j <= t_rel) & (j > t_rel - WINDOW), 0.0, MASK_BIAS).astype(F32)
    j = jnp.arange(NSA_KT, dtype=jnp.int32)[None, None, :]
    causal_bias = jnp.where(j <= off(NSA_KT // TQ) + i, 0.0, MASK_BIAS).astype(F32)
    qblk = pl.BlockSpec((1, TQ, Q_STACKS * NSA_GQ_W), lambda b, g, t: (b, t, g))
    oblk = pl.BlockSpec((1, TQ, NSA_GQ_W), lambda b, g, t: (b, t, g))
    gblk = pl.BlockSpec((1, TQ, LANES), lambda b, g, t: (b, t, g))
    kvblk = lambda j: pl.BlockSpec((1, 1, S, LANES), lambda b, g, t: (g, b, 0, j))
    cblk = pl.BlockSpec((1, 1, 2 * LANES, LANES), lambda b, g, t: (b, g, 0, 0))
    wide = pltpu.VMEM((rows, LANES), F32)
    return pl.pallas_call(
        _nsa_kernel,
        grid=(B, G, S // TQ),
        in_specs=([qblk, gblk] + [kvblk(j) for j in range(KV_TILES)] + [cblk, cblk]
                  + [_resident(win_bias.shape), _resident(causal_bias.shape)]),
        out_specs=oblk,
        out_shape=jax.ShapeDtypeStruct((B, S, NSA_Q_W), BF16),
        scratch_shapes=[wide] * 5 + [pltpu.VMEM((rows, NSA_KT), F32),
                                     pltpu.VMEM((3 * NSA_HPG // 2, TQ, LANES), F32)],
        compiler_params=pltpu.CompilerParams(dimension_semantics=("arbitrary",) * 3,
                                             vmem_limit_bytes=VMEM_LIMIT),
        name="nsa",
    )(nq, gates, *([kv] * KV_TILES), kc, vc, win_bias, causal_bias)


def _tail_kernel(final, x_ref, yr_ref, yn_ref, p_ref, gmix_ref, gmlp_ref, gple_ref, gfin_ref,
                 wmg_ref, wro_ref, wno_ref, wout_ref, wup_ref, wdn_ref, wpg_ref, wpp_ref, o_ref):
    x = x_ref[...]
    h = _rms(x, gmix_ref[...]).astype(BF16)
    o_ret = _dot(yr_ref[...], wro_ref[...])
    o_nsa = _dot(yn_ref[...], wno_ref[...])
    g_ret = jax.nn.sigmoid(_dot(h, wmg_ref[:, 0:D_MODEL]))
    g_nsa = jax.nn.sigmoid(_dot(h, wmg_ref[:, D_MODEL:2 * D_MODEL]))
    mix = (g_ret * o_ret + g_nsa * o_nsa).astype(BF16)
    x = x + _dot(mix, wout_ref[...])
    h2 = _rms(x, gmlp_ref[...]).astype(BF16)
    mlp = jnp.zeros_like(x)
    step = 1024
    for c0 in range(0, MLP_HIDDEN, step):
        up = jnp.maximum(_dot(h2, wup_ref[:, c0:c0 + step]), 0.0)
        mlp = mlp + _dot((up * up).astype(BF16), wdn_ref[c0:c0 + step, :])
    x = x + mlp
    h3 = _rms(x, gple_ref[...]).astype(BF16)
    ple_gate = jax.nn.sigmoid(_dot(h3, wpg_ref[...]))
    x = x + _dot(p_ref[...].astype(BF16), wpp_ref[...]) * ple_gate
    if final:
        x = _rms(x, gfin_ref[...])
    o_ref[...] = x


def _tail(final, x2, y_ret, y_nsa, p2, g_mix, g_mlp, g_ple, g_fin, w_mg, w_ro, w_no, w_out, w_up, w_dn,
          w_pg, w_pp):
    T = x2.shape[0]
    tm = TAIL_TM
    row = lambda w: pl.BlockSpec((tm, w), lambda i: (i, 0))
    gains = [_resident((1, D_MODEL))] * 4
    weights = [_resident(w.shape) for w in (w_mg, w_ro, w_no, w_out, w_up, w_dn, w_pg, w_pp)]
    return pl.pallas_call(
        functools.partial(_tail_kernel, final),
        grid=(T // tm,),
        in_specs=[row(D_MODEL), row(RET_V_W), row(NSA_Q_W), row(PLE_DIM)] + gains + weights,
        out_specs=row(D_MODEL),
        out_shape=jax.ShapeDtypeStruct((T, D_MODEL), F32),
        compiler_params=pltpu.CompilerParams(dimension_semantics=("arbitrary",),
                                             vmem_limit_bytes=VMEM_LIMIT),
        name="tail",
    )(x2, y_ret, y_nsa, p2, g_mix, g_mlp, g_ple, g_fin, w_mg, w_ro, w_no, w_out, w_up, w_dn, w_pg, w_pp)


def _pack_w_in(w):
    gate = w[:, C_GATE:C_GATE + 3 * NSA_HEADS]
    parts = [w[:, :C_GATE]]
    for g in range(NSA_GROUPS):
        cols = [j * NSA_HEADS + g * NSA_HPG + r for j in range(3) for r in range(NSA_HPG)]
        parts.append(jnp.pad(gate[:, jnp.array(cols)], ((0, 0), (0, LANES - len(cols)))))
    return jnp.concatenate(parts, axis=1).astype(BF16)


def kernel(x, p, positions, norm_mix_g, w_in, ret_gn_g, w_ret_o, cmp_pe_k, cmp_k_w1, cmp_k_w2, cmp_pe_v, cmp_v_w1, cmp_v_w2, w_nsa_o, w_merge_gate, w_out, norm_mlp_g, w_mlp_up, w_mlp_down, norm_ple_g, w_ple_gate, w_ple_proj, norm_final_g):
    B, S, D = x.shape
    depth = p.shape[0]
    T = B * S
    G = NSA_GROUPS
    bf = lambda a: a.astype(BF16)
    row = lambda a: a.reshape(1, -1)
    posf = positions.reshape(T, 1).astype(F32)
    inv_r = ROPE_THETA ** (-jnp.arange(0, RET_DK, 2, dtype=F32) / RET_DK)
    inv_n = ROPE_THETA ** (-jnp.arange(0, NSA_DH, 2, dtype=F32) / NSA_DH)
    inv_freq = jnp.concatenate([inv_r, inv_n, jnp.zeros((LANES - inv_r.shape[0] - inv_n.shape[0],), F32)])
    inv_freq = inv_freq.reshape(1, LANES)
    n_rows = S // CMP_STRIDE

    def strides(t):
        t = t.reshape(B, S, G, NSA_DH).transpose(0, 2, 1, 3)
        return t.reshape(B, G, n_rows, CMP_STRIDE * NSA_DH)

    def pe_rows(pe):
        return jnp.broadcast_to(bf(pe).reshape(1, -1), (8, CMP_LEN * NSA_DH))

    def swap_pairs(w):
        return w.reshape(NSA_HEADS // 2, 2, NSA_DH, -1)[:, ::-1].reshape(w.shape)

    def widen(w, outer):
        z = jnp.zeros_like(w)
        return bf(jnp.concatenate([w, z, z, w] if outer else [z, w, w, z], axis=1))

    x2 = x.reshape(T, D)
    for i in range(depth):
        y_ret, nq, cmp_kv, kv, gates = _proj(S, x2, posf, row(norm_mix_g[i]), inv_freq, row(ret_gn_g[i]),
                                              _pack_w_in(w_in[i]))
        kc, vc = _compress(strides(cmp_kv[:, :LANES]), strides(cmp_kv[:, LANES:]),
                           bf(cmp_k_w1[i]), widen(cmp_k_w2[i], True), pe_rows(cmp_pe_k[i]),
                           bf(cmp_v_w1[i]), widen(cmp_v_w2[i], False), pe_rows(cmp_pe_v[i]))
        sh3 = lambda a: a.reshape(B, S, a.shape[-1])
        y_nsa = _nsa(sh3(nq), sh3(gates), kv.reshape(G, B, S, kv.shape[-1]), kc, vc)
        x2 = _tail(i == depth - 1, x2, y_ret, y_nsa.reshape(T, NSA_Q_W),
                   p[i].reshape(T, PLE_DIM), row(norm_mix_g[i]), row(norm_mlp_g[i]), row(norm_ple_g[i]),
                   row(norm_final_g), bf(w_merge_gate[i]), bf(w_ret_o[i]), bf(swap_pairs(w_nsa_o[i])), bf(w_out[i]),
                   bf(w_mlp_up[i]), bf(w_mlp_down[i]), bf(w_ple_gate[i]), bf(w_ple_proj[i]))
    return x2.reshape(B, S, D)
```

```python
import functools
import math

import jax
import jax.numpy as jnp
from jax import lax
from jax.experimental import pallas as pl
from jax.experimental.pallas import tpu as pltpu

F32 = jnp.float32
BF16 = jnp.bfloat16

D_MODEL = 1024
PLE_DIM = 256
RMS_EPS = 1e-6
ROPE_THETA = 10000.0
RET_HEADS = 8
RET_DK = 128
RET_DV = 256
RET_CHUNK = 256
RET_QK_W = RET_HEADS * RET_DK
RET_V_W = RET_HEADS * RET_DV
NSA_HEADS = 16
NSA_GROUPS = 2
NSA_HPG = 8
NSA_DH = 64
NSA_Q_W = NSA_HEADS * NSA_DH
NSA_GQ_W = NSA_HPG * NSA_DH
HEAD_PAIRS = NSA_HPG // 2
Q_STACKS = 3
CMP_LEN = 32
CMP_STRIDE = 16
CMP_HIDDEN = 256
SEL_BLOCK = 64
SEL_TOPN = 8
WINDOW = 512
FORCE_SCORE = 1e6
MLP_HIDDEN = 4 * D_MODEL

LANES = 128
MASK_BIAS = -1e30
MAX_FLOOR = -1e29
LOG2E = math.log2(math.e)
VMEM_LIMIT = 56 * 1024 * 1024

PROJ_TM = 256
TAIL_TM = 512
NSA_TQ = 256
NSA_KT = 512
NSA_WK = WINDOW + NSA_TQ

C_RQ, C_RK, C_RV, C_RG = 0, 1024, 2048, 4096
C_NQ, C_KV, C_GATE, C_END = 6144, 7168, 7936, 8192
GATE_W = NSA_GROUPS * LANES
KV_TILE = {2: 0, 3: 1, 4: 3, 5: 4}
KV_TILES = 6


def _resident(shape):
    nd = len(shape)
    return pl.BlockSpec(shape, lambda *_: (0,) * nd, pipeline_mode=pl.Buffered(1))


def _rms(x, g):
    return x * lax.rsqrt(jnp.mean(x * x, axis=-1, keepdims=True) + RMS_EPS) * g


def _dot(a, b):
    return jnp.dot(a, b, preferred_element_type=F32)


def _dot_tb(a, b):
    return lax.dot_general(a, b, (((1,), (1,)), ((), ())), preferred_element_type=F32)


def _dot_ta(a, b):
    return lax.dot_general(a, b, (((0,), (0,)), ((), ())), preferred_element_type=F32)


def _proj_kernel(seq_len, x_ref, pos_ref, g_ref, inv_ref, gn_ref, w_ref,
                 yret_ref, nq_ref, cmp_ref, kv_ref, gate_ref,
                 rq_ref, rk_ref, rv_ref, rg_ref, state_ref, decay_ref, xi_ref, zeta_ref):
    tm = x_ref.shape[0]
    first_row = pl.program_id(0) * tm

    @pl.when(first_row == 0)
    def _tables():
        _retention_tables(decay_ref, xi_ref, zeta_ref)

    h = _rms(x_ref[...], g_ref[...]).astype(BF16)
    pos = pos_ref[...]
    lane = lax.broadcasted_iota(jnp.int32, (tm, LANES), 1)
    ang = pos * inv_ref[...]
    cos_a = jnp.cos(ang)
    sin_a = jnp.sin(ang)
    cos_r = jnp.where(lane < 64, cos_a, pltpu.roll(cos_a, 64, 1))
    sin_r = jnp.where(lane < 64, -sin_a, pltpu.roll(sin_a, 64, 1))

    def tile_nsa(t):
        return jnp.where(lane < 32, pltpu.roll(t, 64, 1),
                         jnp.where(lane < 64, pltpu.roll(t, 96, 1), jnp.where(lane < 96, t, pltpu.roll(t, 32, 1))))

    low = (lane & 32) == 0
    cos_n = tile_nsa(cos_a)
    sin_n = tile_nsa(sin_a)
    sin_n = jnp.where(low, -sin_n, sin_n)

    def rope_r(y):
        return y * cos_r + pltpu.roll(y, 64, 1) * sin_r

    def rope_n(y):
        partner = jnp.where(low, pltpu.roll(y, 96, 1), pltpu.roll(y, 32, 1))
        return y * cos_n + partner * sin_n

    seq_pos = first_row % seq_len + lax.broadcasted_iota(jnp.int32, (tm, LANES), 0)
    sel_tag = jnp.where(lane - NSA_DH == seq_pos // SEL_BLOCK, MASK_BIAS, 0.0)

    k_scale = RET_DK ** -0.5
    q_scale = NSA_DH ** -0.5 * LOG2E
    chunk = 512
    heads_per_chunk = chunk // RET_DV
    plan = []
    for v_chunk in range(RET_V_W // chunk):
        plan.append(C_RV + v_chunk * chunk)
        if (v_chunk * heads_per_chunk) % (chunk // RET_DK) == 0:
            qk_chunk = v_chunk * heads_per_chunk * RET_DK
            plan += [C_RQ + qk_chunk, C_RK + qk_chunk]
        plan.append(C_RG + v_chunk * chunk)
        plan.append(tuple(range(v_chunk * heads_per_chunk, (v_chunk + 1) * heads_per_chunk)))
    plan += list(range(C_NQ, C_END, chunk))
    for c0 in plan:
        if isinstance(c0, tuple):
            _retention_heads(c0, first_row % seq_len == 0, rq_ref, rk_ref, rv_ref, rg_ref, gn_ref, yret_ref,
                             state_ref, decay_ref, xi_ref, zeta_ref)
            continue
        y = _dot(h, w_ref[:, c0:c0 + chunk])
        for j in range(chunk // LANES):
            col = c0 + j * LANES
            piece = y[:, j * LANES:(j + 1) * LANES]
            if col < C_RK:
                rq_ref[:, col - C_RQ:col - C_RQ + LANES] = rope_r(piece).astype(BF16)
            elif col < C_RV:
                rk_ref[:, col - C_RK:col - C_RK + LANES] = (rope_r(piece) * k_scale).astype(BF16)
            elif col < C_RG:
                rv_ref[:, col - C_RV:col - C_RV + LANES] = piece.astype(BF16)
            elif col < C_NQ:
                rg_ref[:, col - C_RG:col - C_RG + LANES] = piece.astype(BF16)
            elif col < C_KV:
                val = rope_n(piece) * q_scale
                pair = (col - C_NQ) // LANES
                base = (pair // HEAD_PAIRS) * Q_STACKS * NSA_GQ_W + (pair % HEAD_PAIRS) * LANES
                zeros = jnp.zeros_like(val)
                nq_ref[:, base:base + LANES] = jnp.where(lane < NSA_DH, val, zeros).astype(BF16)
                nq_ref[:, base + NSA_GQ_W:base + NSA_GQ_W + LANES] = jnp.where(
                    lane < NSA_DH, pltpu.roll(val, NSA_DH, 1), zeros).astype(BF16)
                nq_ref[:, base + 2 * NSA_GQ_W:base + 2 * NSA_GQ_W + LANES] = val.astype(BF16)
            elif col < C_GATE:
                j_kv = (col - C_KV) // LANES
                is_key = j_kv % 2 == 0
                val = rope_n(piece) if is_key else piece
                if j_kv < 2:
                    cmp_ref[:, j_kv * LANES:(j_kv + 1) * LANES] = val.astype(BF16)
                    continue
                swapped = pltpu.roll(val, NSA_DH, 1)
                lower = lane < NSA_DH
                grouped = (val, swapped), (swapped, val)
                t0 = KV_TILE[j_kv]
                for g in range(NSA_GROUPS):
                    lo, hi = grouped[g]
                    if is_key:
                        fill = sel_tag if j_kv == 2 else jnp.zeros_like(val)
                        tiles = (jnp.where(lower, lo, fill),)
                    else:
                        tiles = (jnp.where(lower, 1.0, hi), jnp.where(lower, lo, 1.0))
                    for k, tile in enumerate(tiles):
                        kv_ref[g, :, (t0 + k) * LANES:(t0 + k + 1) * LANES] = tile.astype(BF16)
            else:
                gate_ref[:, col - C_GATE:col - C_GATE + LANES] = jax.nn.sigmoid(piece)


def _proj(seq_len, x2, posf, g_mix, inv_freq, gn_g, w_all):
    T = x2.shape[0]
    tm = PROJ_TM
    C = RET_CHUNK
    assert seq_len % tm == 0 and tm % C == 0
    row = lambda w: pl.BlockSpec((tm, w), lambda i: (i, 0))
    kv_w = KV_TILES * LANES
    out_shapes = [
        jax.ShapeDtypeStruct((T, RET_V_W), BF16),
        jax.ShapeDtypeStruct((T, Q_STACKS * NSA_Q_W), BF16),
        jax.ShapeDtypeStruct((T, 2 * LANES), BF16),
        jax.ShapeDtypeStruct((NSA_GROUPS, T, kv_w), BF16),
        jax.ShapeDtypeStruct((T, GATE_W), F32),
    ]
    tables = [pltpu.VMEM((RET_HEADS, C, w), F32) for w in (C, RET_DV, RET_DK)]
    return pl.pallas_call(
        functools.partial(_proj_kernel, seq_len),
        grid=(T // tm,),
        in_specs=[row(D_MODEL), row(1), _resident((1, D_MODEL)), _resident((1, LANES)), _resident((1, RET_V_W)),
                  _resident((D_MODEL, C_END))],
        out_specs=[row(RET_V_W), row(Q_STACKS * NSA_Q_W), row(2 * LANES),
                   pl.BlockSpec((NSA_GROUPS, tm, kv_w), lambda i: (0, i, 0)), row(GATE_W)],
        out_shape=out_shapes,
        scratch_shapes=[pltpu.VMEM((tm, RET_QK_W), BF16), pltpu.VMEM((tm, RET_QK_W), BF16),
                        pltpu.VMEM((tm, RET_V_W), BF16), pltpu.VMEM((tm, RET_V_W), BF16),
                        pltpu.VMEM((RET_HEADS, RET_DK, RET_DV), F32)] + tables,
        compiler_params=pltpu.CompilerParams(dimension_semantics=("arbitrary",),
                                             vmem_limit_bytes=VMEM_LIMIT),
        name="proj",
    )(x2, posf, g_mix, inv_freq, gn_g, w_all)


def _compress_kernel(k16_ref, v16_ref, w1k_ref, w2k_ref, pek_ref, w1v_ref, w2v_ref, pev_ref,
                     kc_ref, vc_ref):
    half = CMP_STRIDE * NSA_DH
    for x_ref, w1_ref, w2_ref, pe_ref, o_ref in ((k16_ref, w1k_ref, w2k_ref, pek_ref, kc_ref),
                                                 (v16_ref, w1v_ref, w2v_ref, pev_ref, vc_ref)):
        x = x_ref[0, 0]
        first = _dot(x, w1_ref[0:half, :])
        second = _dot(x, w1_ref[half:2 * half, :])
        pe_term = _dot(pe_ref[...], w1_ref[...])[0:1, :]
        hidden = first + pltpu.roll(second, second.shape[0] - 1, 0) + pe_term
        act = jax.nn.gelu(hidden).astype(BF16)
        both = _dot(act, w2_ref[...]).astype(BF16)
        n_cmp = both.shape[0]
        o_ref[0, 0, 0:n_cmp, :] = both[:, 0:LANES]
        o_ref[0, 0, n_cmp:2 * n_cmp, :] = both[:, LANES:2 * LANES]


def _compress(k16, v16, w1k, w2k, pek, w1v, w2v, pev):
    B, G, R, W = k16.shape
    blk = pl.BlockSpec((1, 1, R, W), lambda b, g: (b, g, 0, 0))
    oblk = pl.BlockSpec((1, 1, 2 * R, LANES), lambda b, g: (b, g, 0, 0))
    out = jax.ShapeDtypeStruct((B, G, 2 * R, LANES), BF16)
    wspecs = [_resident(w1k.shape), _resident(w2k.shape), _resident(pek.shape)]
    return pl.pallas_call(
        _compress_kernel,
        grid=(B, G),
        in_specs=[blk, blk] + wspecs + wspecs,
        out_specs=[oblk, oblk],
        out_shape=[out, out],
        compiler_params=pltpu.CompilerParams(dimension_semantics=("arbitrary", "arbitrary"),
                                             vmem_limit_bytes=VMEM_LIMIT),
        name="compress",
    )(k16, v16, w1k, w2k, pek, w1v, w2v, pev)


_RET_LOG_G = [math.log(1.0 - 2.0 ** (-5.0 - h)) for h in range(RET_HEADS)]


def _retention_tables(decay_ref, xi_ref, zeta_ref):
    C = RET_CHUNK
    r = lax.broadcasted_iota(jnp.int32, (C, C), 0).astype(F32)
    c = lax.broadcasted_iota(jnp.int32, (C, C), 1).astype(F32)
    diff = r - c
    r_out = lax.broadcasted_iota(jnp.int32, (C, RET_DV), 0).astype(F32)
    r_key = lax.broadcasted_iota(jnp.int32, (C, RET_DK), 0).astype(F32)
    for h in range(RET_HEADS):
        lg = _RET_LOG_G[h]
        decay_ref[h] = jnp.where(diff >= 0, jnp.exp(jnp.maximum(diff, 0.0) * lg), 0.0)
        xi_ref[h] = jnp.exp((r_out + 1.0) * lg)
        zeta_ref[h] = jnp.exp((C - 1.0 - r_key) * lg)


def _retention_heads(heads, new_sequence, q_ref, k_ref, v_ref, g_ref, gn_ref, y_ref, state_ref, decay_ref, xi_ref,
                     zeta_ref):
    C = RET_CHUNK
    n_chunks = q_ref.shape[0] // C
    for h in heads:
        qs = slice(h * RET_DK, (h + 1) * RET_DK)
        vs = slice(h * RET_DV, (h + 1) * RET_DV)
        xi = xi_ref[h]
        state = jnp.where(new_sequence, 0.0, state_ref[h])
        for c in range(n_chunks):
            tok = slice(c * C, (c + 1) * C)
            qh = q_ref[tok, qs]
            kh = k_ref[tok, qs]
            vh = v_ref[tok, vs]
            inner = (_dot_tb(qh, kh) * decay_ref[h]).astype(BF16)
            o = _dot(inner, vh) + _dot(qh, state.astype(BF16)) * xi
            kz = (kh.astype(F32) * zeta_ref[h]).astype(BF16)
            state = math.exp(C * _RET_LOG_G[h]) * state + _dot_ta(kz, vh)
            mu = jnp.mean(o, axis=-1, keepdims=True)
            d = o - mu
            var = jnp.mean(d * d, axis=-1, keepdims=True)
            y = d * lax.rsqrt(var + RMS_EPS) * gn_ref[:, vs]
            g = g_ref[tok, vs].astype(F32)
            y_ref[tok, vs] = (y * (g * jax.nn.sigmoid(g))).astype(BF16)
        state_ref[h] = state


NOTSEL_LANE0 = NSA_DH


def _nsa_kernel(q_ref, gate_ref, ksel_ref, vsel_e_ref, vsel_o_ref, kwin_ref, vwin_e_ref, vwin_o_ref,
                kc_ref, vc_ref, wbias_ref, cbias_ref, o_ref,
                acc_e, acc_o, m_e, m_o, out_acc, s_next, gate_tiles):
    TQ, KT, WK = NSA_TQ, NSA_KT, NSA_WK
    HP = HEAD_PAIRS
    n_blk = ksel_ref.shape[2] // SEL_BLOCK
    qt = pl.program_id(2)
    q0 = qt * TQ
    q_plain = tuple(jnp.concatenate([q_ref[0, :, (x * HP + hp) * LANES:(x * HP + hp + 1) * LANES]
                                     for hp in range(HP)], axis=0) for x in range(2))
    t_col = q0 + lax.broadcasted_iota(jnp.int32, (TQ, 1), 0)
    lane = lax.broadcasted_iota(jnp.int32, (TQ, LANES), 1)
    lower_half = lane < NSA_DH
    rows = [slice(hp * TQ, (hp + 1) * TQ) for hp in range(HP)]
    gates = gate_ref[0]

    for branch in range(3):
        for hp in range(HP):
            c = branch * NSA_HPG + hp * 2
            gate_tiles[branch * HP + hp] = jnp.take_along_axis(gates, jnp.where(lower_half, c + 1, c), axis=1)

    def emit(branch, acc_pair):
        for hp in range(HP):
            a_e = acc_pair[0][rows[hp]]
            a_o = acc_pair[1][rows[hp]]
            weight = gate_tiles[branch * HP + hp] / pltpu.roll(jnp.where(lower_half, a_e, a_o), NSA_DH, 1)
            out_acc[rows[hp]] += jnp.where(lower_half, a_o, a_e) * weight

    c_bias = jnp.where(lane * CMP_STRIDE + (CMP_LEN - 1) <= t_col, 0.0, MASK_BIAS)
    q_pairs = jnp.concatenate([q_ref[0, :, (2 * HP + hp) * LANES:(2 * HP + hp + 1) * LANES]
                               for hp in range(HP)], axis=0)
    s = _dot_tb(q_pairs, kc_ref[0, 0])
    p_sum = jnp.zeros((TQ, LANES), F32)
    ps = []
    for hp in range(HP):
        halves = []
        for x in range(2):
            sh = s[rows[hp], x * LANES:(x + 1) * LANES] + c_bias
            m = jnp.maximum(jnp.max(sh, axis=-1, keepdims=True), MAX_FLOOR)
            e = jnp.exp2(sh - m)
            l = jnp.sum(e, axis=-1, keepdims=True)
            p = e * jnp.where(l > 0, 1.0 / l, 0.0)
            p_sum = p_sum + p
            halves.append(p.astype(BF16))
        ps.append(jnp.concatenate(halves, axis=1))
    cmp_out = _dot(jnp.concatenate(ps, axis=0), vc_ref[0, 0])
    for hp in range(HP):
        out_acc[rows[hp]] = cmp_out[rows[hp]] * gate_tiles[hp]

    w0 = pl.multiple_of(jnp.maximum(q0 - WINDOW, 0), TQ)
    w_bias = wbias_ref[jnp.minimum(qt, WINDOW // TQ)]
    kk = kwin_ref[0, 0, pl.ds(w0, WK), :]
    win_acc = []
    for qx, v_ref in zip(q_plain, (vwin_e_ref, vwin_o_ref)):
        s = _dot_tb(qx, kk)
        es = []
        for hp in range(HP):
            sh = s[rows[hp]] + w_bias
            es.append(jnp.exp2(sh - jnp.max(sh, axis=-1, keepdims=True)).astype(BF16))
        win_acc.append(_dot(jnp.concatenate(es, axis=0), v_ref[0, 0, pl.ds(w0, WK), :]))
    emit(2, win_acc)

    ni = lax.broadcasted_iota(jnp.int32, (LANES, LANES), 0)
    ci = lax.broadcasted_iota(jnp.int32, (LANES, LANES), 1)
    overlap_t = ((ci * CMP_STRIDE < ni * SEL_BLOCK + SEL_BLOCK)
                 & (ci * CMP_STRIDE + CMP_LEN - 1 >= ni * SEL_BLOCK) & (ni < n_blk))
    overlap_t = jnp.where(overlap_t, 1.0, 0.0).astype(BF16)
    p_hi = p_sum.astype(BF16)
    p_lo = (p_sum - p_hi.astype(F32)).astype(BF16)
    imp_t = (_dot_tb(overlap_t, p_hi) + _dot_tb(overlap_t, p_lo))[0:n_blk]
    blk_id = lax.broadcasted_iota(jnp.int32, (n_blk, TQ), 0)
    cur = (q0 + lax.broadcasted_iota(jnp.int32, (n_blk, TQ), 1)) // SEL_BLOCK
    forced = (blk_id == 0) | (blk_id == cur) | (blk_id == cur - 1)
    score = jnp.where(forced, FORCE_SCORE, jnp.where(blk_id <= cur, imp_t, -1.0))
    rank = jnp.zeros((n_blk, TQ), jnp.int32)
    for mblk in range(n_blk):
        other = score[mblk:mblk + 1, :]
        ahead = (other > score) | ((other == score) & (blk_id > mblk))
        rank = rank + jnp.where(ahead, 1, 0)
    not_sel_t = jnp.where(rank < SEL_TOPN, 0.0, 1.0)
    padded = jnp.concatenate([jnp.zeros((NOTSEL_LANE0, TQ), F32), not_sel_t,
                              jnp.zeros((LANES - NOTSEL_LANE0 - n_blk, TQ), F32)], axis=0)
    not_sel = jnp.concatenate([padded.T.astype(BF16)] * HP, axis=0)
    q_aug = tuple(qx + not_sel for qx in q_plain)

    for ref in (acc_e, acc_o):
        ref[...] = jnp.zeros_like(ref)
    for ref in (m_e, m_o):
        ref[...] = jnp.full_like(ref, MASK_BIAS)

    last_k0 = ksel_ref.shape[2] - KT

    def even_scores(kt):
        k0 = pl.multiple_of(jnp.minimum(kt * KT, last_k0), KT)
        return _dot_tb(q_aug[0], ksel_ref[0, 0, pl.ds(k0, KT), :])

    def softmax_pv(get_scores, bias, vv, acc, m_ref):
        es, alphas = [], []
        for hp in range(HP):
            sh = get_scores(hp) + bias
            tiles = [sh[:, j * LANES:(j + 1) * LANES] for j in range(bias.shape[1] // LANES)]
            m_old = m_ref[rows[hp]]
            m_new = jnp.maximum(m_old, jnp.max(functools.reduce(jnp.maximum, tiles), axis=-1, keepdims=True))
            m_ref[rows[hp]] = m_new
            alphas.append(jnp.exp2(m_old - m_new))
            es.append(jnp.concatenate([jnp.exp2(t - m_new).astype(BF16) for t in tiles], axis=1))
        pv = _dot(jnp.concatenate(es, axis=0), vv)
        for hp in range(HP):
            acc[rows[hp]] = alphas[hp] * acc[rows[hp]] + pv[rows[hp]]

    s_next[...] = even_scores(0)

    def sel_tile(kt):
        k0 = pl.multiple_of(kt * KT, KT)
        bias = cbias_ref[jnp.minimum(qt - kt * (KT // TQ), KT // TQ)]
        s_odd = _dot_tb(q_aug[1], ksel_ref[0, 0, pl.ds(k0, KT), :])
        softmax_pv(lambda hp: s_next[rows[hp], :], bias, vsel_e_ref[0, 0, pl.ds(k0, KT), :], acc_e, m_e)
        s_next[...] = even_scores(kt + 1)
        softmax_pv(lambda hp: s_odd[rows[hp]], bias, vsel_o_ref[0, 0, pl.ds(k0, KT), :], acc_o, m_o)

    n_whole = (q0 + TQ) // KT

    def tile_pair(i, carry):
        sel_tile(2 * i)
        sel_tile(2 * i + 1)
        return carry

    lax.fori_loop(0, n_whole // 2, tile_pair, 0)

    @pl.when(n_whole % 2 == 1)
    def _odd_tile():
        sel_tile(n_whole - 1)

    @pl.when(n_whole * KT < q0 + TQ)
    def _diagonal_remainder():
        k_rem = pl.multiple_of(q0, TQ)
        kk = ksel_ref[0, 0, pl.ds(k_rem, TQ), :]
        bias = cbias_ref[0][:, 0:TQ]
        for qx, v_ref, acc, m_ref in ((q_aug[0], vsel_e_ref, acc_e, m_e), (q_aug[1], vsel_o_ref, acc_o, m_o)):
            s = _dot_tb(qx, kk)
            softmax_pv(lambda hp: s[rows[hp]], bias, v_ref[0, 0, pl.ds(k_rem, TQ), :], acc, m_ref)

    emit(1, (acc_e, acc_o))

    o_ref[0] = jnp.concatenate([out_acc[rows[hp]] for hp in range(HP)], axis=1).astype(BF16)


def _nsa(nq, gates, kv, kc, vc):
    B, S, _ = nq.shape
    G = NSA_GROUPS
    TQ = NSA_TQ
    assert S // SEL_BLOCK <= LANES - NOTSEL_LANE0 and S % NSA_KT == 0 and S >= NSA_WK
    assert NSA_KT == 2 * TQ and WINDOW % TQ == 0
    rows = HEAD_PAIRS * TQ
    i = jnp.arange(TQ, dtype=jnp.int32)[None, :, None]
    off = lambda n: jnp.arange(n + 1, dtype=jnp.int32)[:, None, None] * TQ
    j = jnp.arange(NSA_WK, dtype=jnp.int32)[None, None, :]
    t_rel = off(WINDOW // TQ) + i
    win_bias = jnp.where((j <= t_rel) & (j > t_rel - WINDOW), 0.0, MASK_BIAS).astype(F32)
    j = jnp.arange(NSA_KT, dtype=jnp.int32)[None, None, :]
    causal_bias = jnp.where(j <= off(NSA_KT // TQ) + i, 0.0, MASK_BIAS).astype(F32)
    qblk = pl.BlockSpec((1, TQ, Q_STACKS * NSA_GQ_W), lambda b, g, t: (b, t, g))
    oblk = pl.BlockSpec((1, TQ, NSA_GQ_W), lambda b, g, t: (b, t, g))
    gblk = pl.BlockSpec((1, TQ, LANES), lambda b, g, t: (b, t, g))
    kvblk = lambda j: pl.BlockSpec((1, 1, S, LANES), lambda b, g, t: (g, b, 0, j))
    cblk = pl.BlockSpec((1, 1, 2 * LANES, LANES), lambda b, g, t: (b, g, 0, 0))
    wide = pltpu.VMEM((rows, LANES), F32)
    return pl.pallas_call(
        _nsa_kernel,
        grid=(B, G, S // TQ),
        in_specs=([qblk, gblk] + [kvblk(j) for j in range(KV_TILES)] + [cblk, cblk]
                  + [_resident(win_bias.shape), _resident(causal_bias.shape)]),
        out_specs=oblk,
        out_shape=jax.ShapeDtypeStruct((B, S, NSA_Q_W), BF16),
        scratch_shapes=[wide] * 5 + [pltpu.VMEM((rows, NSA_KT), F32),
                                     pltpu.VMEM((3 * NSA_HPG // 2, TQ, LANES), F32)],
        compiler_params=pltpu.CompilerParams(dimension_semantics=("arbitrary",) * 3,
                                             vmem_limit_bytes=VMEM_LIMIT),
        name="nsa",
    )(nq, gates, *([kv] * KV_TILES), kc, vc, win_bias, causal_bias)


def _tail_kernel(final, x_ref, yr_ref, yn_ref, p_ref, gmix_ref, gmlp_ref, gple_ref, gfin_ref,
                 wmg_ref, wro_ref, wno_ref, wout_ref, wup_ref, wdn_ref, wpg_ref, wpp_ref, o_ref):
    x = x_ref[...]
    h = _rms(x, gmix_ref[...]).astype(BF16)
    o_ret = _dot(yr_ref[...], wro_ref[...])
    o_nsa = _dot(yn_ref[...], wno_ref[...])
    g_ret = jax.nn.sigmoid(_dot(h, wmg_ref[:, 0:D_MODEL]))
    g_nsa = jax.nn.sigmoid(_dot(h, wmg_ref[:, D_MODEL:2 * D_MODEL]))
    mix = (g_ret * o_ret + g_nsa * o_nsa).astype(BF16)
    x = x + _dot(mix, wout_ref[...])
    h2 = _rms(x, gmlp_ref[...]).astype(BF16)
    mlp = jnp.zeros_like(x)
    step = 1024
    for c0 in range(0, MLP_HIDDEN, step):
        up = jnp.maximum(_dot(h2, wup_ref[:, c0:c0 + step]), 0.0)
        mlp = mlp + _dot((up * up).astype(BF16), wdn_ref[c0:c0 + step, :])
    x = x + mlp
    h3 = _rms(x, gple_ref[...]).astype(BF16)
    ple_gate = jax.nn.sigmoid(_dot(h3, wpg_ref[...]))
    x = x + _dot(p_ref[...].astype(BF16), wpp_ref[...]) * ple_gate
    if final:
        x = _rms(x, gfin_ref[...])
    o_ref[...] = x


def _tail(final, x2, y_ret, y_nsa, p2, g_mix, g_mlp, g_ple, g_fin, w_mg, w_ro, w_no, w_out, w_up, w_dn,
          w_pg, w_pp):
    T = x2.shape[0]
    tm = TAIL_TM
    row = lambda w: pl.BlockSpec((tm, w), lambda i: (i, 0))
    gains = [_resident((1, D_MODEL))] * 4
    weights = [_resident(w.shape) for w in (w_mg, w_ro, w_no, w_out, w_up, w_dn, w_pg, w_pp)]
    return pl.pallas_call(
        functools.partial(_tail_kernel, final),
        grid=(T // tm,),
        in_specs=[row(D_MODEL), row(RET_V_W), row(NSA_Q_W), row(PLE_DIM)] + gains + weights,
        out_specs=row(D_MODEL),
        out_shape=jax.ShapeDtypeStruct((T, D_MODEL), F32),
        compiler_params=pltpu.CompilerParams(dimension_semantics=("arbitrary",),
                                             vmem_limit_bytes=VMEM_LIMIT),
        name="tail",
    )(x2, y_ret, y_nsa, p2, g_mix, g_mlp, g_ple, g_fin, w_mg, w_ro, w_no, w_out, w_up, w_dn, w_pg, w_pp)


def _pack_w_in(w):
    gate = w[:, C_GATE:C_GATE + 3 * NSA_HEADS]
    parts = [w[:, :C_GATE]]
    for g in range(NSA_GROUPS):
        cols = [j * NSA_HEADS + g * NSA_HPG + r for j in range(3) for r in range(NSA_HPG)]
        parts.append(jnp.pad(gate[:, jnp.array(cols)], ((0, 0), (0, LANES - len(cols)))))
    return jnp.concatenate(parts, axis=1).astype(BF16)


def kernel(x, p, positions, norm_mix_g, w_in, ret_gn_g, w_ret_o, cmp_pe_k, cmp_k_w1, cmp_k_w2, cmp_pe_v, cmp_v_w1, cmp_v_w2, w_nsa_o, w_merge_gate, w_out, norm_mlp_g, w_mlp_up, w_mlp_down, norm_ple_g, w_ple_gate, w_ple_proj, norm_final_g):
    B, S, D = x.shape
    depth = p.shape[0]
    T = B * S
    G = NSA_GROUPS
    bf = lambda a: a.astype(BF16)
    row = lambda a: a.reshape(1, -1)
    posf = positions.reshape(T, 1).astype(F32)
    inv_r = ROPE_THETA ** (-jnp.arange(0, RET_DK, 2, dtype=F32) / RET_DK)
    inv_n = ROPE_THETA ** (-jnp.arange(0, NSA_DH, 2, dtype=F32) / NSA_DH)
    inv_freq = jnp.concatenate([inv_r, inv_n, jnp.zeros((LANES - inv_r.shape[0] - inv_n.shape[0],), F32)])
    inv_freq = inv_freq.reshape(1, LANES)
    n_rows = S // CMP_STRIDE

    def strides(t):
        t = t.reshape(B, S, G, NSA_DH).transpose(0, 2, 1, 3)
        return t.reshape(B, G, n_rows, CMP_STRIDE * NSA_DH)

    def pe_rows(pe):
        return jnp.broadcast_to(bf(pe).reshape(1, -1), (8, CMP_LEN * NSA_DH))

    def swap_pairs(w):
        return w.reshape(NSA_HEADS // 2, 2, NSA_DH, -1)[:, ::-1].reshape(w.shape)

    def widen(w, outer):
        z = jnp.zeros_like(w)
        return bf(jnp.concatenate([w, z, z, w] if outer else [z, w, w, z], axis=1))

    x2 = x.reshape(T, D)
    for i in range(depth):
        y_ret, nq, cmp_kv, kv, gates = _proj(S, x2, posf, row(norm_mix_g[i]), inv_freq, row(ret_gn_g[i]),
                                              _pack_w_in(w_in[i]))
        kc, vc = _compress(strides(cmp_kv[:, :LANES]), strides(cmp_kv[:, LANES:]),
                           bf(cmp_k_w1[i]), widen(cmp_k_w2[i], True), pe_rows(cmp_pe_k[i]),
                           bf(cmp_v_w1[i]), widen(cmp_v_w2[i], False), pe_rows(cmp_pe_v[i]))
        sh3 = lambda a: a.reshape(B, S, a.shape[-1])
        y_nsa = _nsa(sh3(nq), sh3(gates), kv.reshape(G, B, S, kv.shape[-1]), kc, vc)
        x2 = _tail(i == depth - 1, x2, y_ret, y_nsa.reshape(T, NSA_Q_W),
                   p[i].reshape(T, PLE_DIM), row(norm_mix_g[i]), row(norm_mlp_g[i]), row(norm_ple_g[i]),
                   row(norm_final_g), bf(w_merge_gate[i]), bf(w_ret_o[i]), bf(swap_pairs(w_nsa_o[i])), bf(w_out[i]),
                   bf(w_mlp_up[i]), bf(w_mlp_down[i]), bf(w_ple_gate[i]), bf(w_ple_proj[i]))
    return x2.reshape(B, S, D)
```

```python
import functools
import math

import jax
import jax.numpy as jnp
from jax import lax
from jax.experimental import pallas as pl
from jax.experimental.pallas import tpu as pltpu

F32 = jnp.float32
BF16 = jnp.bfloat16

D_MODEL = 1024
PLE_DIM = 256
RMS_EPS = 1e-6
ROPE_THETA = 10000.0
RET_HEADS = 8
RET_DK = 128
RET_DV = 256
RET_CHUNK = 256
RET_QK_W = RET_HEADS * RET_DK
RET_V_W = RET_HEADS * RET_DV
NSA_HEADS = 16
NSA_GROUPS = 2
NSA_HPG = 8
NSA_DH = 64
NSA_Q_W = NSA_HEADS * NSA_DH
NSA_GQ_W = NSA_HPG * NSA_DH
HEAD_PAIRS = NSA_HPG // 2
Q_STACKS = 3
CMP_LEN = 32
CMP_STRIDE = 16
CMP_HIDDEN = 256
SEL_BLOCK = 64
SEL_TOPN = 8
WINDOW = 512
FORCE_SCORE = 1e6
MLP_HIDDEN = 4 * D_MODEL

LANES = 128
MASK_BIAS = -1e30
MAX_FLOOR = -1e29
LOG2E = math.log2(math.e)
VMEM_LIMIT = 56 * 1024 * 1024

PROJ_TM = 256
TAIL_TM = 512
NSA_TQ = 256
NSA_KT = 512
NSA_WK = WINDOW + NSA_TQ

C_RQ, C_RK, C_RV, C_RG = 0, 1024, 2048, 4096
C_NQ, C_KV, C_GATE, C_END = 6144, 7168, 7936, 8192
GATE_W = NSA_GROUPS * LANES
KV_TILE = {2: 0, 3: 1, 4: 3, 5: 4}
KV_TILES = 6


def _resident(shape):
    nd = len(shape)
    return pl.BlockSpec(shape, lambda *_: (0,) * nd, pipeline_mode=pl.Buffered(1))


def _rms(x, g):
    return x * lax.rsqrt(jnp.mean(x * x, axis=-1, keepdims=True) + RMS_EPS) * g


def _dot(a, b):
    return jnp.dot(a, b, preferred_element_type=F32)


def _dot_tb(a, b):
    return lax.dot_general(a, b, (((1,), (1,)), ((), ())), preferred_element_type=F32)


def _dot_ta(a, b):
    return lax.dot_general(a, b, (((0,), (0,)), ((), ())), preferred_element_type=F32)


def _proj_kernel(seq_len, x_ref, pos_ref, g_ref, inv_ref, gn_ref, w_ref,
                 yret_ref, nq_ref, cmp_ref, kv_ref, gate_ref,
                 rq_ref, rk_ref, rv_ref, rg_ref, state_ref, decay_ref, xi_ref, zeta_ref):
    tm = x_ref.shape[0]
    first_row = pl.program_id(0) * tm

    @pl.when(first_row == 0)
    def _tables():
        _retention_tables(decay_ref, xi_ref, zeta_ref)

    h = _rms(x_ref[...], g_ref[...]).astype(BF16)
    pos = pos_ref[...]
    lane = lax.broadcasted_iota(jnp.int32, (tm, LANES), 1)
    ang = pos * inv_ref[...]
    cos_a = jnp.cos(ang)
    sin_a = jnp.sin(ang)
    cos_r = jnp.where(lane < 64, cos_a, pltpu.roll(cos_a, 64, 1))
    sin_r = jnp.where(lane < 64, -sin_a, pltpu.roll(sin_a, 64, 1))

    def tile_nsa(t):
        return jnp.where(lane < 32, pltpu.roll(t, 64, 1),
                         jnp.where(lane < 64, pltpu.roll(t, 96, 1), jnp.where(lane < 96, t, pltpu.roll(t, 32, 1))))

    low = (lane & 32) == 0
    cos_n = tile_nsa(cos_a)
    sin_n = tile_nsa(sin_a)
    sin_n = jnp.where(low, -sin_n, sin_n)

    def rope_r(y):
        return y * cos_r + pltpu.roll(y, 64, 1) * sin_r

    def rope_n(y):
        partner = jnp.where(low, pltpu.roll(y, 96, 1), pltpu.roll(y, 32, 1))
        return y * cos_n + partner * sin_n

    seq_pos = first_row % seq_len + lax.broadcasted_iota(jnp.int32, (tm, LANES), 0)
    sel_tag = jnp.where(lane - NSA_DH == seq_pos // SEL_BLOCK, MASK_BIAS, 0.0)

    k_scale = RET_DK ** -0.5
    q_scale = NSA_DH ** -0.5 * LOG2E
    chunk = 512
    heads_per_chunk = chunk // RET_DV
    plan = []
    for v_chunk in range(RET_V_W // chunk):
        plan.append(C_RV + v_chunk * chunk)
        if (v_chunk * heads_per_chunk) % (chunk // RET_DK) == 0:
            qk_chunk = v_chunk * heads_per_chunk * RET_DK
            plan += [C_RQ + qk_chunk, C_RK + qk_chunk]
        plan.append(C_RG + v_chunk * chunk)
        plan.append(tuple(range(v_chunk * heads_per_chunk, (v_chunk + 1) * heads_per_chunk)))
    plan += list(range(C_NQ, C_END, chunk))
    for c0 in plan:
        if isinstance(c0, tuple):
            _retention_heads(c0, first_row % seq_len == 0, rq_ref, rk_ref, rv_ref, rg_ref, gn_ref, yret_ref,
                             state_ref, decay_ref, xi_ref, zeta_ref)
            continue
        y = _dot(h, w_ref[:, c0:c0 + chunk])
        for j in range(chunk // LANES):
            col = c0 + j * LANES
            piece = y[:, j * LANES:(j + 1) * LANES]
            if col < C_RK:
                rq_ref[:, col - C_RQ:col - C_RQ + LANES] = rope_r(piece).astype(BF16)
            elif col < C_RV:
                rk_ref[:, col - C_RK:col - C_RK + LANES] = (rope_r(piece) * k_scale).astype(BF16)
            elif col < C_RG:
                rv_ref[:, col - C_RV:col - C_RV + LANES] = piece.astype(BF16)
            elif col < C_NQ:
                rg_ref[:, col - C_RG:col - C_RG + LANES] = piece.astype(BF16)
            elif col < C_KV:
                val = rope_n(piece) * q_scale
                pair = (col - C_NQ) // LANES
                base = (pair // HEAD_PAIRS) * Q_STACKS * NSA_GQ_W + (pair % HEAD_PAIRS) * LANES
                zeros = jnp.zeros_like(val)
                nq_ref[:, base:base + LANES] = jnp.where(lane < NSA_DH, val, zeros).astype(BF16)
                nq_ref[:, base + NSA_GQ_W:base + NSA_GQ_W + LANES] = jnp.where(
                    lane < NSA_DH, pltpu.roll(val, NSA_DH, 1), zeros).astype(BF16)
                nq_ref[:, base + 2 * NSA_GQ_W:base + 2 * NSA_GQ_W + LANES] = val.astype(BF16)
            elif col < C_GATE:
                j_kv = (col - C_KV) // LANES
                is_key = j_kv % 2 == 0
                val = rope_n(piece) if is_key else piece
                if j_kv < 2:
                    cmp_ref[:, j_kv * LANES:(j_kv + 1) * LANES] = val.astype(BF16)
                    continue
                swapped = pltpu.roll(val, NSA_DH, 1)
                lower = lane < NSA_DH
                grouped = (val, swapped), (swapped, val)
                t0 = KV_TILE[j_kv]
                for g in range(NSA_GROUPS):
                    lo, hi = grouped[g]
                    if is_key:
                        fill = sel_tag if j_kv == 2 else jnp.zeros_like(val)
                        tiles = (jnp.where(lower, lo, fill),)
                    else:
                        tiles = (jnp.where(lower, 1.0, hi), jnp.where(lower, lo, 1.0))
                    for k, tile in enumerate(tiles):
                        kv_ref[g, :, (t0 + k) * LANES:(t0 + k + 1) * LANES] = tile.astype(BF16)
            else:
                gate_ref[:, col - C_GATE:col - C_GATE + LANES] = jax.nn.sigmoid(piece)


def _proj(seq_len, x2, posf, g_mix, inv_freq, gn_g, w_all):
    T = x2.shape[0]
    tm = PROJ_TM
    C = RET_CHUNK
    assert seq_len % tm == 0 and tm % C == 0
    row = lambda w: pl.BlockSpec((tm, w), lambda i: (i, 0))
    kv_w = KV_TILES * LANES
    out_shapes = [
        jax.ShapeDtypeStruct((T, RET_V_W), BF16),
        jax.ShapeDtypeStruct((T, Q_STACKS * NSA_Q_W), BF16),
        jax.ShapeDtypeStruct((T, 2 * LANES), BF16),
        jax.ShapeDtypeStruct((NSA_GROUPS, T, kv_w), BF16),
        jax.ShapeDtypeStruct((T, GATE_W), F32),
    ]
    tables = [pltpu.VMEM((RET_HEADS, C, w), F32) for w in (C, RET_DV, RET_DK)]
    return pl.pallas_call(
        functools.partial(_proj_kernel, seq_len),
        grid=(T // tm,),
        in_specs=[row(D_MODEL), row(1), _resident((1, D_MODEL)), _resident((1, LANES)), _resident((1, RET_V_W)),
                  _resident((D_MODEL, C_END))],
        out_specs=[row(RET_V_W), row(Q_STACKS * NSA_Q_W), row(2 * LANES),
                   pl.BlockSpec((NSA_GROUPS, tm, kv_w), lambda i: (0, i, 0)), row(GATE_W)],
        out_shape=out_shapes,
        scratch_shapes=[pltpu.VMEM((tm, RET_QK_W), BF16), pltpu.VMEM((tm, RET_QK_W), BF16),
                        pltpu.VMEM((tm, RET_V_W), BF16), pltpu.VMEM((tm, RET_V_W), BF16),
                        pltpu.VMEM((RET_HEADS, RET_DK, RET_DV), F32)] + tables,
        compiler_params=pltpu.CompilerParams(dimension_semantics=("arbitrary",),
                                             vmem_limit_bytes=VMEM_LIMIT),
        name="proj",
    )(x2, posf, g_mix, inv_freq, gn_g, w_all)


def _compress_kernel(k16_ref, v16_ref, w1k_ref, w2k_ref, pek_ref, w1v_ref, w2v_ref, pev_ref,
                     kc_ref, vc_ref):
    half = CMP_STRIDE * NSA_DH
    for x_ref, w1_ref, w2_ref, pe_ref, o_ref in ((k16_ref, w1k_ref, w2k_ref, pek_ref, kc_ref),
                                                 (v16_ref, w1v_ref, w2v_ref, pev_ref, vc_ref)):
        x = x_ref[0, 0]
        first = _dot(x, w1_ref[0:half, :])
        second = _dot(x, w1_ref[half:2 * half, :])
        pe_term = _dot(pe_ref[...], w1_ref[...])[0:1, :]
        hidden = first + pltpu.roll(second, second.shape[0] - 1, 0) + pe_term
        act = jax.nn.gelu(hidden).astype(BF16)
        both = _dot(act, w2_ref[...]).astype(BF16)
        n_cmp = both.shape[0]
        o_ref[0, 0, 0:n_cmp, :] = both[:, 0:LANES]
        o_ref[0, 0, n_cmp:2 * n_cmp, :] = both[:, LANES:2 * LANES]


def _compress(k16, v16, w1k, w2k, pek, w1v, w2v, pev):
    B, G, R, W = k16.shape
    blk = pl.BlockSpec((1, 1, R, W), lambda b, g: (b, g, 0, 0))
    oblk = pl.BlockSpec((1, 1, 2 * R, LANES), lambda b, g: (b, g, 0, 0))
    out = jax.ShapeDtypeStruct((B, G, 2 * R, LANES), BF16)
    wspecs = [_resident(w1k.shape), _resident(w2k.shape), _resident(pek.shape)]
    return pl.pallas_call(
        _compress_kernel,
        grid=(B, G),
        in_specs=[blk, blk] + wspecs + wspecs,
        out_specs=[oblk, oblk],
        out_shape=[out, out],
        compiler_params=pltpu.CompilerParams(dimension_semantics=("arbitrary", "arbitrary"),
                                             vmem_limit_bytes=VMEM_LIMIT),
        name="compress",
    )(k16, v16, w1k, w2k, pek, w1v, w2v, pev)


_RET_LOG_G = [math.log(1.0 - 2.0 ** (-5.0 - h)) for h in range(RET_HEADS)]


def _retention_tables(decay_ref, xi_ref, zeta_ref):
    C = RET_CHUNK
    r = lax.broadcasted_iota(jnp.int32, (C, C), 0).astype(F32)
    c = lax.broadcasted_iota(jnp.int32, (C, C), 1).astype(F32)
    diff = r - c
    r_out = lax.broadcasted_iota(jnp.int32, (C, RET_DV), 0).astype(F32)
    r_key = lax.broadcasted_iota(jnp.int32, (C, RET_DK), 0).astype(F32)
    for h in range(RET_HEADS):
        lg = _RET_LOG_G[h]
        decay_ref[h] = jnp.where(diff >= 0, jnp.exp(jnp.maximum(diff, 0.0) * lg), 0.0)
        xi_ref[h] = jnp.exp((r_out + 1.0) * lg)
        zeta_ref[h] = jnp.exp((C - 1.0 - r_key) * lg)


def _retention_heads(heads, new_sequence, q_ref, k_ref, v_ref, g_ref, gn_ref, y_ref, state_ref, decay_ref, xi_ref,
                     zeta_ref):
    C = RET_CHUNK
    n_chunks = q_ref.shape[0] // C
    for h in heads:
        qs = slice(h * RET_DK, (h + 1) * RET_DK)
        vs = slice(h * RET_DV, (h + 1) * RET_DV)
        xi = xi_ref[h]
        state = jnp.where(new_sequence, 0.0, state_ref[h])
        for c in range(n_chunks):
            tok = slice(c * C, (c + 1) * C)
            qh = q_ref[tok, qs]
            kh = k_ref[tok, qs]
            vh = v_ref[tok, vs]
            inner = (_dot_tb(qh, kh) * decay_ref[h]).astype(BF16)
            o = _dot(inner, vh) + _dot(qh, state.astype(BF16)) * xi
            kz = (kh.astype(F32) * zeta_ref[h]).astype(BF16)
            state = math.exp(C * _RET_LOG_G[h]) * state + _dot_ta(kz, vh)
            mu = jnp.mean(o, axis=-1, keepdims=True)
            d = o - mu
            var = jnp.mean(d * d, axis=-1, keepdims=True)
            y = d * lax.rsqrt(var + RMS_EPS) * gn_ref[:, vs]
            g = g_ref[tok, vs].astype(F32)
            y_ref[tok, vs] = (y * (g * jax.nn.sigmoid(g))).astype(BF16)
        state_ref[h] = state


NOTSEL_LANE0 = NSA_DH


def _nsa_kernel(q_ref, gate_ref, ksel_ref, vsel_e_ref, vsel_o_ref, kwin_ref, vwin_e_ref, vwin_o_ref,
                kc_ref, vc_ref, wbias_ref, cbias_ref, o_ref,
                acc_e, acc_o, m_e, m_o, out_acc, s_next, gate_tiles):
    TQ, KT, WK = NSA_TQ, NSA_KT, NSA_WK
    HP = HEAD_PAIRS
    n_blk = ksel_ref.shape[2] // SEL_BLOCK
    qt = pl.program_id(2)
    q0 = qt * TQ
    q_plain = tuple(jnp.concatenate([q_ref[0, :, (x * HP + hp) * LANES:(x * HP + hp + 1) * LANES]
                                     for hp in range(HP)], axis=0) for x in range(2))
    t_col = q0 + lax.broadcasted_iota(jnp.int32, (TQ, 1), 0)
    lane = lax.broadcasted_iota(jnp.int32, (TQ, LANES), 1)
    lower_half = lane < NSA_DH
    rows = [slice(hp * TQ, (hp + 1) * TQ) for hp in range(HP)]
    gates = gate_ref[0]

    for branch in range(3):
        for hp in range(HP):
            c = branch * NSA_HPG + hp * 2
            gate_tiles[branch * HP + hp] = jnp.take_along_axis(gates, jnp.where(lower_half, c + 1, c), axis=1)

    def emit(branch, acc_pair):
        for hp in range(HP):
            a_e = acc_pair[0][rows[hp]]
            a_o = acc_pair[1][rows[hp]]
            weight = gate_tiles[branch * HP + hp] / pltpu.roll(jnp.where(lower_half, a_e, a_o), NSA_DH, 1)
            out_acc[rows[hp]] += jnp.where(lower_half, a_o, a_e) * weight

    c_bias = jnp.where(lane * CMP_STRIDE + (CMP_LEN - 1) <= t_col, 0.0, MASK_BIAS)
    q_pairs = jnp.concatenate([q_ref[0, :, (2 * HP + hp) * LANES:(2 * HP + hp + 1) * LANES]
                               for hp in range(HP)], axis=0)
    s = _dot_tb(q_pairs, kc_ref[0, 0])
    p_sum = jnp.zeros((TQ, LANES), F32)
    ps = []
    for hp in range(HP):
        halves = []
        for x in range(2):
            sh = s[rows[hp], x * LANES:(x + 1) * LANES] + c_bias
            m = jnp.maximum(jnp.max(sh, axis=-1, keepdims=True), MAX_FLOOR)
            e = jnp.exp2(sh - m)
            l = jnp.sum(e, axis=-1, keepdims=True)
            p = e * jnp.where(l > 0, 1.0 / l, 0.0)
            p_sum = p_sum + p
            halves.append(p.astype(BF16))
        ps.append(jnp.concatenate(halves, axis=1))
    cmp_out = _dot(jnp.concatenate(ps, axis=0), vc_ref[0, 0])
    for hp in range(HP):
        out_acc[rows[hp]] = cmp_out[rows[hp]] * gate_tiles[hp]

    w0 = pl.multiple_of(jnp.maximum(q0 - WINDOW, 0), TQ)
    w_bias = wbias_ref[jnp.minimum(qt, WINDOW // TQ)]
    kk = kwin_ref[0, 0, pl.ds(w0, WK), :]
    win_acc = []
    for qx, v_ref in zip(q_plain, (vwin_e_ref, vwin_o_ref)):
        s = _dot_tb(qx, kk)
        es = []
        for hp in range(HP):
            sh = s[rows[hp]] + w_bias
            es.append(jnp.exp2(sh - jnp.max(sh, axis=-1, keepdims=True)).astype(BF16))
        win_acc.append(_dot(jnp.concatenate(es, axis=0), v_ref[0, 0, pl.ds(w0, WK), :]))
    emit(2, win_acc)

    ni = lax.broadcasted_iota(jnp.int32, (LANES, LANES), 0)
    ci = lax.broadcasted_iota(jnp.int32, (LANES, LANES), 1)
    overlap_t = ((ci * CMP_STRIDE < ni * SEL_BLOCK + SEL_BLOCK)
                 & (ci * CMP_STRIDE + CMP_LEN - 1 >= ni * SEL_BLOCK) & (ni < n_blk))
    overlap_t = jnp.where(overlap_t, 1.0, 0.0).astype(BF16)
    p_hi = p_sum.astype(BF16)
    p_lo = (p_sum - p_hi.astype(F32)).astype(BF16)
    imp_t = (_dot_tb(overlap_t, p_hi) + _dot_tb(overlap_t, p_lo))[0:n_blk]
    blk_id = lax.broadcasted_iota(jnp.int32, (n_blk, TQ), 0)
    cur = (q0 + lax.broadcasted_iota(jnp.int32, (n_blk, TQ), 1)) // SEL_BLOCK
    forced = (blk_id == 0) | (blk_id == cur) | (blk_id == cur - 1)
    score = jnp.where(forced, FORCE_SCORE, jnp.where(blk_id <= cur, imp_t, -1.0))
    rank = jnp.zeros((n_blk, TQ), jnp.int32)
    for mblk in range(n_blk):
        other = score[mblk:mblk + 1, :]
        ahead = (other > score) | ((other == score) & (blk_id > mblk))
        rank = rank + jnp.where(ahead, 1, 0)
    not_sel_t = jnp.where(rank < SEL_TOPN, 0.0, 1.0)
    padded = jnp.concatenate([jnp.zeros((NOTSEL_LANE0, TQ), F32), not_sel_t,
                              jnp.zeros((LANES - NOTSEL_LANE0 - n_blk, TQ), F32)], axis=0)
    not_sel = jnp.concatenate([padded.T.astype(BF16)] * HP, axis=0)
    q_aug = tuple(qx + not_sel for qx in q_plain)

    for ref in (acc_e, acc_o):
        ref[...] = jnp.zeros_like(ref)
    for ref in (m_e, m_o):
        ref[...] = jnp.full_like(ref, MASK_BIAS)

    last_k0 = ksel_ref.shape[2] - KT

    def even_scores(kt):
        k0 = pl.multiple_of(jnp.minimum(kt * KT, last_k0), KT)
        return _dot_tb(q_aug[0], ksel_ref[0, 0, pl.ds(k0, KT), :])

    def softmax_pv(get_scores, bias, vv, acc, m_ref):
        es, alphas = [], []
        for hp in range(HP):
            sh = get_scores(hp) + bias
            tiles = [sh[:, j * LANES:(j + 1) * LANES] for j in range(bias.shape[1] // LANES)]
            m_old = m_ref[rows[hp]]
            m_new = jnp.maximum(m_old, jnp.max(functools.reduce(jnp.maximum, tiles), axis=-1, keepdims=True))
            m_ref[rows[hp]] = m_new
            alphas.append(jnp.exp2(m_old - m_new))
            es.append(jnp.concatenate([jnp.exp2(t - m_new).astype(BF16) for t in tiles], axis=1))
        pv = _dot(jnp.concatenate(es, axis=0), vv)
        for hp in range(HP):
            acc[rows[hp]] = alphas[hp] * acc[rows[hp]] + pv[rows[hp]]

    s_next[...] = even_scores(0)

    def sel_tile(kt):
        k0 = pl.multiple_of(kt * KT, KT)
        bias = cbias_ref[jnp.minimum(qt - kt * (KT // TQ), KT // TQ)]
        s_odd = _dot_tb(q_aug[1], ksel_ref[0, 0, pl.ds(k0, KT), :])
        softmax_pv(lambda hp: s_next[rows[hp], :], bias, vsel_e_ref[0, 0, pl.ds(k0, KT), :], acc_e, m_e)
        s_next[...] = even_scores(kt + 1)
        softmax_pv(lambda hp: s_odd[rows[hp]], bias, vsel_o_ref[0, 0, pl.ds(k0, KT), :], acc_o, m_o)

    def diagonal_remainder(k_rem):
        kk = ksel_ref[0, 0, pl.ds(k_rem, TQ), :]
        bias = cbias_ref[0][:, 0:TQ]
        for qx, v_ref, acc, m_ref in ((q_aug[0], vsel_e_ref, acc_e, m_e), (q_aug[1], vsel_o_ref, acc_o, m_o)):
            s = _dot_tb(qx, kk)
            softmax_pv(lambda hp: s[rows[hp]], bias, v_ref[0, 0, pl.ds(k_rem, TQ), :], acc, m_ref)

    for tile_pos in range(ksel_ref.shape[2] // TQ):
        @pl.when(qt == tile_pos)
        def _keys():
            last_query = (tile_pos + 1) * TQ
            for kt in range(last_query // KT):
                sel_tile(kt)
            if last_query % KT:
                diagonal_remainder(tile_pos * TQ)

    emit(1, (acc_e, acc_o))

    o_ref[0] = jnp.concatenate([out_acc[rows[hp]] for hp in range(HP)], axis=1).astype(BF16)


def _nsa(nq, gates, kv, kc, vc):
    B, S, _ = nq.shape
    G = NSA_GROUPS
    TQ = NSA_TQ
    assert S // SEL_BLOCK <= LANES - NOTSEL_LANE0 and S % NSA_KT == 0 and S >= NSA_WK
    assert NSA_KT == 2 * TQ and WINDOW % TQ == 0
    rows = HEAD_PAIRS * TQ
    i = jnp.arange(TQ, dtype=jnp.int32)[None, :, None]
    off = lambda n: jnp.arange(n + 1, dtype=jnp.int32)[:, None, None] * TQ
    j = jnp.arange(NSA_WK, dtype=jnp.int32)[None, None, :]
    t_rel = off(WINDOW // TQ) + i
    win_bias = jnp.where((j <= t_rel) & (j > t_rel - WINDOW), 0.0, MASK_BIAS).astype(F32)
    j = jnp.arange(NSA_KT, dtype=jnp.int32)[None, None, :]
    causal_bias = jnp.where(j <= off(NSA_KT // TQ) + i, 0.0, MASK_BIAS).astype(F32)
    qblk = pl.BlockSpec((1, TQ, Q_STACKS * NSA_GQ_W), lambda b, g, t: (b, t, g))
    oblk = pl.BlockSpec((1, TQ, NSA_GQ_W), lambda b, g, t: (b, t, g))
    gblk = pl.BlockSpec((1, TQ, LANES), lambda b, g, t: (b, t, g))
    kvblk = lambda j: pl.BlockSpec((1, 1, S, LANES), lambda b, g, t: (g, b, 0, j))
    cblk = pl.BlockSpec((1, 1, 2 * LANES, LANES), lambda b, g, t: (b, g, 0, 0))
    wide = pltpu.VMEM((rows, LANES), F32)
    return pl.pallas_call(
        _nsa_kernel,
        grid=(B, G, S // TQ),
        in_specs=([qblk, gblk] + [kvblk(j) for j in range(KV_TILES)] + [cblk, cblk]
                  + [_resident(win_bias.shape), _resident(causal_bias.shape)]),
        out_specs=oblk,
        out_shape=jax.ShapeDtypeStruct((B, S, NSA_Q_W), BF16),
        scratch_shapes=[wide] * 5 + [pltpu.VMEM((rows, NSA_KT), F32),
                                     pltpu.VMEM((3 * NSA_HPG // 2, TQ, LANES), F32)],
        compiler_params=pltpu.CompilerParams(dimension_semantics=("arbitrary",) * 3,
                                             vmem_limit_bytes=VMEM_LIMIT),
        name="nsa",
    )(nq, gates, *([kv] * KV_TILES), kc, vc, win_bias, causal_bias)


def _tail_kernel(final, x_ref, yr_ref, yn_ref, p_ref, gmix_ref, gmlp_ref, gple_ref, gfin_ref,
                 wmg_ref, wro_ref, wno_ref, wout_ref, wup_ref, wdn_ref, wpg_ref, wpp_ref, o_ref):
    x = x_ref[...]
    h = _rms(x, gmix_ref[...]).astype(BF16)
    o_ret = _dot(yr_ref[...], wro_ref[...])
    o_nsa = _dot(yn_ref[...], wno_ref[...])
    g_ret = jax.nn.sigmoid(_dot(h, wmg_ref[:, 0:D_MODEL]))
    g_nsa = jax.nn.sigmoid(_dot(h, wmg_ref[:, D_MODEL:2 * D_MODEL]))
    mix = (g_ret * o_ret + g_nsa * o_nsa).astype(BF16)
    x = x + _dot(mix, wout_ref[...])
    h2 = _rms(x, gmlp_ref[...]).astype(BF16)
    mlp = jnp.zeros_like(x)
    step = 1024
    for c0 in range(0, MLP_HIDDEN, step):
        up = jnp.maximum(_dot(h2, wup_ref[:, c0:c0 + step]), 0.0)
        mlp = mlp + _dot((up * up).astype(BF16), wdn_ref[c0:c0 + step, :])
    x = x + mlp
    h3 = _rms(x, gple_ref[...]).astype(BF16)
    ple_gate = jax.nn.sigmoid(_dot(h3, wpg_ref[...]))
    x = x + _dot(p_ref[...].astype(BF16), wpp_ref[...]) * ple_gate
    if final:
        x = _rms(x, gfin_ref[...])
    o_ref[...] = x


def _tail(final, x2, y_ret, y_nsa, p2, g_mix, g_mlp, g_ple, g_fin, w_mg, w_ro, w_no, w_out, w_up, w_dn,
          w_pg, w_pp):
    T = x2.shape[0]
    tm = TAIL_TM
    row = lambda w: pl.BlockSpec((tm, w), lambda i: (i, 0))
    gains = [_resident((1, D_MODEL))] * 4
    weights = [_resident(w.shape) for w in (w_mg, w_ro, w_no, w_out, w_up, w_dn, w_pg, w_pp)]
    return pl.pallas_call(
        functools.partial(_tail_kernel, final),
        grid=(T // tm,),
        in_specs=[row(D_MODEL), row(RET_V_W), row(NSA_Q_W), row(PLE_DIM)] + gains + weights,
        out_specs=row(D_MODEL),
        out_shape=jax.ShapeDtypeStruct((T, D_MODEL), F32),
        compiler_params=pltpu.CompilerParams(dimension_semantics=("arbitrary",),
                                             vmem_limit_bytes=VMEM_LIMIT),
        name="tail",
    )(x2, y_ret, y_nsa, p2, g_mix, g_mlp, g_ple, g_fin, w_mg, w_ro, w_no, w_out, w_up, w_dn, w_pg, w_pp)


def _pack_w_in(w):
    gate = w[:, C_GATE:C_GATE + 3 * NSA_HEADS]
    parts = [w[:, :C_GATE]]
    for g in range(NSA_GROUPS):
        cols = [j * NSA_HEADS + g * NSA_HPG + r for j in range(3) for r in range(NSA_HPG)]
        parts.append(jnp.pad(gate[:, jnp.array(cols)], ((0, 0), (0, LANES - len(cols)))))
    return jnp.concatenate(parts, axis=1).astype(BF16)


def kernel(x, p, positions, norm_mix_g, w_in, ret_gn_g, w_ret_o, cmp_pe_k, cmp_k_w1, cmp_k_w2, cmp_pe_v, cmp_v_w1, cmp_v_w2, w_nsa_o, w_merge_gate, w_out, norm_mlp_g, w_mlp_up, w_mlp_down, norm_ple_g, w_ple_gate, w_ple_proj, norm_final_g):
    B, S, D = x.shape
    depth = p.shape[0]
    T = B * S
    G = NSA_GROUPS
    bf = lambda a: a.astype(BF16)
    row = lambda a: a.reshape(1, -1)
    posf = positions.reshape(T, 1).astype(F32)
    inv_r = ROPE_THETA ** (-jnp.arange(0, RET_DK, 2, dtype=F32) / RET_DK)
    inv_n = ROPE_THETA ** (-jnp.arange(0, NSA_DH, 2, dtype=F32) / NSA_DH)
    inv_freq = jnp.concatenate([inv_r, inv_n, jnp.zeros((LANES - inv_r.shape[0] - inv_n.shape[0],), F32)])
    inv_freq = inv_freq.reshape(1, LANES)
    n_rows = S // CMP_STRIDE

    def strides(t):
        t = t.reshape(B, S, G, NSA_DH).transpose(0, 2, 1, 3)
        return t.reshape(B, G, n_rows, CMP_STRIDE * NSA_DH)

    def pe_rows(pe):
        return jnp.broadcast_to(bf(pe).reshape(1, -1), (8, CMP_LEN * NSA_DH))

    def swap_pairs(w):
        return w.reshape(NSA_HEADS // 2, 2, NSA_DH, -1)[:, ::-1].reshape(w.shape)

    def widen(w, outer):
        z = jnp.zeros_like(w)
        return bf(jnp.concatenate([w, z, z, w] if outer else [z, w, w, z], axis=1))

    x2 = x.reshape(T, D)
    for i in range(depth):
        y_ret, nq, cmp_kv, kv, gates = _proj(S, x2, posf, row(norm_mix_g[i]), inv_freq, row(ret_gn_g[i]),
                                              _pack_w_in(w_in[i]))
        kc, vc = _compress(strides(cmp_kv[:, :LANES]), strides(cmp_kv[:, LANES:]),
                           bf(cmp_k_w1[i]), widen(cmp_k_w2[i], True), pe_rows(cmp_pe_k[i]),
                           bf(cmp_v_w1[i]), widen(cmp_v_w2[i], False), pe_rows(cmp_pe_v[i]))
        sh3 = lambda a: a.reshape(B, S, a.shape[-1])
        y_nsa = _nsa(sh3(nq), sh3(gates), kv.reshape(G, B, S, kv.shape[-1]), kc, vc)
        x2 = _tail(i == depth - 1, x2, y_ret, y_nsa.reshape(T, NSA_Q_W),
                   p[i].reshape(T, PLE_DIM), row(norm_mix_g[i]), row(norm_mlp_g[i]), row(norm_ple_g[i]),
                   row(norm_final_g), bf(w_merge_gate[i]), bf(w_ret_o[i]), bf(swap_pairs(w_nsa_o[i])), bf(w_out[i]),
                   bf(w_mlp_up[i]), bf(w_mlp_down[i]), bf(w_ple_gate[i]), bf(w_ple_proj[i]))
    return x2.reshape(B, S, D)
```

```python
import functools
import math

import jax
import jax.numpy as jnp
from jax import lax
from jax.experimental import pallas as pl
from jax.experimental.pallas import tpu as pltpu

F32 = jnp.float32
BF16 = jnp.bfloat16

D_MODEL = 1024
PLE_DIM = 256
RMS_EPS = 1e-6
ROPE_THETA = 10000.0
RET_HEADS = 8
RET_DK = 128
RET_DV = 256
RET_CHUNK = 256
RET_QK_W = RET_HEADS * RET_DK
RET_V_W = RET_HEADS * RET_DV
NSA_HEADS = 16
NSA_GROUPS = 2
NSA_HPG = 8
NSA_DH = 64
NSA_Q_W = NSA_HEADS * NSA_DH
NSA_GQ_W = NSA_HPG * NSA_DH
HEAD_PAIRS = NSA_HPG // 2
Q_STACKS = 3
CMP_LEN = 32
CMP_STRIDE = 16
CMP_HIDDEN = 256
SEL_BLOCK = 64
SEL_TOPN = 8
WINDOW = 512
FORCE_SCORE = 1e6
MLP_HIDDEN = 4 * D_MODEL

LANES = 128
MASK_BIAS = -1e30
MAX_FLOOR = -1e29
LOG2E = math.log2(math.e)
VMEM_LIMIT = 56 * 1024 * 1024

PROJ_TM = 256
TAIL_TM = 512
NSA_TQ = 256
NSA_KT = 512
NSA_WK = WINDOW + NSA_TQ

C_RQ, C_RK, C_RV, C_RG = 0, 1024, 2048, 4096
C_NQ, C_KV, C_GATE, C_END = 6144, 7168, 7936, 8192
GATE_W = NSA_GROUPS * LANES
KV_TILE = {2: 0, 3: 1, 4: 3, 5: 4}
KV_TILES = 6


def _resident(shape):
    nd = len(shape)
    return pl.BlockSpec(shape, lambda *_: (0,) * nd, pipeline_mode=pl.Buffered(1))


def _rms(x, g):
    return x * lax.rsqrt(jnp.mean(x * x, axis=-1, keepdims=True) + RMS_EPS) * g


def _dot(a, b):
    return jnp.dot(a, b, preferred_element_type=F32)


def _dot_tb(a, b):
    return lax.dot_general(a, b, (((1,), (1,)), ((), ())), preferred_element_type=F32)


def _dot_ta(a, b):
    return lax.dot_general(a, b, (((0,), (0,)), ((), ())), preferred_element_type=F32)


def _proj_kernel(seq_len, x_ref, pos_ref, g_ref, inv_ref, gn_ref, w_ref,
                 yret_ref, nq_ref, cmp_ref, kv_ref, gate_ref,
                 rq_ref, rk_ref, rv_ref, rg_ref, state_ref, decay_ref, xi_ref, zeta_ref):
    tm = x_ref.shape[0]
    first_row = pl.program_id(0) * tm

    @pl.when(first_row == 0)
    def _tables():
        _retention_tables(decay_ref, xi_ref, zeta_ref)

    h = _rms(x_ref[...], g_ref[...]).astype(BF16)
    pos = pos_ref[...]
    lane = lax.broadcasted_iota(jnp.int32, (tm, LANES), 1)
    ang = pos * inv_ref[...]
    cos_a = jnp.cos(ang)
    sin_a = jnp.sin(ang)
    cos_r = jnp.where(lane < 64, cos_a, pltpu.roll(cos_a, 64, 1))
    sin_r = jnp.where(lane < 64, -sin_a, pltpu.roll(sin_a, 64, 1))

    def tile_nsa(t):
        return jnp.where(lane < 32, pltpu.roll(t, 64, 1),
                         jnp.where(lane < 64, pltpu.roll(t, 96, 1), jnp.where(lane < 96, t, pltpu.roll(t, 32, 1))))

    low = (lane & 32) == 0
    cos_n = tile_nsa(cos_a)
    sin_n = tile_nsa(sin_a)
    sin_n = jnp.where(low, -sin_n, sin_n)

    def rope_r(y):
        return y * cos_r + pltpu.roll(y, 64, 1) * sin_r

    def rope_n(y):
        partner = jnp.where(low, pltpu.roll(y, 96, 1), pltpu.roll(y, 32, 1))
        return y * cos_n + partner * sin_n

    seq_pos = first_row % seq_len + lax.broadcasted_iota(jnp.int32, (tm, LANES), 0)
    sel_tag = jnp.where(lane - NSA_DH == seq_pos // SEL_BLOCK, MASK_BIAS, 0.0)

    k_scale = RET_DK ** -0.5
    q_scale = NSA_DH ** -0.5 * LOG2E
    chunk = 512
    heads_per_chunk = chunk // RET_DV
    plan = []
    for v_chunk in range(RET_V_W // chunk):
        plan.append(C_RV + v_chunk * chunk)
        if (v_chunk * heads_per_chunk) % (chunk // RET_DK) == 0:
            qk_chunk = v_chunk * heads_per_chunk * RET_DK
            plan += [C_RQ + qk_chunk, C_RK + qk_chunk]
        plan.append(C_RG + v_chunk * chunk)
        plan.append(tuple(range(v_chunk * heads_per_chunk, (v_chunk + 1) * heads_per_chunk)))
    plan += list(range(C_NQ, C_END, chunk))
    for c0 in plan:
        if isinstance(c0, tuple):
            _retention_heads(c0, first_row % seq_len == 0, rq_ref, rk_ref, rv_ref, rg_ref, gn_ref, yret_ref,
                             state_ref, decay_ref, xi_ref, zeta_ref)
            continue
        y = _dot(h, w_ref[:, c0:c0 + chunk])
        for j in range(chunk // LANES):
            col = c0 + j * LANES
            piece = y[:, j * LANES:(j + 1) * LANES]
            if col < C_RK:
                rq_ref[:, col - C_RQ:col - C_RQ + LANES] = rope_r(piece).astype(BF16)
            elif col < C_RV:
                rk_ref[:, col - C_RK:col - C_RK + LANES] = (rope_r(piece) * k_scale).astype(BF16)
            elif col < C_RG:
                rv_ref[:, col - C_RV:col - C_RV + LANES] = piece.astype(BF16)
            elif col < C_NQ:
                rg_ref[:, col - C_RG:col - C_RG + LANES] = piece.astype(BF16)
            elif col < C_KV:
                val = rope_n(piece) * q_scale
                pair = (col - C_NQ) // LANES
                base = (pair // HEAD_PAIRS) * Q_STACKS * NSA_GQ_W + (pair % HEAD_PAIRS) * LANES
                zeros = jnp.zeros_like(val)
                nq_ref[:, base:base + LANES] = jnp.where(lane < NSA_DH, val, zeros).astype(BF16)
                nq_ref[:, base + NSA_GQ_W:base + NSA_GQ_W + LANES] = jnp.where(
                    lane < NSA_DH, pltpu.roll(val, NSA_DH, 1), zeros).astype(BF16)
                nq_ref[:, base + 2 * NSA_GQ_W:base + 2 * NSA_GQ_W + LANES] = val.astype(BF16)
            elif col < C_GATE:
                j_kv = (col - C_KV) // LANES
                is_key = j_kv % 2 == 0
                val = rope_n(piece) if is_key else piece
                if j_kv < 2:
                    cmp_ref[:, j_kv * LANES:(j_kv + 1) * LANES] = val.astype(BF16)
                    continue
                swapped = pltpu.roll(val, NSA_DH, 1)
                lower = lane < NSA_DH
                grouped = (val, swapped), (swapped, val)
                t0 = KV_TILE[j_kv]
                for g in range(NSA_GROUPS):
                    lo, hi = grouped[g]
                    if is_key:
                        fill = sel_tag if j_kv == 2 else jnp.zeros_like(val)
                        tiles = (jnp.where(lower, lo, fill),)
                    else:
                        tiles = (jnp.where(lower, 1.0, hi), jnp.where(lower, lo, 1.0))
                    for k, tile in enumerate(tiles):
                        kv_ref[g, :, (t0 + k) * LANES:(t0 + k + 1) * LANES] = tile.astype(BF16)
            else:
                gate_ref[:, col - C_GATE:col - C_GATE + LANES] = jax.nn.sigmoid(piece)


def _proj(seq_len, x2, posf, g_mix, inv_freq, gn_g, w_all):
    T = x2.shape[0]
    tm = PROJ_TM
    C = RET_CHUNK
    assert seq_len % tm == 0 and tm % C == 0
    row = lambda w: pl.BlockSpec((tm, w), lambda i: (i, 0))
    kv_w = KV_TILES * LANES
    out_shapes = [
        jax.ShapeDtypeStruct((T, RET_V_W), BF16),
        jax.ShapeDtypeStruct((T, Q_STACKS * NSA_Q_W), BF16),
        jax.ShapeDtypeStruct((T, 2 * LANES), BF16),
        jax.ShapeDtypeStruct((NSA_GROUPS, T, kv_w), BF16),
        jax.ShapeDtypeStruct((T, GATE_W), F32),
    ]
    tables = [pltpu.VMEM((RET_HEADS, C, w), F32) for w in (C, RET_DV, RET_DK)]
    return pl.pallas_call(
        functools.partial(_proj_kernel, seq_len),
        grid=(T // tm,),
        in_specs=[row(D_MODEL), row(1), _resident((1, D_MODEL)), _resident((1, LANES)), _resident((1, RET_V_W)),
                  _resident((D_MODEL, C_END))],
        out_specs=[row(RET_V_W), row(Q_STACKS * NSA_Q_W), row(2 * LANES),
                   pl.BlockSpec((NSA_GROUPS, tm, kv_w), lambda i: (0, i, 0)), row(GATE_W)],
        out_shape=out_shapes,
        scratch_shapes=[pltpu.VMEM((tm, RET_QK_W), BF16), pltpu.VMEM((tm, RET_QK_W), BF16),
                        pltpu.VMEM((tm, RET_V_W), BF16), pltpu.VMEM((tm, RET_V_W), BF16),
                        pltpu.VMEM((RET_HEADS, RET_DK, RET_DV), F32)] + tables,
        compiler_params=pltpu.CompilerParams(dimension_semantics=("arbitrary",),
                                             vmem_limit_bytes=VMEM_LIMIT),
        name="proj",
    )(x2, posf, g_mix, inv_freq, gn_g, w_all)


def _compress_kernel(k16_ref, v16_ref, w1k_ref, w2k_ref, pek_ref, w1v_ref, w2v_ref, pev_ref,
                     kc_ref, vc_ref):
    half = CMP_STRIDE * NSA_DH
    for x_ref, w1_ref, w2_ref, pe_ref, o_ref in ((k16_ref, w1k_ref, w2k_ref, pek_ref, kc_ref),
                                                 (v16_ref, w1v_ref, w2v_ref, pev_ref, vc_ref)):
        x = x_ref[0, 0]
        first = _dot(x, w1_ref[0:half, :])
        second = _dot(x, w1_ref[half:2 * half, :])
        pe_term = _dot(pe_ref[...], w1_ref[...])[0:1, :]
        hidden = first + pltpu.roll(second, second.shape[0] - 1, 0) + pe_term
        act = jax.nn.gelu(hidden).astype(BF16)
        both = _dot(act, w2_ref[...]).astype(BF16)
        n_cmp = both.shape[0]
        o_ref[0, 0, 0:n_cmp, :] = both[:, 0:LANES]
        o_ref[0, 0, n_cmp:2 * n_cmp, :] = both[:, LANES:2 * LANES]


def _compress(k16, v16, w1k, w2k, pek, w1v, w2v, pev):
    B, G, R, W = k16.shape
    blk = pl.BlockSpec((1, 1, R, W), lambda b, g: (b, g, 0, 0))
    oblk = pl.BlockSpec((1, 1, 2 * R, LANES), lambda b, g: (b, g, 0, 0))
    out = jax.ShapeDtypeStruct((B, G, 2 * R, LANES), BF16)
    wspecs = [_resident(w1k.shape), _resident(w2k.shape), _resident(pek.shape)]
    return pl.pallas_call(
        _compress_kernel,
        grid=(B, G),
        in_specs=[blk, blk] + wspecs + wspecs,
        out_specs=[oblk, oblk],
        out_shape=[out, out],
        compiler_params=pltpu.CompilerParams(dimension_semantics=("arbitrary", "arbitrary"),
                                             vmem_limit_bytes=VMEM_LIMIT),
        name="compress",
    )(k16, v16, w1k, w2k, pek, w1v, w2v, pev)


_RET_LOG_G = [math.log(1.0 - 2.0 ** (-5.0 - h)) for h in range(RET_HEADS)]


def _retention_tables(decay_ref, xi_ref, zeta_ref):
    C = RET_CHUNK
    r = lax.broadcasted_iota(jnp.int32, (C, C), 0).astype(F32)
    c = lax.broadcasted_iota(jnp.int32, (C, C), 1).astype(F32)
    diff = r - c
    r_out = lax.broadcasted_iota(jnp.int32, (C, RET_DV), 0).astype(F32)
    r_key = lax.broadcasted_iota(jnp.int32, (C, RET_DK), 0).astype(F32)
    for h in range(RET_HEADS):
        lg = _RET_LOG_G[h]
        decay_ref[h] = jnp.where(diff >= 0, jnp.exp(jnp.maximum(diff, 0.0) * lg), 0.0)
        xi_ref[h] = jnp.exp((r_out + 1.0) * lg)
        zeta_ref[h] = jnp.exp((C - 1.0 - r_key) * lg)


def _retention_heads(heads, new_sequence, q_ref, k_ref, v_ref, g_ref, gn_ref, y_ref, state_ref, decay_ref, xi_ref,
                     zeta_ref):
    C = RET_CHUNK
    n_chunks = q_ref.shape[0] // C
    for h in heads:
        qs = slice(h * RET_DK, (h + 1) * RET_DK)
        vs = slice(h * RET_DV, (h + 1) * RET_DV)
        xi = xi_ref[h]
        state = jnp.where(new_sequence, 0.0, state_ref[h])
        for c in range(n_chunks):
            tok = slice(c * C, (c + 1) * C)
            qh = q_ref[tok, qs]
            kh = k_ref[tok, qs]
            vh = v_ref[tok, vs]
            inner = (_dot_tb(qh, kh) * decay_ref[h]).astype(BF16)
            o = _dot(inner, vh) + _dot(qh, state.astype(BF16)) * xi
            kz = (kh.astype(F32) * zeta_ref[h]).astype(BF16)
            state = math.exp(C * _RET_LOG_G[h]) * state + _dot_ta(kz, vh)
            mu = jnp.mean(o, axis=-1, keepdims=True)
            d = o - mu
            var = jnp.mean(d * d, axis=-1, keepdims=True)
            y = d * lax.rsqrt(var + RMS_EPS) * gn_ref[:, vs]
            g = g_ref[tok, vs].astype(F32)
            y_ref[tok, vs] = (y * (g * jax.nn.sigmoid(g))).astype(BF16)
        state_ref[h] = state


NOTSEL_LANE0 = NSA_DH


def _nsa_kernel(q_ref, gate_ref, ksel_ref, vsel_e_ref, vsel_o_ref, kwin_ref, vwin_e_ref, vwin_o_ref,
                kc_ref, vc_ref, wbias_ref, cbias_ref, o_ref,
                acc_e, acc_o, m_e, m_o, out_acc, s_next, gate_tiles):
    TQ, KT, WK = NSA_TQ, NSA_KT, NSA_WK
    HP = HEAD_PAIRS
    n_blk = ksel_ref.shape[2] // SEL_BLOCK
    qt = pl.program_id(2)
    q0 = qt * TQ
    q_plain = tuple(jnp.concatenate([q_ref[0, :, (x * HP + hp) * LANES:(x * HP + hp + 1) * LANES]
                                     for hp in range(HP)], axis=0) for x in range(2))
    t_col = q0 + lax.broadcasted_iota(jnp.int32, (TQ, 1), 0)
    lane = lax.broadcasted_iota(jnp.int32, (TQ, LANES), 1)
    lower_half = lane < NSA_DH
    rows = [slice(hp * TQ, (hp + 1) * TQ) for hp in range(HP)]
    gates = gate_ref[0]

    for branch in range(3):
        for hp in range(HP):
            c = branch * NSA_HPG + hp * 2
            gate_tiles[branch * HP + hp] = jnp.take_along_axis(gates, jnp.where(lower_half, c + 1, c), axis=1)

    def emit(branch, acc_pair):
        for hp in range(HP):
            a_e = acc_pair[0][rows[hp]]
            a_o = acc_pair[1][rows[hp]]
            weight = gate_tiles[branch * HP + hp] / pltpu.roll(jnp.where(lower_half, a_e, a_o), NSA_DH, 1)
            out_acc[rows[hp]] += jnp.where(lower_half, a_o, a_e) * weight

    c_bias = jnp.where(lane * CMP_STRIDE + (CMP_LEN - 1) <= t_col, 0.0, MASK_BIAS)
    q_pairs = jnp.concatenate([q_ref[0, :, (2 * HP + hp) * LANES:(2 * HP + hp + 1) * LANES]
                               for hp in range(HP)], axis=0)
    s = _dot_tb(q_pairs, kc_ref[0, 0])
    p_sum = jnp.zeros((TQ, LANES), F32)
    ps = []
    for hp in range(HP):
        halves = []
        for x in range(2):
            sh = s[rows[hp], x * LANES:(x + 1) * LANES] + c_bias
            m = jnp.maximum(jnp.max(sh, axis=-1, keepdims=True), MAX_FLOOR)
            e = jnp.exp2(sh - m)
            l = jnp.sum(e, axis=-1, keepdims=True)
            p = e * jnp.where(l > 0, 1.0 / l, 0.0)
            p_sum = p_sum + p
            halves.append(p.astype(BF16))
        ps.append(jnp.concatenate(halves, axis=1))
    cmp_out = _dot(jnp.concatenate(ps, axis=0), vc_ref[0, 0])
    for hp in range(HP):
        out_acc[rows[hp]] = cmp_out[rows[hp]] * gate_tiles[hp]

    w0 = pl.multiple_of(jnp.maximum(q0 - WINDOW, 0), TQ)
    w_bias = wbias_ref[jnp.minimum(qt, WINDOW // TQ)]
    kk = kwin_ref[0, 0, pl.ds(w0, WK), :]
    win_acc = []
    for qx, v_ref in zip(q_plain, (vwin_e_ref, vwin_o_ref)):
        s = _dot_tb(qx, kk)
        es = []
        for hp in range(HP):
            sh = s[rows[hp]] + w_bias
            es.append(jnp.exp2(sh - jnp.max(sh, axis=-1, keepdims=True)).astype(BF16))
        win_acc.append(_dot(jnp.concatenate(es, axis=0), v_ref[0, 0, pl.ds(w0, WK), :]))
    emit(2, win_acc)

    ni = lax.broadcasted_iota(jnp.int32, (LANES, LANES), 0)
    ci = lax.broadcasted_iota(jnp.int32, (LANES, LANES), 1)
    overlap_t = ((ci * CMP_STRIDE < ni * SEL_BLOCK + SEL_BLOCK)
                 & (ci * CMP_STRIDE + CMP_LEN - 1 >= ni * SEL_BLOCK) & (ni < n_blk))
    overlap_t = jnp.where(overlap_t, 1.0, 0.0).astype(BF16)
    p_hi = p_sum.astype(BF16)
    p_lo = (p_sum - p_hi.astype(F32)).astype(BF16)
    imp_t = (_dot_tb(overlap_t, p_hi) + _dot_tb(overlap_t, p_lo))[0:n_blk]
    blk_id = lax.broadcasted_iota(jnp.int32, (n_blk, TQ), 0)
    cur = (q0 + lax.broadcasted_iota(jnp.int32, (n_blk, TQ), 1)) // SEL_BLOCK
    forced = (blk_id == 0) | (blk_id == cur) | (blk_id == cur - 1)
    score = jnp.where(forced, FORCE_SCORE, jnp.where(blk_id <= cur, imp_t, -1.0))
    rank = jnp.zeros((n_blk, TQ), jnp.int32)
    for mblk in range(n_blk):
        other = score[mblk:mblk + 1, :]
        ahead = (other > score) | ((other == score) & (blk_id > mblk))
        rank = rank + jnp.where(ahead, 1, 0)
    not_sel_t = jnp.where(rank < SEL_TOPN, 0.0, 1.0)
    padded = jnp.concatenate([jnp.zeros((NOTSEL_LANE0, TQ), F32), not_sel_t,
                              jnp.zeros((LANES - NOTSEL_LANE0 - n_blk, TQ), F32)], axis=0)
    not_sel = jnp.concatenate([padded.T.astype(BF16)] * HP, axis=0)
    q_aug = tuple(qx + not_sel for qx in q_plain)

    for ref in (acc_e, acc_o):
        ref[...] = jnp.zeros_like(ref)
    for ref in (m_e, m_o):
        ref[...] = jnp.full_like(ref, MASK_BIAS)

    last_k0 = ksel_ref.shape[2] - KT

    def even_scores(kt):
        k0 = pl.multiple_of(jnp.minimum(kt * KT, last_k0), KT)
        return _dot_tb(q_aug[0], ksel_ref[0, 0, pl.ds(k0, KT), :])

    def softmax_pv(get_scores, bias, vv, acc, m_ref):
        es, alphas = [], []
        for hp in range(HP):
            sh = get_scores(hp) + bias
            tiles = [sh[:, j * LANES:(j + 1) * LANES] for j in range(bias.shape[1] // LANES)]
            m_old = m_ref[rows[hp]]
            m_new = jnp.maximum(m_old, jnp.max(functools.reduce(jnp.maximum, tiles), axis=-1, keepdims=True))
            m_ref[rows[hp]] = m_new
            alphas.append(jnp.exp2(m_old - m_new))
            es.append(jnp.concatenate([jnp.exp2(t - m_new).astype(BF16) for t in tiles], axis=1))
        pv = _dot(jnp.concatenate(es, axis=0), vv)
        for hp in range(HP):
            acc[rows[hp]] = alphas[hp] * acc[rows[hp]] + pv[rows[hp]]

    s_next[...] = even_scores(0)

    def sel_tile(kt):
        k0 = pl.multiple_of(kt * KT, KT)
        bias = cbias_ref[jnp.minimum(qt - kt * (KT // TQ), KT // TQ)]
        s_odd = _dot_tb(q_aug[1], ksel_ref[0, 0, pl.ds(k0, KT), :])
        softmax_pv(lambda hp: s_next[rows[hp], :], bias, vsel_e_ref[0, 0, pl.ds(k0, KT), :], acc_e, m_e)
        s_next[...] = even_scores(kt + 1)
        softmax_pv(lambda hp: s_odd[rows[hp]], bias, vsel_o_ref[0, 0, pl.ds(k0, KT), :], acc_o, m_o)

    n_whole = (q0 + TQ) // KT

    def tile_pair(i, carry):
        sel_tile(2 * i)
        sel_tile(2 * i + 1)
        return carry

    lax.fori_loop(0, n_whole // 2, tile_pair, 0)

    @pl.when(n_whole % 2 == 1)
    def _odd_tile():
        sel_tile(n_whole - 1)

    @pl.when(n_whole * KT < q0 + TQ)
    def _diagonal_remainder():
        k_rem = pl.multiple_of(q0, TQ)
        kk = ksel_ref[0, 0, pl.ds(k_rem, TQ), :]
        bias = cbias_ref[0][:, 0:TQ]
        for qx, v_ref, acc, m_ref in ((q_aug[0], vsel_e_ref, acc_e, m_e), (q_aug[1], vsel_o_ref, acc_o, m_o)):
            s = _dot_tb(qx, kk)
            softmax_pv(lambda hp: s[rows[hp]], bias, v_ref[0, 0, pl.ds(k_rem, TQ), :], acc, m_ref)

    emit(1, (acc_e, acc_o))

    o_ref[0] = jnp.concatenate([out_acc[rows[hp]] for hp in range(HP)], axis=1).astype(BF16)


def _nsa(nq, gates, kv, kc, vc):
    B, S, _ = nq.shape
    G = NSA_GROUPS
    TQ = NSA_TQ
    assert S // SEL_BLOCK <= LANES - NOTSEL_LANE0 and S % NSA_KT == 0 and S >= NSA_WK
    assert NSA_KT == 2 * TQ and WINDOW % TQ == 0
    rows = HEAD_PAIRS * TQ
    i = jnp.arange(TQ, dtype=jnp.int32)[None, :, None]
    off = lambda n: jnp.arange(n + 1, dtype=jnp.int32)[:, None, None] * TQ
    j = jnp.arange(NSA_WK, dtype=jnp.int32)[None, None, :]
    t_rel = off(WINDOW // TQ) + i
    win_bias = jnp.where((j <= t_rel) & (j > t_rel - WINDOW), 0.0, MASK_BIAS).astype(F32)
    j = jnp.arange(NSA_KT, dtype=jnp.int32)[None, None, :]
    causal_bias = jnp.where(j <= off(NSA_KT // TQ) + i, 0.0, MASK_BIAS).astype(F32)
    qblk = pl.BlockSpec((1, TQ, Q_STACKS * NSA_GQ_W), lambda b, g, t: (b, t, g))
    oblk = pl.BlockSpec((1, TQ, NSA_GQ_W), lambda b, g, t: (b, t, g))
    gblk = pl.BlockSpec((1, TQ, LANES), lambda b, g, t: (b, t, g))
    kvblk = lambda j: pl.BlockSpec((1, 1, S, LANES), lambda b, g, t: (g, b, 0, j))
    cblk = pl.BlockSpec((1, 1, 2 * LANES, LANES), lambda b, g, t: (b, g, 0, 0))
    wide = pltpu.VMEM((rows, LANES), F32)
    return pl.pallas_call(
        _nsa_kernel,
        grid=(B, G, S // TQ),
        in_specs=([qblk, gblk] + [kvblk(j) for j in range(KV_TILES)] + [cblk, cblk]
                  + [_resident(win_bias.shape), _resident(causal_bias.shape)]),
        out_specs=oblk,
        out_shape=jax.ShapeDtypeStruct((B, S, NSA_Q_W), BF16),
        scratch_shapes=[wide] * 5 + [pltpu.VMEM((rows, NSA_KT), F32),
                                     pltpu.VMEM((3 * NSA_HPG // 2, TQ, LANES), F32)],
        compiler_params=pltpu.CompilerParams(dimension_semantics=("arbitrary",) * 3,
                                             vmem_limit_bytes=VMEM_LIMIT),
        name="nsa",
    )(nq, gates, *([kv] * KV_TILES), kc, vc, win_bias, causal_bias)


def _tail_kernel(final, x_ref, yr_ref, yn_ref, p_ref, gmix_ref, gmlp_ref, gple_ref, gfin_ref,
                 wmg_ref, wro_ref, wno_ref, wout_ref, wup_ref, wdn_ref, wpg_ref, wpp_ref, o_ref):
    x = x_ref[...]
    h = _rms(x, gmix_ref[...]).astype(BF16)
    o_ret = _dot(yr_ref[...], wro_ref[...])
    o_nsa = _dot(yn_ref[...], wno_ref[...])
    g_ret = jax.nn.sigmoid(_dot(h, wmg_ref[:, 0:D_MODEL]))
    g_nsa = jax.nn.sigmoid(_dot(h, wmg_ref[:, D_MODEL:2 * D_MODEL]))
    mix = (g_ret * o_ret + g_nsa * o_nsa).astype(BF16)
    x = x + _dot(mix, wout_ref[...])
    h2 = _rms(x, gmlp_ref[...]).astype(BF16)
    mlp = jnp.zeros_like(x)
    step = 1024
    for c0 in range(0, MLP_HIDDEN, step):
        up = jnp.maximum(_dot(h2, wup_ref[:, c0:c0 + step]), 0.0)
        mlp = mlp + _dot((up * up).astype(BF16), wdn_ref[c0:c0 + step, :])
    x = x + mlp
    h3 = _rms(x, gple_ref[...]).astype(BF16)
    ple_gate = jax.nn.sigmoid(_dot(h3, wpg_ref[...]))
    x = x + _dot(p_ref[...].astype(BF16), wpp_ref[...]) * ple_gate
    if final:
        x = _rms(x, gfin_ref[...])
    o_ref[...] = x


def _tail(final, x2, y_ret, y_nsa, p2, g_mix, g_mlp, g_ple, g_fin, w_mg, w_ro, w_no, w_out, w_up, w_dn,
          w_pg, w_pp):
    T = x2.shape[0]
    tm = TAIL_TM
    row = lambda w: pl.BlockSpec((tm, w), lambda i: (i, 0))
    gains = [_resident((1, D_MODEL))] * 4
    weights = [_resident(w.shape) for w in (w_mg, w_ro, w_no, w_out, w_up, w_dn, w_pg, w_pp)]
    return pl.pallas_call(
        functools.partial(_tail_kernel, final),
        grid=(T // tm,),
        in_specs=[row(D_MODEL), row(RET_V_W), row(NSA_Q_W), row(PLE_DIM)] + gains + weights,
        out_specs=row(D_MODEL),
        out_shape=jax.ShapeDtypeStruct((T, D_MODEL), F32),
        compiler_params=pltpu.CompilerParams(dimension_semantics=("arbitrary",),
                                             vmem_limit_bytes=VMEM_LIMIT),
        name="tail",
    )(x2, y_ret, y_nsa, p2, g_mix, g_mlp, g_ple, g_fin, w_mg, w_ro, w_no, w_out, w_up, w_dn, w_pg, w_pp)


def _pack_w_in(w):
    parts = [w[:, :C_GATE]]
    for g in range(NSA_GROUPS):
        for branch in range(3):
            c0 = C_GATE + branch * NSA_HEADS + g * NSA_HPG
            parts.append(w[:, c0:c0 + NSA_HPG])
        parts.append(jnp.zeros((w.shape[0], LANES - 3 * NSA_HPG), w.dtype))
    return jnp.concatenate(parts, axis=1).astype(BF16)


def kernel(x, p, positions, norm_mix_g, w_in, ret_gn_g, w_ret_o, cmp_pe_k, cmp_k_w1, cmp_k_w2, cmp_pe_v, cmp_v_w1, cmp_v_w2, w_nsa_o, w_merge_gate, w_out, norm_mlp_g, w_mlp_up, w_mlp_down, norm_ple_g, w_ple_gate, w_ple_proj, norm_final_g):
    B, S, D = x.shape
    depth = p.shape[0]
    T = B * S
    G = NSA_GROUPS
    bf = lambda a: a.astype(BF16)
    row = lambda a: a.reshape(1, -1)
    posf = positions.reshape(T, 1).astype(F32)
    inv_r = ROPE_THETA ** (-jnp.arange(0, RET_DK, 2, dtype=F32) / RET_DK)
    inv_n = ROPE_THETA ** (-jnp.arange(0, NSA_DH, 2, dtype=F32) / NSA_DH)
    inv_freq = jnp.concatenate([inv_r, inv_n, jnp.zeros((LANES - inv_r.shape[0] - inv_n.shape[0],), F32)])
    inv_freq = inv_freq.reshape(1, LANES)
    n_rows = S // CMP_STRIDE

    def strides(t):
        t = t.reshape(B, S, G, NSA_DH).transpose(0, 2, 1, 3)
        return t.reshape(B, G, n_rows, CMP_STRIDE * NSA_DH)

    def pe_rows(pe):
        return jnp.broadcast_to(bf(pe).reshape(1, -1), (8, CMP_LEN * NSA_DH))

    def swap_pairs(w):
        return w.reshape(NSA_HEADS // 2, 2, NSA_DH, -1)[:, ::-1].reshape(w.shape)

    def widen(w, outer):
        z = jnp.zeros_like(w)
        return bf(jnp.concatenate([w, z, z, w] if outer else [z, w, w, z], axis=1))

    x2 = x.reshape(T, D)
    for i in range(depth):
        y_ret, nq, cmp_kv, kv, gates = _proj(S, x2, posf, row(norm_mix_g[i]), inv_freq, row(ret_gn_g[i]),
                                              _pack_w_in(w_in[i]))
        kc, vc = _compress(strides(cmp_kv[:, :LANES]), strides(cmp_kv[:, LANES:]),
                           bf(cmp_k_w1[i]), widen(cmp_k_w2[i], True), pe_rows(cmp_pe_k[i]),
                           bf(cmp_v_w1[i]), widen(cmp_v_w2[i], False), pe_rows(cmp_pe_v[i]))
        sh3 = lambda a: a.reshape(B, S, a.shape[-1])
        y_nsa = _nsa(sh3(nq), sh3(gates), kv.reshape(G, B, S, kv.shape[-1]), kc, vc)
        x2 = _tail(i == depth - 1, x2, y_ret, y_nsa.reshape(T, NSA_Q_W),
                   p[i].reshape(T, PLE_DIM), row(norm_mix_g[i]), row(norm_mlp_g[i]), row(norm_ple_g[i]),
                   row(norm_final_g), bf(w_merge_gate[i]), bf(w_ret_o[i]), bf(swap_pairs(w_nsa_o[i])), bf(w_out[i]),
                   bf(w_mlp_up[i]), bf(w_mlp_down[i]), bf(w_ple_gate[i]), bf(w_ple_proj[i]))
    return x2.reshape(B, S, D)
```

```python
import functools
import math

import jax
import jax.numpy as jnp
from jax import lax
from jax.experimental import pallas as pl
from jax.experimental.pallas import tpu as pltpu

F32 = jnp.float32
BF16 = jnp.bfloat16

D_MODEL = 1024
PLE_DIM = 256
RMS_EPS = 1e-6
ROPE_THETA = 10000.0
RET_HEADS = 8
RET_DK = 128
RET_DV = 256
RET_CHUNK = 256
RET_QK_W = RET_HEADS * RET_DK
RET_V_W = RET_HEADS * RET_DV
NSA_HEADS = 16
NSA_GROUPS = 2
NSA_HPG = 8
NSA_DH = 64
NSA_Q_W = NSA_HEADS * NSA_DH
NSA_GQ_W = NSA_HPG * NSA_DH
HEAD_PAIRS = NSA_HPG // 2
Q_STACKS = 3
CMP_LEN = 32
CMP_STRIDE = 16
CMP_HIDDEN = 256
SEL_BLOCK = 64
SEL_TOPN = 8
WINDOW = 512
FORCE_SCORE = 1e6
MLP_HIDDEN = 4 * D_MODEL

LANES = 128
MASK_BIAS = -1e30
MAX_FLOOR = -1e29
LOG2E = math.log2(math.e)
VMEM_LIMIT = 56 * 1024 * 1024

PROJ_TM = 256
TAIL_TM = 512
NSA_TQ = 256
NSA_KT = 512
NSA_WK = WINDOW + NSA_TQ

C_RQ, C_RK, C_RV, C_RG = 0, 1024, 2048, 4096
C_NQ, C_KV, C_GATE, C_END = 6144, 7168, 7936, 8192
GATE_W = NSA_GROUPS * LANES
KV_TILE = {2: 0, 3: 1, 4: 3, 5: 4}
KV_TILES = 6


def _resident(shape):
    nd = len(shape)
    return pl.BlockSpec(shape, lambda *_: (0,) * nd, pipeline_mode=pl.Buffered(1))


def _rms(x, g):
    return x * lax.rsqrt(jnp.mean(x * x, axis=-1, keepdims=True) + RMS_EPS) * g


def _dot(a, b):
    return jnp.dot(a, b, preferred_element_type=F32)


def _dot_tb(a, b):
    return lax.dot_general(a, b, (((1,), (1,)), ((), ())), preferred_element_type=F32)


def _dot_ta(a, b):
    return lax.dot_general(a, b, (((0,), (0,)), ((), ())), preferred_element_type=F32)


def _proj_kernel(seq_len, x_ref, pos_ref, g_ref, inv_ref, gn_ref, w_ref,
                 yret_ref, nq_ref, cmp_ref, kv_ref, gate_ref,
                 rq_ref, rk_ref, rv_ref, rg_ref, state_ref, decay_ref, xi_ref, zeta_ref):
    tm = x_ref.shape[0]
    first_row = pl.program_id(0) * tm

    @pl.when(first_row == 0)
    def _tables():
        _retention_tables(decay_ref, xi_ref, zeta_ref)
        state_ref[...] = jnp.zeros_like(state_ref)

    h = _rms(x_ref[...], g_ref[...]).astype(BF16)
    pos = pos_ref[...]
    lane = lax.broadcasted_iota(jnp.int32, (tm, LANES), 1)
    ang = pos * inv_ref[...]
    cos_a = jnp.cos(ang)
    sin_a = jnp.sin(ang)
    HR = RET_DK // 2
    cos_r = jnp.where(lane < HR, cos_a, pltpu.roll(cos_a, HR, 1))
    sin_r = jnp.where(lane < HR, -sin_a, pltpu.roll(sin_a, HR, 1))

    HN = NSA_DH // 2
    assert RET_DK == LANES and 2 * NSA_DH == LANES

    def tile_nsa(t):
        return jnp.where(lane < HN, pltpu.roll(t, LANES - HR, 1),
                         jnp.where(lane < 2 * HN, pltpu.roll(t, LANES - HN, 1),
                                   jnp.where(lane < 3 * HN, t, pltpu.roll(t, HN, 1))))

    low = (lane & HN) == 0
    cos_n = tile_nsa(cos_a)
    sin_n = tile_nsa(sin_a)
    sin_n = jnp.where(low, -sin_n, sin_n)

    def rope_r(y):
        return y * cos_r + pltpu.roll(y, HR, 1) * sin_r

    def rope_n(y):
        partner = jnp.where(low, pltpu.roll(y, LANES - HN, 1), pltpu.roll(y, HN, 1))
        return y * cos_n + partner * sin_n

    seq_pos = first_row % seq_len + lax.broadcasted_iota(jnp.int32, (tm, LANES), 0)
    sel_tag = jnp.where(lane - NSA_DH == seq_pos // SEL_BLOCK, MASK_BIAS, 0.0)

    k_scale = RET_DK ** -0.5
    q_scale = NSA_DH ** -0.5 * LOG2E
    chunk = 512
    heads_per_chunk = chunk // RET_DV
    plan = []
    for v_chunk in range(RET_V_W // chunk):
        plan.append(C_RV + v_chunk * chunk)
        if (v_chunk * heads_per_chunk) % (chunk // RET_DK) == 0:
            qk_chunk = v_chunk * heads_per_chunk * RET_DK
            plan += [C_RQ + qk_chunk, C_RK + qk_chunk]
        plan.append(C_RG + v_chunk * chunk)
        plan.append(tuple(range(v_chunk * heads_per_chunk, (v_chunk + 1) * heads_per_chunk)))
    plan += list(range(C_NQ, C_END, chunk))
    for c0 in plan:
        if isinstance(c0, tuple):
            _retention_heads(c0, first_row % seq_len == 0, rq_ref, rk_ref, rv_ref, rg_ref, gn_ref, yret_ref,
                             state_ref, decay_ref, xi_ref, zeta_ref)
            continue
        y = _dot(h, w_ref[:, c0:c0 + chunk])
        for j in range(chunk // LANES):
            col = c0 + j * LANES
            piece = y[:, j * LANES:(j + 1) * LANES]
            if col < C_RK:
                rq_ref[:, col - C_RQ:col - C_RQ + LANES] = rope_r(piece).astype(BF16)
            elif col < C_RV:
                rk_ref[:, col - C_RK:col - C_RK + LANES] = (rope_r(piece) * k_scale).astype(BF16)
            elif col < C_RG:
                rv_ref[:, col - C_RV:col - C_RV + LANES] = piece.astype(BF16)
            elif col < C_NQ:
                rg_ref[:, col - C_RG:col - C_RG + LANES] = piece.astype(BF16)
            elif col < C_KV:
                val = rope_n(piece) * q_scale
                pair = (col - C_NQ) // LANES
                base = (pair // HEAD_PAIRS) * Q_STACKS * NSA_GQ_W + (pair % HEAD_PAIRS) * LANES
                zeros = jnp.zeros_like(val)
                nq_ref[:, base:base + LANES] = jnp.where(lane < NSA_DH, val, zeros).astype(BF16)
                nq_ref[:, base + NSA_GQ_W:base + NSA_GQ_W + LANES] = jnp.where(
                    lane < NSA_DH, pltpu.roll(val, NSA_DH, 1), zeros).astype(BF16)
                nq_ref[:, base + 2 * NSA_GQ_W:base + 2 * NSA_GQ_W + LANES] = val.astype(BF16)
            elif col < C_GATE:
                j_kv = (col - C_KV) // LANES
                is_key = j_kv % 2 == 0
                val = rope_n(piece) if is_key else piece
                if j_kv < 2:
                    cmp_ref[:, j_kv * LANES:(j_kv + 1) * LANES] = val.astype(BF16)
                    continue
                swapped = pltpu.roll(val, NSA_DH, 1)
                lower = lane < NSA_DH
                grouped = (val, swapped), (swapped, val)
                t0 = KV_TILE[j_kv]
                for g in range(NSA_GROUPS):
                    lo, hi = grouped[g]
                    if is_key:
                        fill = sel_tag if j_kv == 2 else jnp.zeros_like(val)
                        tiles = (jnp.where(lower, lo, fill),)
                    else:
                        tiles = (jnp.where(lower, 1.0, hi), jnp.where(lower, lo, 1.0))
                    for k, tile in enumerate(tiles):
                        kv_ref[g, :, (t0 + k) * LANES:(t0 + k + 1) * LANES] = tile.astype(BF16)
            else:
                gate_ref[:, col - C_GATE:col - C_GATE + LANES] = jax.nn.sigmoid(piece)


def _proj(seq_len, x2, posf, g_mix, inv_freq, gn_g, w_all):
    T = x2.shape[0]
    tm = PROJ_TM
    C = RET_CHUNK
    assert seq_len % tm == 0 and tm % C == 0
    row = lambda w: pl.BlockSpec((tm, w), lambda i: (i, 0))
    kv_w = KV_TILES * LANES
    out_shapes = [
        jax.ShapeDtypeStruct((T, RET_V_W), BF16),
        jax.ShapeDtypeStruct((T, Q_STACKS * NSA_Q_W), BF16),
        jax.ShapeDtypeStruct((T, 2 * LANES), BF16),
        jax.ShapeDtypeStruct((NSA_GROUPS, T, kv_w), BF16),
        jax.ShapeDtypeStruct((T, GATE_W), F32),
    ]
    tables = [pltpu.VMEM((RET_HEADS, C, w), F32) for w in (C, RET_DV, RET_DK)]
    return pl.pallas_call(
        functools.partial(_proj_kernel, seq_len),
        grid=(T // tm,),
        in_specs=[row(D_MODEL), row(1), _resident((1, D_MODEL)), _resident((1, LANES)), _resident((1, RET_V_W)),
                  _resident((D_MODEL, C_END))],
        out_specs=[row(RET_V_W), row(Q_STACKS * NSA_Q_W), row(2 * LANES),
                   pl.BlockSpec((NSA_GROUPS, tm, kv_w), lambda i: (0, i, 0)), row(GATE_W)],
        out_shape=out_shapes,
        scratch_shapes=[pltpu.VMEM((tm, RET_QK_W), BF16), pltpu.VMEM((tm, RET_QK_W), BF16),
                        pltpu.VMEM((tm, RET_V_W), BF16), pltpu.VMEM((tm, RET_V_W), BF16),
                        pltpu.VMEM((RET_HEADS, RET_DK, RET_DV), F32)] + tables,
        compiler_params=pltpu.CompilerParams(dimension_semantics=("arbitrary",),
                                             vmem_limit_bytes=VMEM_LIMIT),
        name="proj",
    )(x2, posf, g_mix, inv_freq, gn_g, w_all)


def _compress_kernel(k16_ref, v16_ref, w1k_ref, w2k_ref, pek_ref, w1v_ref, w2v_ref, pev_ref,
                     kc_ref, vc_ref):
    half = CMP_STRIDE * NSA_DH
    for x_ref, w1_ref, w2_ref, pe_ref, o_ref in ((k16_ref, w1k_ref, w2k_ref, pek_ref, kc_ref),
                                                 (v16_ref, w1v_ref, w2v_ref, pev_ref, vc_ref)):
        x = x_ref[0, 0]
        first = _dot(x, w1_ref[0:half, :])
        second = _dot(x, w1_ref[half:2 * half, :])
        pe_term = _dot(pe_ref[...], w1_ref[...])[0:1, :]
        hidden = first + pltpu.roll(second, second.shape[0] - 1, 0) + pe_term
        act = jax.nn.gelu(hidden).astype(BF16)
        both = _dot(act, w2_ref[...]).astype(BF16)
        n_cmp = both.shape[0]
        o_ref[0, 0, 0:n_cmp, :] = both[:, 0:LANES]
        o_ref[0, 0, n_cmp:2 * n_cmp, :] = both[:, LANES:2 * LANES]


def _compress(k16, v16, w1k, w2k, pek, w1v, w2v, pev):
    B, G, R, W = k16.shape
    blk = pl.BlockSpec((1, 1, R, W), lambda b, g: (b, g, 0, 0))
    oblk = pl.BlockSpec((1, 1, 2 * R, LANES), lambda b, g: (b, g, 0, 0))
    out = jax.ShapeDtypeStruct((B, G, 2 * R, LANES), BF16)
    wspecs = [_resident(w1k.shape), _resident(w2k.shape), _resident(pek.shape)]
    return pl.pallas_call(
        _compress_kernel,
        grid=(B, G),
        in_specs=[blk, blk] + wspecs + wspecs,
        out_specs=[oblk, oblk],
        out_shape=[out, out],
        compiler_params=pltpu.CompilerParams(dimension_semantics=("arbitrary", "arbitrary"),
                                             vmem_limit_bytes=VMEM_LIMIT),
        name="compress",
    )(k16, v16, w1k, w2k, pek, w1v, w2v, pev)


_RET_LOG_G = [math.log(1.0 - 2.0 ** (-5.0 - h)) for h in range(RET_HEADS)]


def _retention_tables(decay_ref, xi_ref, zeta_ref):
    C = RET_CHUNK
    r = lax.broadcasted_iota(jnp.int32, (C, C), 0).astype(F32)
    c = lax.broadcasted_iota(jnp.int32, (C, C), 1).astype(F32)
    diff = r - c
    r_out = lax.broadcasted_iota(jnp.int32, (C, RET_DV), 0).astype(F32)
    r_key = lax.broadcasted_iota(jnp.int32, (C, RET_DK), 0).astype(F32)
    for h in range(RET_HEADS):
        lg = _RET_LOG_G[h]
        decay_ref[h] = jnp.where(diff >= 0, jnp.exp(jnp.maximum(diff, 0.0) * lg), 0.0)
        xi_ref[h] = jnp.exp((r_out + 1.0) * lg)
        zeta_ref[h] = jnp.exp((C - 1.0 - r_key) * lg)


def _retention_heads(heads, new_sequence, q_ref, k_ref, v_ref, g_ref, gn_ref, y_ref, state_ref, decay_ref, xi_ref,
                     zeta_ref):
    C = RET_CHUNK
    n_chunks = q_ref.shape[0] // C
    for h in heads:
        qs = slice(h * RET_DK, (h + 1) * RET_DK)
        vs = slice(h * RET_DV, (h + 1) * RET_DV)
        xi = xi_ref[h]
        state = jnp.where(new_sequence, 0.0, state_ref[h])
        for c in range(n_chunks):
            tok = slice(c * C, (c + 1) * C)
            qh = q_ref[tok, qs]
            kh = k_ref[tok, qs]
            vh = v_ref[tok, vs]
            inner = (_dot_tb(qh, kh) * decay_ref[h]).astype(BF16)
            o = _dot(inner, vh) + _dot(qh, state.astype(BF16)) * xi
            kz = (kh.astype(F32) * zeta_ref[h]).astype(BF16)
            state = math.exp(C * _RET_LOG_G[h]) * state + _dot_ta(kz, vh)
            mu = jnp.mean(o, axis=-1, keepdims=True)
            d = o - mu
            var = jnp.mean(d * d, axis=-1, keepdims=True)
            y = d * lax.rsqrt(var + RMS_EPS) * gn_ref[:, vs]
            g = g_ref[tok, vs].astype(F32)
            y_ref[tok, vs] = (y * (g * jax.nn.sigmoid(g))).astype(BF16)
        state_ref[h] = state


NOTSEL_LANE0 = NSA_DH


def _nsa_kernel(q_ref, gate_ref, ksel_ref, vsel_e_ref, vsel_o_ref, kwin_ref, vwin_e_ref, vwin_o_ref,
                kc_ref, vc_ref, wbias_ref, cbias_ref, o_ref,
                acc_e, acc_o, m_e, m_o, out_acc, s_next, gate_tiles):
    TQ, KT, WK = NSA_TQ, NSA_KT, NSA_WK
    HP = HEAD_PAIRS
    n_blk = ksel_ref.shape[2] // SEL_BLOCK
    qt = pl.program_id(2)
    q0 = qt * TQ
    q_plain = tuple(jnp.concatenate([q_ref[0, :, (x * HP + hp) * LANES:(x * HP + hp + 1) * LANES]
                                     for hp in range(HP)], axis=0) for x in range(2))
    t_col = q0 + lax.broadcasted_iota(jnp.int32, (TQ, 1), 0)
    lane = lax.broadcasted_iota(jnp.int32, (TQ, LANES), 1)
    lower_half = lane < NSA_DH
    rows = [slice(hp * TQ, (hp + 1) * TQ) for hp in range(HP)]
    gates = gate_ref[0]

    for branch in range(3):
        for hp in range(HP):
            c = branch * NSA_HPG + hp * 2
            gate_tiles[branch * HP + hp] = jnp.take_along_axis(gates, jnp.where(lower_half, c + 1, c), axis=1)

    def emit(branch, acc_pair):
        for hp in range(HP):
            a_e = acc_pair[0][rows[hp]]
            a_o = acc_pair[1][rows[hp]]
            weight = gate_tiles[branch * HP + hp] / pltpu.roll(jnp.where(lower_half, a_e, a_o), NSA_DH, 1)
            out_acc[rows[hp]] += jnp.where(lower_half, a_o, a_e) * weight

    c_bias = jnp.where(lane * CMP_STRIDE + (CMP_LEN - 1) <= t_col, 0.0, MASK_BIAS)
    q_pairs = jnp.concatenate([q_ref[0, :, (2 * HP + hp) * LANES:(2 * HP + hp + 1) * LANES]
                               for hp in range(HP)], axis=0)
    s = _dot_tb(q_pairs, kc_ref[0, 0])
    p_sum = jnp.zeros((TQ, LANES), F32)
    ps = []
    for hp in range(HP):
        halves = []
        for x in range(2):
            sh = s[rows[hp], x * LANES:(x + 1) * LANES] + c_bias
            m = jnp.maximum(jnp.max(sh, axis=-1, keepdims=True), MAX_FLOOR)
            e = jnp.exp2(sh - m)
            l = jnp.sum(e, axis=-1, keepdims=True)
            p = e * jnp.where(l > 0, 1.0 / l, 0.0)
            p_sum = p_sum + p
            halves.append(p.astype(BF16))
        ps.append(jnp.concatenate(halves, axis=1))
    cmp_out = _dot(jnp.concatenate(ps, axis=0), vc_ref[0, 0])
    for hp in range(HP):
        out_acc[rows[hp]] = cmp_out[rows[hp]] * gate_tiles[hp]

    w0 = pl.multiple_of(jnp.maximum(q0 - WINDOW, 0), TQ)
    w_bias = wbias_ref[jnp.minimum(qt, WINDOW // TQ)]
    kk = kwin_ref[0, 0, pl.ds(w0, WK), :]
    win_acc = []
    for qx, v_ref in zip(q_plain, (vwin_e_ref, vwin_o_ref)):
        s = _dot_tb(qx, kk)
        es = []
        for hp in range(HP):
            sh = s[rows[hp]] + w_bias
            es.append(jnp.exp2(sh - jnp.max(sh, axis=-1, keepdims=True)).astype(BF16))
        win_acc.append(_dot(jnp.concatenate(es, axis=0), v_ref[0, 0, pl.ds(w0, WK), :]))
    emit(2, win_acc)

    ni = lax.broadcasted_iota(jnp.int32, (LANES, LANES), 0)
    ci = lax.broadcasted_iota(jnp.int32, (LANES, LANES), 1)
    overlap_t = ((ci * CMP_STRIDE < ni * SEL_BLOCK + SEL_BLOCK)
                 & (ci * CMP_STRIDE + CMP_LEN - 1 >= ni * SEL_BLOCK) & (ni < n_blk))
    overlap_t = jnp.where(overlap_t, 1.0, 0.0).astype(BF16)
    p_hi = p_sum.astype(BF16)
    p_lo = (p_sum - p_hi.astype(F32)).astype(BF16)
    imp_t = (_dot_tb(overlap_t, p_hi) + _dot_tb(overlap_t, p_lo))[0:n_blk]
    blk_id = lax.broadcasted_iota(jnp.int32, (n_blk, TQ), 0)
    cur = (q0 + lax.broadcasted_iota(jnp.int32, (n_blk, TQ), 1)) // SEL_BLOCK
    forced = (blk_id == 0) | (blk_id == cur) | (blk_id == cur - 1)
    score = jnp.where(forced, FORCE_SCORE, jnp.where(blk_id <= cur, imp_t, -1.0))
    rank = jnp.zeros((n_blk, TQ), jnp.int32)
    for mblk in range(n_blk):
        other = score[mblk:mblk + 1, :]
        ahead = (other > score) | ((other == score) & (blk_id > mblk))
        rank = rank + jnp.where(ahead, 1, 0)
    not_sel_t = jnp.where(rank < SEL_TOPN, 0.0, 1.0)
    padded = jnp.concatenate([jnp.zeros((NOTSEL_LANE0, TQ), F32), not_sel_t,
                              jnp.zeros((LANES - NOTSEL_LANE0 - n_blk, TQ), F32)], axis=0)
    not_sel = jnp.concatenate([padded.T.astype(BF16)] * HP, axis=0)
    q_aug = tuple(qx + not_sel for qx in q_plain)

    for ref in (acc_e, acc_o):
        ref[...] = jnp.zeros_like(ref)
    for ref in (m_e, m_o):
        ref[...] = jnp.full_like(ref, MASK_BIAS)

    last_k0 = ksel_ref.shape[2] - KT

    def even_scores(kt):
        k0 = pl.multiple_of(jnp.minimum(kt * KT, last_k0), KT)
        return _dot_tb(q_aug[0], ksel_ref[0, 0, pl.ds(k0, KT), :])

    def softmax_pv(get_scores, bias, vv, acc, m_ref):
        es, alphas = [], []
        for hp in range(HP):
            sh = get_scores(hp) + bias
            tiles = [sh[:, j * LANES:(j + 1) * LANES] for j in range(bias.shape[1] // LANES)]
            m_old = m_ref[rows[hp]]
            m_new = jnp.maximum(m_old, jnp.max(functools.reduce(jnp.maximum, tiles), axis=-1, keepdims=True))
            m_ref[rows[hp]] = m_new
            alphas.append(jnp.exp2(m_old - m_new))
            es.append(jnp.concatenate([jnp.exp2(t - m_new).astype(BF16) for t in tiles], axis=1))
        pv = _dot(jnp.concatenate(es, axis=0), vv)
        for hp in range(HP):
            acc[rows[hp]] = alphas[hp] * acc[rows[hp]] + pv[rows[hp]]

    s_next[...] = even_scores(0)

    def sel_tile(kt):
        k0 = pl.multiple_of(kt * KT, KT)
        bias = cbias_ref[jnp.minimum(qt - kt * (KT // TQ), KT // TQ)]
        s_odd = _dot_tb(q_aug[1], ksel_ref[0, 0, pl.ds(k0, KT), :])
        softmax_pv(lambda hp: s_next[rows[hp], :], bias, vsel_e_ref[0, 0, pl.ds(k0, KT), :], acc_e, m_e)
        s_next[...] = even_scores(kt + 1)
        softmax_pv(lambda hp: s_odd[rows[hp]], bias, vsel_o_ref[0, 0, pl.ds(k0, KT), :], acc_o, m_o)

    n_whole = (q0 + TQ) // KT

    def tile_pair(i, carry):
        sel_tile(2 * i)
        sel_tile(2 * i + 1)
        return carry

    lax.fori_loop(0, n_whole // 2, tile_pair, 0)

    @pl.when(n_whole % 2 == 1)
    def _odd_tile():
        sel_tile(n_whole - 1)

    @pl.when(n_whole * KT < q0 + TQ)
    def _diagonal_remainder():
        k_rem = pl.multiple_of(q0, TQ)
        kk = ksel_ref[0, 0, pl.ds(k_rem, TQ), :]
        bias = cbias_ref[0][:, 0:TQ]
        for qx, v_ref, acc, m_ref in ((q_aug[0], vsel_e_ref, acc_e, m_e), (q_aug[1], vsel_o_ref, acc_o, m_o)):
            s = _dot_tb(qx, kk)
            softmax_pv(lambda hp: s[rows[hp]], bias, v_ref[0, 0, pl.ds(k_rem, TQ), :], acc, m_ref)

    emit(1, (acc_e, acc_o))

    o_ref[0] = jnp.concatenate([out_acc[rows[hp]] for hp in range(HP)], axis=1).astype(BF16)


def _nsa(nq, gates, kv, kc, vc):
    B, S, _ = nq.shape
    G = NSA_GROUPS
    TQ = NSA_TQ
    assert S // SEL_BLOCK <= LANES - NOTSEL_LANE0 and S % NSA_KT == 0 and S >= NSA_WK
    assert NSA_KT == 2 * TQ and WINDOW % TQ == 0
    rows = HEAD_PAIRS * TQ
    i = jnp.arange(TQ, dtype=jnp.int32)[None, :, None]
    off = lambda n: jnp.arange(n + 1, dtype=jnp.int32)[:, None, None] * TQ
    j = jnp.arange(NSA_WK, dtype=jnp.int32)[None, None, :]
    t_rel = off(WINDOW // TQ) + i
    win_bias = jnp.where((j <= t_rel) & (j > t_rel - WINDOW), 0.0, MASK_BIAS).astype(F32)
    j = jnp.arange(NSA_KT, dtype=jnp.int32)[None, None, :]
    causal_bias = jnp.where(j <= off(NSA_KT // TQ) + i, 0.0, MASK_BIAS).astype(F32)
    qblk = pl.BlockSpec((1, TQ, Q_STACKS * NSA_GQ_W), lambda b, g, t: (b, t, g))
    oblk = pl.BlockSpec((1, TQ, NSA_GQ_W), lambda b, g, t: (b, t, g))
    gblk = pl.BlockSpec((1, TQ, LANES), lambda b, g, t: (b, t, g))
    kvblk = lambda j: pl.BlockSpec((1, 1, S, LANES), lambda b, g, t: (g, b, 0, j))
    cblk = pl.BlockSpec((1, 1, 2 * LANES, LANES), lambda b, g, t: (b, g, 0, 0))
    wide = pltpu.VMEM((rows, LANES), F32)
    return pl.pallas_call(
        _nsa_kernel,
        grid=(B, G, S // TQ),
        in_specs=([qblk, gblk] + [kvblk(j) for j in range(KV_TILES)] + [cblk, cblk]
                  + [_resident(win_bias.shape), _resident(causal_bias.shape)]),
        out_specs=oblk,
        out_shape=jax.ShapeDtypeStruct((B, S, NSA_Q_W), BF16),
        scratch_shapes=[wide] * 5 + [pltpu.VMEM((rows, NSA_KT), F32),
                                     pltpu.VMEM((3 * NSA_HPG // 2, TQ, LANES), F32)],
        compiler_params=pltpu.CompilerParams(dimension_semantics=("arbitrary",) * 3,
                                             vmem_limit_bytes=VMEM_LIMIT),
        name="nsa",
    )(nq, gates, *([kv] * KV_TILES), kc, vc, win_bias, causal_bias)


def _tail_kernel(final, x_ref, yr_ref, yn_ref, p_ref, gmix_ref, gmlp_ref, gple_ref, gfin_ref,
                 wmg_ref, wro_ref, wno_ref, wout_ref, wup_ref, wdn_ref, wpg_ref, wpp_ref, o_ref):
    x = x_ref[...]
    h = _rms(x, gmix_ref[...]).astype(BF16)
    o_ret = _dot(yr_ref[...], wro_ref[...])
    o_nsa = _dot(yn_ref[...], wno_ref[...])
    g_ret = jax.nn.sigmoid(_dot(h, wmg_ref[:, 0:D_MODEL]))
    g_nsa = jax.nn.sigmoid(_dot(h, wmg_ref[:, D_MODEL:2 * D_MODEL]))
    mix = (g_ret * o_ret + g_nsa * o_nsa).astype(BF16)
    x = x + _dot(mix, wout_ref[...])
    h2 = _rms(x, gmlp_ref[...]).astype(BF16)
    mlp = jnp.zeros_like(x)
    step = 1024
    for c0 in range(0, MLP_HIDDEN, step):
        up = jnp.maximum(_dot(h2, wup_ref[:, c0:c0 + step]), 0.0)
        mlp = mlp + _dot((up * up).astype(BF16), wdn_ref[c0:c0 + step, :])
    x = x + mlp
    h3 = _rms(x, gple_ref[...]).astype(BF16)
    ple_gate = jax.nn.sigmoid(_dot(h3, wpg_ref[...]))
    x = x + _dot(p_ref[...].astype(BF16), wpp_ref[...]) * ple_gate
    if final:
        x = _rms(x, gfin_ref[...])
    o_ref[...] = x


def _tail(final, x2, y_ret, y_nsa, p2, g_mix, g_mlp, g_ple, g_fin, w_mg, w_ro, w_no, w_out, w_up, w_dn,
          w_pg, w_pp):
    T = x2.shape[0]
    tm = TAIL_TM
    row = lambda w: pl.BlockSpec((tm, w), lambda i: (i, 0))
    gains = [_resident((1, D_MODEL))] * 4
    weights = [_resident(w.shape) for w in (w_mg, w_ro, w_no, w_out, w_up, w_dn, w_pg, w_pp)]
    return pl.pallas_call(
        functools.partial(_tail_kernel, final),
        grid=(T // tm,),
        in_specs=[row(D_MODEL), row(RET_V_W), row(NSA_Q_W), row(PLE_DIM)] + gains + weights,
        out_specs=row(D_MODEL),
        out_shape=jax.ShapeDtypeStruct((T, D_MODEL), F32),
        compiler_params=pltpu.CompilerParams(dimension_semantics=("arbitrary",),
                                             vmem_limit_bytes=VMEM_LIMIT),
        name="tail",
    )(x2, y_ret, y_nsa, p2, g_mix, g_mlp, g_ple, g_fin, w_mg, w_ro, w_no, w_out, w_up, w_dn, w_pg, w_pp)


def _pack_w_in(w):
    gate = w[:, C_GATE:C_GATE + 3 * NSA_HEADS]
    parts = [w[:, :C_GATE]]
    for g in range(NSA_GROUPS):
        cols = [j * NSA_HEADS + g * NSA_HPG + r for j in range(3) for r in range(NSA_HPG)]
        parts.append(jnp.pad(gate[:, jnp.array(cols)], ((0, 0), (0, LANES - len(cols)))))
    return jnp.concatenate(parts, axis=1).astype(BF16)


def kernel(x, p, positions, norm_mix_g, w_in, ret_gn_g, w_ret_o, cmp_pe_k, cmp_k_w1, cmp_k_w2, cmp_pe_v, cmp_v_w1, cmp_v_w2, w_nsa_o, w_merge_gate, w_out, norm_mlp_g, w_mlp_up, w_mlp_down, norm_ple_g, w_ple_gate, w_ple_proj, norm_final_g):
    B, S, D = x.shape
    depth = p.shape[0]
    T = B * S
    G = NSA_GROUPS
    bf = lambda a: a.astype(BF16)
    row = lambda a: a.reshape(1, -1)
    posf = positions.reshape(T, 1).astype(F32)
    inv_r = ROPE_THETA ** (-jnp.arange(0, RET_DK, 2, dtype=F32) / RET_DK)
    inv_n = ROPE_THETA ** (-jnp.arange(0, NSA_DH, 2, dtype=F32) / NSA_DH)
    inv_freq = jnp.concatenate([inv_r, inv_n, jnp.zeros((LANES - inv_r.shape[0] - inv_n.shape[0],), F32)])
    inv_freq = inv_freq.reshape(1, LANES)
    n_rows = S // CMP_STRIDE

    def strides(t):
        t = t.reshape(B, S, G, NSA_DH).transpose(0, 2, 1, 3)
        return t.reshape(B, G, n_rows, CMP_STRIDE * NSA_DH)

    def pe_rows(pe):
        return jnp.broadcast_to(bf(pe).reshape(1, -1), (8, CMP_LEN * NSA_DH))

    def swap_pairs(w):
        return w.reshape(NSA_HEADS // 2, 2, NSA_DH, -1)[:, ::-1].reshape(w.shape)

    def widen(w, outer):
        z = jnp.zeros_like(w)
        return bf(jnp.concatenate([w, z, z, w] if outer else [z, w, w, z], axis=1))

    x2 = x.reshape(T, D)
    for i in range(depth):
        y_ret, nq, cmp_kv, kv, gates = _proj(S, x2, posf, row(norm_mix_g[i]), inv_freq, row(ret_gn_g[i]),
                                              _pack_w_in(w_in[i]))
        kc, vc = _compress(strides(cmp_kv[:, :LANES]), strides(cmp_kv[:, LANES:]),
                           bf(cmp_k_w1[i]), widen(cmp_k_w2[i], True), pe_rows(cmp_pe_k[i]),
                           bf(cmp_v_w1[i]), widen(cmp_v_w2[i], False), pe_rows(cmp_pe_v[i]))
        sh3 = lambda a: a.reshape(B, S, a.shape[-1])
        y_nsa = _nsa(sh3(nq), sh3(gates), kv.reshape(G, B, S, kv.shape[-1]), kc, vc)
        x2 = _tail(i == depth - 1, x2, y_ret, y_nsa.reshape(T, NSA_Q_W),
                   p[i].reshape(T, PLE_DIM), row(norm_mix_g[i]), row(norm_mlp_g[i]), row(norm_ple_g[i]),
                   row(norm_final_g), bf(w_merge_gate[i]), bf(w_ret_o[i]), bf(swap_pairs(w_nsa_o[i])), bf(w_out[i]),
                   bf(w_mlp_up[i]), bf(w_mlp_down[i]), bf(w_ple_gate[i]), bf(w_ple_proj[i]))
    return x2.reshape(B, S, D)
```

```python
import functools
import math

import jax
import jax.numpy as jnp
from jax import lax
from jax.experimental import pallas as pl
from jax.experimental.pallas import tpu as pltpu

F32 = jnp.float32
BF16 = jnp.bfloat16

D_MODEL = 1024
PLE_DIM = 256
RMS_EPS = 1e-6
ROPE_THETA = 10000.0
RET_HEADS = 8
RET_DK = 128
RET_DV = 256
RET_CHUNK = 256
RET_QK_W = RET_HEADS * RET_DK
RET_V_W = RET_HEADS * RET_DV
NSA_HEADS = 16
NSA_GROUPS = 2
NSA_HPG = 8
NSA_DH = 64
NSA_Q_W = NSA_HEADS * NSA_DH
NSA_GQ_W = NSA_HPG * NSA_DH
HEAD_PAIRS = NSA_HPG // 2
Q_STACKS = 3
CMP_LEN = 32
CMP_STRIDE = 16
CMP_HIDDEN = 256
SEL_BLOCK = 64
SEL_TOPN = 8
WINDOW = 512
FORCE_SCORE = 1e6
MLP_HIDDEN = 4 * D_MODEL

LANES = 128
MASK_BIAS = -1e30
MAX_FLOOR = -1e29
LOG2E = math.log2(math.e)
VMEM_LIMIT = 56 * 1024 * 1024
PROJ_VMEM_LIMIT = 60 * 1024 * 1024

PROJ_TM = 512
TAIL_TM = 512
NSA_TQ = 256
NSA_KT = 512
NSA_WK = WINDOW + NSA_TQ

C_RQ, C_RK, C_RV, C_RG = 0, 1024, 2048, 4096
C_NQ, C_KV, C_GATE, C_END = 6144, 7168, 7936, 8192
GATE_W = NSA_GROUPS * LANES
KV_TILE = {2: 0, 3: 1, 4: 3, 5: 4}
KV_TILES = 6


def _resident(shape):
    nd = len(shape)
    return pl.BlockSpec(shape, lambda *_: (0,) * nd, pipeline_mode=pl.Buffered(1))


def _rms(x, g):
    return x * lax.rsqrt(jnp.mean(x * x, axis=-1, keepdims=True) + RMS_EPS) * g


def _dot(a, b):
    return jnp.dot(a, b, preferred_element_type=F32)


def _dot_tb(a, b):
    return lax.dot_general(a, b, (((1,), (1,)), ((), ())), preferred_element_type=F32)


def _dot_ta(a, b):
    return lax.dot_general(a, b, (((0,), (0,)), ((), ())), preferred_element_type=F32)


def _proj_kernel(seq_len, x_ref, pos_ref, g_ref, inv_ref, gn_ref, w_ref,
                 yret_ref, nq_ref, cmp_ref, kv_ref, gate_ref,
                 rq_ref, rk_ref, rv_ref, rg_ref, state_ref, decay_ref, xi_ref, zeta_ref):
    tm = x_ref.shape[0]
    first_row = pl.program_id(0) * tm

    @pl.when(first_row == 0)
    def _tables():
        _retention_tables(decay_ref, xi_ref, zeta_ref)
        state_ref[...] = jnp.zeros_like(state_ref)

    h = _rms(x_ref[...], g_ref[...]).astype(BF16)
    pos = pos_ref[...]
    lane = lax.broadcasted_iota(jnp.int32, (tm, LANES), 1)
    ang = pos * inv_ref[...]
    cos_a = jnp.cos(ang)
    sin_a = jnp.sin(ang)
    HR = RET_DK // 2
    cos_r = jnp.where(lane < HR, cos_a, pltpu.roll(cos_a, HR, 1))
    sin_r = jnp.where(lane < HR, -sin_a, pltpu.roll(sin_a, HR, 1))

    HN = NSA_DH // 2
    assert RET_DK == LANES and 2 * NSA_DH == LANES

    def tile_nsa(t):
        return jnp.where(lane < HN, pltpu.roll(t, LANES - HR, 1),
                         jnp.where(lane < 2 * HN, pltpu.roll(t, LANES - HN, 1),
                                   jnp.where(lane < 3 * HN, t, pltpu.roll(t, HN, 1))))

    low = (lane & HN) == 0
    cos_n = tile_nsa(cos_a)
    sin_n = tile_nsa(sin_a)
    sin_n = jnp.where(low, -sin_n, sin_n)

    def rope_r(y):
        return y * cos_r + pltpu.roll(y, HR, 1) * sin_r

    def rope_n(y):
        partner = jnp.where(low, pltpu.roll(y, LANES - HN, 1), pltpu.roll(y, HN, 1))
        return y * cos_n + partner * sin_n

    seq_pos = first_row % seq_len + lax.broadcasted_iota(jnp.int32, (tm, LANES), 0)
    sel_tag = jnp.where(lane - NSA_DH == seq_pos // SEL_BLOCK, MASK_BIAS, 0.0)

    k_scale = RET_DK ** -0.5
    q_scale = NSA_DH ** -0.5 * LOG2E
    chunk = 512
    heads_per_chunk = chunk // RET_DV
    plan = []
    for v_chunk in range(RET_V_W // chunk):
        plan.append(C_RV + v_chunk * chunk)
        if (v_chunk * heads_per_chunk) % (chunk // RET_DK) == 0:
            qk_chunk = v_chunk * heads_per_chunk * RET_DK
            plan += [C_RQ + qk_chunk, C_RK + qk_chunk]
        plan.append(C_RG + v_chunk * chunk)
    plan.append(tuple(range(RET_HEADS)))
    plan += list(range(C_NQ, C_END, chunk))
    for c0 in plan:
        if isinstance(c0, tuple):
            _retention_heads(c0, first_row % seq_len == 0, rq_ref, rk_ref, rv_ref, rg_ref, gn_ref, yret_ref,
                             state_ref, decay_ref, xi_ref, zeta_ref)
            continue
        y = _dot(h, w_ref[:, c0:c0 + chunk])
        for j in range(chunk // LANES):
            col = c0 + j * LANES
            piece = y[:, j * LANES:(j + 1) * LANES]
            if col < C_RK:
                rq_ref[:, col - C_RQ:col - C_RQ + LANES] = rope_r(piece).astype(BF16)
            elif col < C_RV:
                rk_ref[:, col - C_RK:col - C_RK + LANES] = (rope_r(piece) * k_scale).astype(BF16)
            elif col < C_RG:
                rv_ref[:, col - C_RV:col - C_RV + LANES] = piece.astype(BF16)
            elif col < C_NQ:
                rg_ref[:, col - C_RG:col - C_RG + LANES] = piece.astype(BF16)
            elif col < C_KV:
                val = rope_n(piece) * q_scale
                pair = (col - C_NQ) // LANES
                base = (pair // HEAD_PAIRS) * Q_STACKS * NSA_GQ_W + (pair % HEAD_PAIRS) * LANES
                zeros = jnp.zeros_like(val)
                nq_ref[:, base:base + LANES] = jnp.where(lane < NSA_DH, val, zeros).astype(BF16)
                nq_ref[:, base + NSA_GQ_W:base + NSA_GQ_W + LANES] = jnp.where(
                    lane < NSA_DH, pltpu.roll(val, NSA_DH, 1), zeros).astype(BF16)
                nq_ref[:, base + 2 * NSA_GQ_W:base + 2 * NSA_GQ_W + LANES] = val.astype(BF16)
            elif col < C_GATE:
                j_kv = (col - C_KV) // LANES
                is_key = j_kv % 2 == 0
                val = rope_n(piece) if is_key else piece
                if j_kv < 2:
                    cmp_ref[:, j_kv * LANES:(j_kv + 1) * LANES] = val.astype(BF16)
                    continue
                swapped = pltpu.roll(val, NSA_DH, 1)
                lower = lane < NSA_DH
                grouped = (val, swapped), (swapped, val)
                t0 = KV_TILE[j_kv]
                for g in range(NSA_GROUPS):
                    lo, hi = grouped[g]
                    if is_key:
                        fill = sel_tag if j_kv == 2 else jnp.zeros_like(val)
                        tiles = (jnp.where(lower, lo, fill),)
                    else:
                        tiles = (jnp.where(lower, 1.0, hi), jnp.where(lower, lo, 1.0))
                    for k, tile in enumerate(tiles):
                        kv_ref[g, :, (t0 + k) * LANES:(t0 + k + 1) * LANES] = tile.astype(BF16)
            else:
                gate_ref[:, col - C_GATE:col - C_GATE + LANES] = jax.nn.sigmoid(piece)


def _proj(seq_len, x2, posf, g_mix, inv_freq, gn_g, w_all):
    T = x2.shape[0]
    tm = PROJ_TM
    C = RET_CHUNK
    assert seq_len % tm == 0 and tm % C == 0
    row = lambda w: pl.BlockSpec((tm, w), lambda i: (i, 0))
    kv_w = KV_TILES * LANES
    out_shapes = [
        jax.ShapeDtypeStruct((T, RET_V_W), BF16),
        jax.ShapeDtypeStruct((T, Q_STACKS * NSA_Q_W), BF16),
        jax.ShapeDtypeStruct((T, 2 * LANES), BF16),
        jax.ShapeDtypeStruct((NSA_GROUPS, T, kv_w), BF16),
        jax.ShapeDtypeStruct((T, GATE_W), F32),
    ]
    tables = [pltpu.VMEM((RET_HEADS, C, w), F32) for w in (C, RET_DV, RET_DK)]
    return pl.pallas_call(
        functools.partial(_proj_kernel, seq_len),
        grid=(T // tm,),
        in_specs=[row(D_MODEL), row(1), _resident((1, D_MODEL)), _resident((1, LANES)), _resident((1, RET_V_W)),
                  _resident((D_MODEL, C_END))],
        out_specs=[row(RET_V_W), row(Q_STACKS * NSA_Q_W), row(2 * LANES),
                   pl.BlockSpec((NSA_GROUPS, tm, kv_w), lambda i: (0, i, 0)), row(GATE_W)],
        out_shape=out_shapes,
        scratch_shapes=[pltpu.VMEM((tm, RET_QK_W), BF16), pltpu.VMEM((tm, RET_QK_W), BF16),
                        pltpu.VMEM((tm, RET_V_W), BF16), pltpu.VMEM((tm, RET_V_W), BF16),
                        pltpu.VMEM((RET_HEADS, RET_DK, RET_DV), F32)] + tables,
        compiler_params=pltpu.CompilerParams(dimension_semantics=("arbitrary",),
                                             vmem_limit_bytes=PROJ_VMEM_LIMIT),
        name="proj",
    )(x2, posf, g_mix, inv_freq, gn_g, w_all)


def _compress_kernel(k16_ref, v16_ref, w1k_ref, w2k_ref, pek_ref, w1v_ref, w2v_ref, pev_ref,
                     kc_ref, vc_ref):
    half = CMP_STRIDE * NSA_DH
    for x_ref, w1_ref, w2_ref, pe_ref, o_ref in ((k16_ref, w1k_ref, w2k_ref, pek_ref, kc_ref),
                                                 (v16_ref, w1v_ref, w2v_ref, pev_ref, vc_ref)):
        x = x_ref[0, 0]
        first = _dot(x, w1_ref[0:half, :])
        second = _dot(x, w1_ref[half:2 * half, :])
        pe_term = _dot(pe_ref[...], w1_ref[...])[0:1, :]
        hidden = first + pltpu.roll(second, second.shape[0] - 1, 0) + pe_term
        act = jax.nn.gelu(hidden).astype(BF16)
        both = _dot(act, w2_ref[...]).astype(BF16)
        n_cmp = both.shape[0]
        o_ref[0, 0, 0:n_cmp, :] = both[:, 0:LANES]
        o_ref[0, 0, n_cmp:2 * n_cmp, :] = both[:, LANES:2 * LANES]


def _compress(k16, v16, w1k, w2k, pek, w1v, w2v, pev):
    B, G, R, W = k16.shape
    blk = pl.BlockSpec((1, 1, R, W), lambda b, g: (b, g, 0, 0))
    oblk = pl.BlockSpec((1, 1, 2 * R, LANES), lambda b, g: (b, g, 0, 0))
    out = jax.ShapeDtypeStruct((B, G, 2 * R, LANES), BF16)
    wspecs = [_resident(w1k.shape), _resident(w2k.shape), _resident(pek.shape)]
    return pl.pallas_call(
        _compress_kernel,
        grid=(B, G),
        in_specs=[blk, blk] + wspecs + wspecs,
        out_specs=[oblk, oblk],
        out_shape=[out, out],
        compiler_params=pltpu.CompilerParams(dimension_semantics=("arbitrary", "arbitrary"),
                                             vmem_limit_bytes=VMEM_LIMIT),
        name="compress",
    )(k16, v16, w1k, w2k, pek, w1v, w2v, pev)


_RET_LOG_G = [math.log(1.0 - 2.0 ** (-5.0 - h)) for h in range(RET_HEADS)]


def _retention_tables(decay_ref, xi_ref, zeta_ref):
    C = RET_CHUNK
    r = lax.broadcasted_iota(jnp.int32, (C, C), 0).astype(F32)
    c = lax.broadcasted_iota(jnp.int32, (C, C), 1).astype(F32)
    diff = r - c
    r_out = lax.broadcasted_iota(jnp.int32, (C, RET_DV), 0).astype(F32)
    r_key = lax.broadcasted_iota(jnp.int32, (C, RET_DK), 0).astype(F32)
    for h in range(RET_HEADS):
        lg = _RET_LOG_G[h]
        decay_ref[h] = jnp.where(diff >= 0, jnp.exp(jnp.maximum(diff, 0.0) * lg), 0.0)
        xi_ref[h] = jnp.exp((r_out + 1.0) * lg)
        zeta_ref[h] = jnp.exp((C - 1.0 - r_key) * lg)


def _retention_heads(heads, new_sequence, q_ref, k_ref, v_ref, g_ref, gn_ref, y_ref, state_ref, decay_ref, xi_ref,
                     zeta_ref):
    C = RET_CHUNK
    n_chunks = q_ref.shape[0] // C
    for h in heads:
        qs = slice(h * RET_DK, (h + 1) * RET_DK)
        vs = slice(h * RET_DV, (h + 1) * RET_DV)
        xi = xi_ref[h]
        state = jnp.where(new_sequence, 0.0, state_ref[h])
        for c in range(n_chunks):
            tok = slice(c * C, (c + 1) * C)
            qh = q_ref[tok, qs]
            kh = k_ref[tok, qs]
            vh = v_ref[tok, vs]
            inner = (_dot_tb(qh, kh) * decay_ref[h]).astype(BF16)
            o = _dot(inner, vh) + _dot(qh, state.astype(BF16)) * xi
            kz = (kh.astype(F32) * zeta_ref[h]).astype(BF16)
            state = math.exp(C * _RET_LOG_G[h]) * state + _dot_ta(kz, vh)
            mu = jnp.mean(o, axis=-1, keepdims=True)
            d = o - mu
            var = jnp.mean(d * d, axis=-1, keepdims=True)
            y = d * lax.rsqrt(var + RMS_EPS) * gn_ref[:, vs]
            g = g_ref[tok, vs].astype(F32)
            y_ref[tok, vs] = (y * (g * jax.nn.sigmoid(g))).astype(BF16)
        state_ref[h] = state


NOTSEL_LANE0 = NSA_DH


def _nsa_kernel(q_ref, gate_ref, ksel_ref, vsel_e_ref, vsel_o_ref, kwin_ref, vwin_e_ref, vwin_o_ref,
                kc_ref, vc_ref, wbias_ref, cbias_ref, o_ref,
                acc_e, acc_o, m_e, m_o, out_acc, s_next, gate_tiles):
    TQ, KT, WK = NSA_TQ, NSA_KT, NSA_WK
    HP = HEAD_PAIRS
    n_blk = ksel_ref.shape[2] // SEL_BLOCK
    qt = pl.program_id(2)
    q0 = qt * TQ
    q_plain = tuple(jnp.concatenate([q_ref[0, :, (x * HP + hp) * LANES:(x * HP + hp + 1) * LANES]
                                     for hp in range(HP)], axis=0) for x in range(2))
    t_col = q0 + lax.broadcasted_iota(jnp.int32, (TQ, 1), 0)
    lane = lax.broadcasted_iota(jnp.int32, (TQ, LANES), 1)
    lower_half = lane < NSA_DH
    rows = [slice(hp * TQ, (hp + 1) * TQ) for hp in range(HP)]
    gates = gate_ref[0]

    for branch in range(3):
        for hp in range(HP):
            c = branch * NSA_HPG + hp * 2
            gate_tiles[branch * HP + hp] = jnp.take_along_axis(gates, jnp.where(lower_half, c + 1, c), axis=1)

    def emit(branch, acc_pair):
        for hp in range(HP):
            a_e = acc_pair[0][rows[hp]]
            a_o = acc_pair[1][rows[hp]]
            weight = gate_tiles[branch * HP + hp] / pltpu.roll(jnp.where(lower_half, a_e, a_o), NSA_DH, 1)
            out_acc[rows[hp]] += jnp.where(lower_half, a_o, a_e) * weight

    c_bias = jnp.where(lane * CMP_STRIDE + (CMP_LEN - 1) <= t_col, 0.0, MASK_BIAS)
    q_pairs = jnp.concatenate([q_ref[0, :, (2 * HP + hp) * LANES:(2 * HP + hp + 1) * LANES]
                               for hp in range(HP)], axis=0)
    s = _dot_tb(q_pairs, kc_ref[0, 0])
    p_sum = jnp.zeros((TQ, LANES), F32)
    ps = []
    for hp in range(HP):
        halves = []
        for x in range(2):
            sh = s[rows[hp], x * LANES:(x + 1) * LANES] + c_bias
            m = jnp.maximum(jnp.max(sh, axis=-1, keepdims=True), MAX_FLOOR)
            e = jnp.exp2(sh - m)
            l = jnp.sum(e, axis=-1, keepdims=True)
            p = e * jnp.where(l > 0, 1.0 / l, 0.0)
            p_sum = p_sum + p
            halves.append(p.astype(BF16))
        ps.append(jnp.concatenate(halves, axis=1))
    cmp_out = _dot(jnp.concatenate(ps, axis=0), vc_ref[0, 0])
    for hp in range(HP):
        out_acc[rows[hp]] = cmp_out[rows[hp]] * gate_tiles[hp]

    w0 = pl.multiple_of(jnp.maximum(q0 - WINDOW, 0), TQ)
    w_bias = wbias_ref[jnp.minimum(qt, WINDOW // TQ)]
    kk = kwin_ref[0, 0, pl.ds(w0, WK), :]
    win_acc = []
    for qx, v_ref in zip(q_plain, (vwin_e_ref, vwin_o_ref)):
        s = _dot_tb(qx, kk)
        es = []
        for hp in range(HP):
            sh = s[rows[hp]] + w_bias
            es.append(jnp.exp2(sh - jnp.max(sh, axis=-1, keepdims=True)).astype(BF16))
        win_acc.append(_dot(jnp.concatenate(es, axis=0), v_ref[0, 0, pl.ds(w0, WK), :]))
    emit(2, win_acc)

    ni = lax.broadcasted_iota(jnp.int32, (LANES, LANES), 0)
    ci = lax.broadcasted_iota(jnp.int32, (LANES, LANES), 1)
    overlap_t = ((ci * CMP_STRIDE < ni * SEL_BLOCK + SEL_BLOCK)
                 & (ci * CMP_STRIDE + CMP_LEN - 1 >= ni * SEL_BLOCK) & (ni < n_blk))
    overlap_t = jnp.where(overlap_t, 1.0, 0.0).astype(BF16)
    p_hi = p_sum.astype(BF16)
    p_lo = (p_sum - p_hi.astype(F32)).astype(BF16)
    imp_t = (_dot_tb(overlap_t, p_hi) + _dot_tb(overlap_t, p_lo))[0:n_blk]
    blk_id = lax.broadcasted_iota(jnp.int32, (n_blk, TQ), 0)
    cur = (q0 + lax.broadcasted_iota(jnp.int32, (n_blk, TQ), 1)) // SEL_BLOCK
    forced = (blk_id == 0) | (blk_id == cur) | (blk_id == cur - 1)
    score = jnp.where(forced, FORCE_SCORE, jnp.where(blk_id <= cur, imp_t, -1.0))
    rank = jnp.zeros((n_blk, TQ), jnp.int32)
    for mblk in range(n_blk):
        other = score[mblk:mblk + 1, :]
        ahead = (other > score) | ((other == score) & (blk_id > mblk))
        rank = rank + jnp.where(ahead, 1, 0)
    not_sel_t = jnp.where(rank < SEL_TOPN, 0.0, 1.0)
    padded = jnp.concatenate([jnp.zeros((NOTSEL_LANE0, TQ), F32), not_sel_t,
                              jnp.zeros((LANES - NOTSEL_LANE0 - n_blk, TQ), F32)], axis=0)
    not_sel = jnp.concatenate([padded.T.astype(BF16)] * HP, axis=0)
    q_aug = tuple(qx + not_sel for qx in q_plain)

    for ref in (acc_e, acc_o):
        ref[...] = jnp.zeros_like(ref)
    for ref in (m_e, m_o):
        ref[...] = jnp.full_like(ref, MASK_BIAS)

    last_k0 = ksel_ref.shape[2] - KT

    def even_scores(kt):
        k0 = pl.multiple_of(jnp.minimum(kt * KT, last_k0), KT)
        return _dot_tb(q_aug[0], ksel_ref[0, 0, pl.ds(k0, KT), :])

    def softmax_pv(get_scores, bias, vv, acc, m_ref):
        es, alphas = [], []
        for hp in range(HP):
            sh = get_scores(hp) + bias
            tiles = [sh[:, j * LANES:(j + 1) * LANES] for j in range(bias.shape[1] // LANES)]
            m_old = m_ref[rows[hp]]
            m_new = jnp.maximum(m_old, jnp.max(functools.reduce(jnp.maximum, tiles), axis=-1, keepdims=True))
            m_ref[rows[hp]] = m_new
            alphas.append(jnp.exp2(m_old - m_new))
            es.append(jnp.concatenate([jnp.exp2(t - m_new).astype(BF16) for t in tiles], axis=1))
        pv = _dot(jnp.concatenate(es, axis=0), vv)
        for hp in range(HP):
            acc[rows[hp]] = alphas[hp] * acc[rows[hp]] + pv[rows[hp]]

    s_next[...] = even_scores(0)

    def sel_tile(kt):
        k0 = pl.multiple_of(kt * KT, KT)
        bias = cbias_ref[jnp.minimum(qt - kt * (KT // TQ), KT // TQ)]
        s_odd = _dot_tb(q_aug[1], ksel_ref[0, 0, pl.ds(k0, KT), :])
        softmax_pv(lambda hp: s_next[rows[hp], :], bias, vsel_e_ref[0, 0, pl.ds(k0, KT), :], acc_e, m_e)
        s_next[...] = even_scores(kt + 1)
        softmax_pv(lambda hp: s_odd[rows[hp]], bias, vsel_o_ref[0, 0, pl.ds(k0, KT), :], acc_o, m_o)

    n_whole = (q0 + TQ) // KT

    def tile_pair(i, carry):
        sel_tile(2 * i)
        sel_tile(2 * i + 1)
        return carry

    lax.fori_loop(0, n_whole // 2, tile_pair, 0)

    @pl.when(n_whole % 2 == 1)
    def _odd_tile():
        sel_tile(n_whole - 1)

    @pl.when(n_whole * KT < q0 + TQ)
    def _diagonal_remainder():
        k_rem = pl.multiple_of(q0, TQ)
        kk = ksel_ref[0, 0, pl.ds(k_rem, TQ), :]
        bias = cbias_ref[0][:, 0:TQ]
        for qx, v_ref, acc, m_ref in ((q_aug[0], vsel_e_ref, acc_e, m_e), (q_aug[1], vsel_o_ref, acc_o, m_o)):
            s = _dot_tb(qx, kk)
            softmax_pv(lambda hp: s[rows[hp]], bias, v_ref[0, 0, pl.ds(k_rem, TQ), :], acc, m_ref)

    emit(1, (acc_e, acc_o))

    o_ref[0] = jnp.concatenate([out_acc[rows[hp]] for hp in range(HP)], axis=1).astype(BF16)


def _nsa(nq, gates, kv, kc, vc):
    B, S, _ = nq.shape
    G = NSA_GROUPS
    TQ = NSA_TQ
    assert S // SEL_BLOCK <= LANES - NOTSEL_LANE0 and S % NSA_KT == 0 and S >= NSA_WK
    assert NSA_KT == 2 * TQ and WINDOW % TQ == 0
    rows = HEAD_PAIRS * TQ
    i = jnp.arange(TQ, dtype=jnp.int32)[None, :, None]
    off = lambda n: jnp.arange(n + 1, dtype=jnp.int32)[:, None, None] * TQ
    j = jnp.arange(NSA_WK, dtype=jnp.int32)[None, None, :]
    t_rel = off(WINDOW // TQ) + i
    win_bias = jnp.where((j <= t_rel) & (j > t_rel - WINDOW), 0.0, MASK_BIAS).astype(F32)
    j = jnp.arange(NSA_KT, dtype=jnp.int32)[None, None, :]
    causal_bias = jnp.where(j <= off(NSA_KT // TQ) + i, 0.0, MASK_BIAS).astype(F32)
    qblk = pl.BlockSpec((1, TQ, Q_STACKS * NSA_GQ_W), lambda b, g, t: (b, t, g))
    oblk = pl.BlockSpec((1, TQ, NSA_GQ_W), lambda b, g, t: (b, t, g))
    gblk = pl.BlockSpec((1, TQ, LANES), lambda b, g, t: (b, t, g))
    kvblk = lambda j: pl.BlockSpec((1, 1, S, LANES), lambda b, g, t: (g, b, 0, j))
    cblk = pl.BlockSpec((1, 1, 2 * LANES, LANES), lambda b, g, t: (b, g, 0, 0))
    wide = pltpu.VMEM((rows, LANES), F32)
    return pl.pallas_call(
        _nsa_kernel,
        grid=(B, G, S // TQ),
        in_specs=([qblk, gblk] + [kvblk(j) for j in range(KV_TILES)] + [cblk, cblk]
                  + [_resident(win_bias.shape), _resident(causal_bias.shape)]),
        out_specs=oblk,
        out_shape=jax.ShapeDtypeStruct((B, S, NSA_Q_W), BF16),
        scratch_shapes=[wide] * 5 + [pltpu.VMEM((rows, NSA_KT), F32),
                                     pltpu.VMEM((3 * NSA_HPG // 2, TQ, LANES), F32)],
        compiler_params=pltpu.CompilerParams(dimension_semantics=("arbitrary",) * 3,
                                             vmem_limit_bytes=VMEM_LIMIT),
        name="nsa",
    )(nq, gates, *([kv] * KV_TILES), kc, vc, win_bias, causal_bias)


def _tail_kernel(final, x_ref, yr_ref, yn_ref, p_ref, gmix_ref, gmlp_ref, gple_ref, gfin_ref,
                 wmg_ref, wro_ref, wno_ref, wout_ref, wup_ref, wdn_ref, wpg_ref, wpp_ref, o_ref):
    x = x_ref[...]
    h = _rms(x, gmix_ref[...]).astype(BF16)
    o_ret = _dot(yr_ref[...], wro_ref[...])
    o_nsa = _dot(yn_ref[...], wno_ref[...])
    g_ret = jax.nn.sigmoid(_dot(h, wmg_ref[:, 0:D_MODEL]))
    g_nsa = jax.nn.sigmoid(_dot(h, wmg_ref[:, D_MODEL:2 * D_MODEL]))
    mix = (g_ret * o_ret + g_nsa * o_nsa).astype(BF16)
    x = x + _dot(mix, wout_ref[...])
    h2 = _rms(x, gmlp_ref[...]).astype(BF16)
    mlp = jnp.zeros_like(x)
    step = 1024
    for c0 in range(0, MLP_HIDDEN, step):
        up = jnp.maximum(_dot(h2, wup_ref[:, c0:c0 + step]), 0.0)
        mlp = mlp + _dot((up * up).astype(BF16), wdn_ref[c0:c0 + step, :])
    x = x + mlp
    h3 = _rms(x, gple_ref[...]).astype(BF16)
    ple_gate = jax.nn.sigmoid(_dot(h3, wpg_ref[...]))
    x = x + _dot(p_ref[...].astype(BF16), wpp_ref[...]) * ple_gate
    if final:
        x = _rms(x, gfin_ref[...])
    o_ref[...] = x


def _tail(final, x2, y_ret, y_nsa, p2, g_mix, g_mlp, g_ple, g_fin, w_mg, w_ro, w_no, w_out, w_up, w_dn,
          w_pg, w_pp):
    T = x2.shape[0]
    tm = TAIL_TM
    row = lambda w: pl.BlockSpec((tm, w), lambda i: (i, 0))
    gains = [_resident((1, D_MODEL))] * 4
    weights = [_resident(w.shape) for w in (w_mg, w_ro, w_no, w_out, w_up, w_dn, w_pg, w_pp)]
    return pl.pallas_call(
        functools.partial(_tail_kernel, final),
        grid=(T // tm,),
        in_specs=[row(D_MODEL), row(RET_V_W), row(NSA_Q_W), row(PLE_DIM)] + gains + weights,
        out_specs=row(D_MODEL),
        out_shape=jax.ShapeDtypeStruct((T, D_MODEL), F32),
        compiler_params=pltpu.CompilerParams(dimension_semantics=("arbitrary",),
                                             vmem_limit_bytes=VMEM_LIMIT),
        name="tail",
    )(x2, y_ret, y_nsa, p2, g_mix, g_mlp, g_ple, g_fin, w_mg, w_ro, w_no, w_out, w_up, w_dn, w_pg, w_pp)


def _pack_w_in(w):
    gate = w[:, C_GATE:C_GATE + 3 * NSA_HEADS]
    parts = [w[:, :C_GATE]]
    for g in range(NSA_GROUPS):
        cols = [j * NSA_HEADS + g * NSA_HPG + r for j in range(3) for r in range(NSA_HPG)]
        parts.append(jnp.pad(gate[:, jnp.array(cols)], ((0, 0), (0, LANES - len(cols)))))
    return jnp.concatenate(parts, axis=1).astype(BF16)


def kernel(x, p, positions, norm_mix_g, w_in, ret_gn_g, w_ret_o, cmp_pe_k, cmp_k_w1, cmp_k_w2, cmp_pe_v, cmp_v_w1, cmp_v_w2, w_nsa_o, w_merge_gate, w_out, norm_mlp_g, w_mlp_up, w_mlp_down, norm_ple_g, w_ple_gate, w_ple_proj, norm_final_g):
    B, S, D = x.shape
    depth = p.shape[0]
    T = B * S
    G = NSA_GROUPS
    bf = lambda a: a.astype(BF16)
    row = lambda a: a.reshape(1, -1)
    posf = positions.reshape(T, 1).astype(F32)
    inv_r = ROPE_THETA ** (-jnp.arange(0, RET_DK, 2, dtype=F32) / RET_DK)
    inv_n = ROPE_THETA ** (-jnp.arange(0, NSA_DH, 2, dtype=F32) / NSA_DH)
    inv_freq = jnp.concatenate([inv_r, inv_n, jnp.zeros((LANES - inv_r.shape[0] - inv_n.shape[0],), F32)])
    inv_freq = inv_freq.reshape(1, LANES)
    n_rows = S // CMP_STRIDE

    def strides(t):
        t = t.reshape(B, S, G, NSA_DH).transpose(0, 2, 1, 3)
        return t.reshape(B, G, n_rows, CMP_STRIDE * NSA_DH)

    def pe_rows(pe):
        return jnp.broadcast_to(bf(pe).reshape(1, -1), (8, CMP_LEN * NSA_DH))

    def swap_pairs(w):
        return w.reshape(NSA_HEADS // 2, 2, NSA_DH, -1)[:, ::-1].reshape(w.shape)

    def widen(w, outer):
        z = jnp.zeros_like(w)
        return bf(jnp.concatenate([w, z, z, w] if outer else [z, w, w, z], axis=1))

    x2 = x.reshape(T, D)
    for i in range(depth):
        y_ret, nq, cmp_kv, kv, gates = _proj(S, x2, posf, row(norm_mix_g[i]), inv_freq, row(ret_gn_g[i]),
                                              _pack_w_in(w_in[i]))
        kc, vc = _compress(strides(cmp_kv[:, :LANES]), strides(cmp_kv[:, LANES:]),
                           bf(cmp_k_w1[i]), widen(cmp_k_w2[i], True), pe_rows(cmp_pe_k[i]),
                           bf(cmp_v_w1[i]), widen(cmp_v_w2[i], False), pe_rows(cmp_pe_v[i]))
        sh3 = lambda a: a.reshape(B, S, a.shape[-1])
        y_nsa = _nsa(sh3(nq), sh3(gates), kv.reshape(G, B, S, kv.shape[-1]), kc, vc)
        x2 = _tail(i == depth - 1, x2, y_ret, y_nsa.reshape(T, NSA_Q_W),
                   p[i].reshape(T, PLE_DIM), row(norm_mix_g[i]), row(norm_mlp_g[i]), row(norm_ple_g[i]),
                   row(norm_final_g), bf(w_merge_gate[i]), bf(w_ret_o[i]), bf(swap_pairs(w_nsa_o[i])), bf(w_out[i]),
                   bf(w_mlp_up[i]), bf(w_mlp_down[i]), bf(w_ple_gate[i]), bf(w_ple_proj[i]))
    return x2.reshape(B, S, D)
```
